```python
import math
import jax, jax.numpy as jnp
from jax import lax
import numpy as np

D_MODEL = 4096
BATCH = 4
SEQ = 4096
DEPTH = 1

N_MEM = 256
EPS = 1e-6
ROPE_THETA = 500000.0
MIX_WIDTH = D_MODEL
HEAD_DIM = 128
ROT_DIM = HEAD_DIM // 4
GDN_DK = 128
GDN_DV = 128
GDN_HEADS = (MIX_WIDTH // 2) // GDN_DV
GDN_CONV = 4
GDN_CHUNK = 64
GDN_QK = GDN_HEADS * GDN_DK
GDN_V = GDN_HEADS * GDN_DV
ATT_HEADS = (MIX_WIDTH // 2) // HEAD_DIM
ATT_KV_HEADS = 4
IDX_HEADS = 32
IDX_DIM = 128
IDX_ROT = IDX_DIM // 4
TOPK_MAX = 256
Q_BLOCK = 128
MEM_HEADS = 4
MEM_HEAD_DIM = 128
MEM_WIDTH = MEM_HEADS * MEM_HEAD_DIM
FFN_HIDDEN = -(-8 * D_MODEL // (3 * 256)) * 256
IN_SIZES = (GDN_QK, GDN_QK, GDN_V, GDN_V, GDN_HEADS, GDN_HEADS,
            ATT_HEADS * HEAD_DIM, ATT_KV_HEADS * HEAD_DIM, ATT_KV_HEADS * HEAD_DIM,
            IDX_HEADS * IDX_DIM, IDX_DIM, IDX_HEADS)
D_IN = sum(IN_SIZES)

kernel_name = "hybrid_gdn_dsa_parallel_heads"


def rms_norm(x, g):
    xf = x.astype(jnp.float32)
    y = xf * lax.rsqrt(jnp.mean(xf * xf, axis=-1, keepdims=True) + EPS)
    return (y * g.astype(jnp.float32)).astype(x.dtype)


def l2_norm(x):
    xf = x.astype(jnp.float32)
    return xf * lax.rsqrt(jnp.sum(xf * xf, axis=-1, keepdims=True) + EPS)


def partial_rotary(x, pos, rot_dim):
    half = rot_dim // 2
    inv_freq = ROPE_THETA ** (-jnp.arange(half, dtype=jnp.float32) * 2.0 / rot_dim)
    ang = pos.astype(jnp.float32)[:, None] * inv_freq[None, :]
    cos = jnp.cos(ang)[:, None, :]
    sin = jnp.sin(ang)[:, None, :]
    xf = x.astype(jnp.float32)
    x1, x2 = xf[..., :half], xf[..., half:rot_dim]
    out = jnp.concatenate([x1 * cos - x2 * sin, x2 * cos + x1 * sin, xf[..., rot_dim:]], axis=-1)
    return out.astype(x.dtype)


def causal_depthwise_conv(x, w):
    K = w.shape[0]
    return lax.conv_general_dilated(x, w[:, None, :], window_strides=(1,), padding=[(K - 1, 0)],
                                    dimension_numbers=('NWC', 'WIO', 'NWC'),
                                    feature_group_count=x.shape[-1])


def gated_delta_rule_chunked(q, k, v, g, beta):
    B, S, H, dk = q.shape
    dv = v.shape[-1]
    C = GDN_CHUNK
    N = S // C

    def chunks(t):
        t = t.astype(jnp.float32).reshape((B, N, C) + t.shape[2:])
        return jnp.moveaxis(t, 2, 3)

    q, k, v, g, beta = chunks(q), chunks(k), chunks(v), chunks(g), chunks(beta)
    g = jnp.cumsum(g, axis=-1)
    tri_incl = jnp.tril(jnp.ones((C, C), dtype=bool))
    tri_strict = jnp.tril(jnp.ones((C, C), dtype=bool), -1)
    decay = jnp.exp(jnp.where(tri_incl, g[..., :, None] - g[..., None, :], -jnp.inf))
    k_beta = k * beta[..., None]
    v_beta = v * beta[..., None]
    lower = jnp.where(tri_strict, jnp.einsum('bnhcd,bnhed->bnhce', k_beta, k) * decay, 0.0)
    a_mat = lower + jnp.eye(C, dtype=jnp.float32)
    rhs = jnp.concatenate([v_beta, k_beta * jnp.exp(g)[..., None]], axis=-1)
    sol = lax.linalg.triangular_solve(a_mat, rhs, left_side=True, lower=True, unit_diagonal=True)
    u, w = sol[..., :dv], sol[..., dv:]
    qk = jnp.einsum('bnhcd,bnhed->bnhce', q, k) * decay
    q_dec = q * jnp.exp(g)[..., None]
    g_last = g[..., -1]
    k_dec = k * jnp.exp(g_last[..., None] - g)[..., None]

    def step(state, xs):
        u_c, w_c, qk_c, qd_c, kd_c, gl_c = xs
        v_new = u_c - jnp.einsum('bhcd,bhdv->bhcv', w_c, state)
        o = jnp.einsum('bhcd,bhdv->bhcv', qd_c, state) + jnp.einsum('bhce,bhev->bhcv', qk_c, v_new)
        state = state * jnp.exp(gl_c)[..., None, None] + jnp.einsum('bhcd,bhcv->bhdv', kd_c, v_new)
        return state, o

    xs = tuple(jnp.moveaxis(t, 1, 0) for t in (u, w, qk, q_dec, k_dec, g_last))
    state0 = jnp.zeros((B, H, dk, dv), jnp.float32)
    _, o = lax.scan(step, state0, xs)
    return o.transpose(1, 0, 3, 2, 4).reshape(B, S, H, dv)


def dsa_attention(q, k, v, q_idx, k_idx, w_idx, topk):
    B, S, H, d = q.shape
    G = H // ATT_KV_HEADS
    nblk = S // Q_BLOCK
    scale = d ** -0.5
    idx_scale = IDX_DIM ** -0.5
    key_pos = jnp.arange(S)
    k_idx_f = k_idx.astype(jnp.float32)
    gather = jax.vmap(lambda src, ix: src[ix])

    def blocks(t):
        return jnp.moveaxis(t.reshape((B, nblk, Q_BLOCK) + t.shape[2:]), 1, 0)

    def one_block(args):
        blk, q_blk, qi_blk, wi_blk = args
        t = blk * Q_BLOCK + jnp.arange(Q_BLOCK)
        logits = jnp.einsum('bqhd,bsd->bqhs', qi_blk.astype(jnp.float32), k_idx_f) * idx_scale
        score = jnp.einsum('bqhs,bqh->bqs', jax.nn.relu(logits), wi_blk.astype(jnp.float32))
        causal = key_pos[None, :] <= t[:, None]
        score = jnp.where(causal[None], score, -jnp.inf)
        _, sel = lax.top_k(score, topk)
        valid = sel <= t[None, :, None]
        k_sel = gather(k, sel).astype(jnp.float32)
        v_sel = gather(v, sel).astype(jnp.float32)
        s = jnp.einsum('bqhgd,bqkhd->bqhgk', q_blk.astype(jnp.float32), k_sel) * scale
        s = jnp.where(valid[:, :, None, None, :], s, -jnp.inf)
        p = jax.nn.softmax(s, axis=-1)
        return jnp.einsum('bqhgk,bqkhd->bqhgd', p, v_sel).astype(q.dtype)

    qg = q.reshape(B, S, ATT_KV_HEADS, G, d)
    out = lax.map(one_block, (jnp.arange(nblk), blocks(qg), blocks(q_idx), blocks(w_idx)))
    return jnp.moveaxis(out, 0, 1).reshape(B, S, H * d)


def setup_inputs(seed: int = 0) -> dict:
    key = jax.random.key(seed)
    ks = jax.random.split(key, 24)
    f32 = jnp.float32
    L = DEPTH

    def dense(k, fan_in, shape):
        return jax.random.normal(k, shape, f32) * fan_in ** -0.5

    def gain(k, shape):
        return 1.0 + 0.02 * jax.random.normal(k, shape, f32)

    dt = jnp.exp(jax.random.uniform(ks[6], (L, GDN_HEADS), f32, math.log(1e-3), math.log(1e-1)))
    return {
        "x": jax.random.normal(ks[0], (BATCH, SEQ, D_MODEL), f32),
        "mem": jax.random.normal(ks[1], (BATCH, N_MEM, D_MODEL), f32),
        "norm_mix": gain(ks[2], (L, D_MODEL)),
        "w_in": dense(ks[3], D_MODEL, (L, D_MODEL, D_IN)),
        "conv_w": dense(ks[4], GDN_CONV, (L, GDN_CONV, 2 * GDN_QK + GDN_V)),
        "a_log": jnp.log(jax.random.uniform(ks[5], (L, GDN_HEADS), f32, 1.0, 16.0)),
        "dt_bias": dt + jnp.log(-jnp.expm1(-dt)),
        "gdn_norm": gain(ks[7], (L, GDN_DV)),
        "att_q_norm": gain(ks[8], (L, HEAD_DIM)),
        "att_k_norm": gain(ks[9], (L, HEAD_DIM)),
        "w_out": dense(ks[10], MIX_WIDTH, (L, MIX_WIDTH, D_MODEL)),
        "norm_mem_q": gain(ks[11], (L, D_MODEL)),
        "norm_mem_kv": gain(ks[12], (L, D_MODEL)),
        "w_mem_q": dense(ks[13], D_MODEL, (L, D_MODEL, MEM_WIDTH)),
        "w_mem_kv": dense(ks[14], D_MODEL, (L, D_MODEL, 2 * MEM_WIDTH)),
        "mem_q_norm": gain(ks[15], (L, MEM_HEAD_DIM)),
        "mem_k_norm": gain(ks[16], (L, MEM_HEAD_DIM)),
        "w_mem_o": dense(ks[17], MEM_WIDTH, (L, MEM_WIDTH, D_MODEL)),
        "norm_ffn": gain(ks[18], (L, D_MODEL)),
        "w_gate": dense(ks[19], D_MODEL, (L, D_MODEL, FFN_HIDDEN)),
        "w_up": dense(ks[20], D_MODEL, (L, D_MODEL, FFN_HIDDEN)),
        "w_down": dense(ks[21], FFN_HIDDEN, (L, FFN_HIDDEN, D_MODEL)),
    }


def reference(x, mem, norm_mix, w_in, conv_w, a_log, dt_bias, gdn_norm, att_q_norm, att_k_norm,
              w_out, norm_mem_q, norm_mem_kv, w_mem_q, w_mem_kv, mem_q_norm, mem_k_norm, w_mem_o,
              norm_ffn, w_gate, w_up, w_down):
    B, S, _ = x.shape
    M = mem.shape[1]
    topk = min(TOPK_MAX, S // 4)
    pos = jnp.arange(S)
    split_at = [int(i) for i in np.cumsum(IN_SIZES)[:-1]]
    for l in range(DEPTH):
        h = rms_norm(x, norm_mix[l])
        proj = h @ w_in[l]
        gq, gk, gv, gz, ga, gb, aq, ak, av, iq, ik, iw = jnp.split(proj, split_at, axis=-1)

        qkv = jax.nn.silu(causal_depthwise_conv(jnp.concatenate([gq, gk, gv], axis=-1), conv_w[l]))
        gq, gk, gv = jnp.split(qkv, [GDN_QK, 2 * GDN_QK], axis=-1)
        gq = l2_norm(gq.reshape(B, S, GDN_HEADS, GDN_DK)) * GDN_DK ** -0.5
        gk = l2_norm(gk.reshape(B, S, GDN_HEADS, GDN_DK))
        gv = gv.reshape(B, S, GDN_HEADS, GDN_DV)
        beta = jax.nn.sigmoid(gb.astype(jnp.float32))
        g = -jnp.exp(a_log[l].astype(jnp.float32)) * jax.nn.softplus(
            ga.astype(jnp.float32) + dt_bias[l].astype(jnp.float32))
        o_a = gated_delta_rule_chunked(gq, gk, gv, g, beta)
        o_a = rms_norm(o_a, gdn_norm[l]) * jax.nn.silu(
            gz.reshape(B, S, GDN_HEADS, GDN_DV).astype(jnp.float32))
        o_a = o_a.reshape(B, S, GDN_V).astype(x.dtype)

        aq = partial_rotary(rms_norm(aq.reshape(B, S, ATT_HEADS, HEAD_DIM), att_q_norm[l]), pos, ROT_DIM)
        ak = partial_rotary(rms_norm(ak.reshape(B, S, ATT_KV_HEADS, HEAD_DIM), att_k_norm[l]), pos, ROT_DIM)
        av = av.reshape(B, S, ATT_KV_HEADS, HEAD_DIM)
        iq = partial_rotary(iq.reshape(B, S, IDX_HEADS, IDX_DIM), pos, IDX_ROT)
        ik = partial_rotary(ik[:, :, None, :], pos, IDX_ROT)[:, :, 0]
        iw = iw * IDX_HEADS ** -0.5
        o_b = dsa_attention(aq, ak, av, iq, ik, iw, topk)

        x = x + jnp.concatenate([o_a, o_b], axis=-1) @ w_out[l]

        hq = rms_norm(x, norm_mem_q[l])
        hm = rms_norm(mem, norm_mem_kv[l])
        mq = rms_norm((hq @ w_mem_q[l]).reshape(B, S, MEM_HEADS, MEM_HEAD_DIM), mem_q_norm[l])
        mk, mv = jnp.split(hm @ w_mem_kv[l], 2, axis=-1)
        mk = rms_norm(mk.reshape(B, M, MEM_HEADS, MEM_HEAD_DIM), mem_k_norm[l])
        mv = mv.reshape(B, M, MEM_HEADS, MEM_HEAD_DIM)
        s = jnp.einsum('bshd,bmhd->bhsm', mq.astype(jnp.float32), mk.astype(jnp.float32)) * MEM_HEAD_DIM ** -0.5
        p = jax.nn.softmax(s, axis=-1)
        mo = jnp.einsum('bhsm,bmhd->bshd', p, mv.astype(jnp.float32)).reshape(B, S, MEM_WIDTH).astype(x.dtype)
        x = x + mo @ w_mem_o[l]

        hf = rms_norm(x, norm_ffn[l])
        x = x + (jax.nn.silu(hf @ w_gate[l]) * (hf @ w_up[l])) @ w_down[l]
    return x
```

```python
import functools
import math

import jax
import jax.numpy as jnp
import numpy as np
from jax import lax
from jax.experimental import pallas as pl
from jax.experimental.pallas import tpu as pltpu

D_MODEL = 4096
ATT_KV_HEADS = 4
IDX_HEADS = 32
TOPK_MAX = 256

EPS = 1e-6
ROPE_THETA = 500000.0
HEAD_DIM = 128
ROT_DIM = HEAD_DIM // 4
GDN_DK = 128
GDN_DV = 128
GDN_HEADS = (D_MODEL // 2) // GDN_DV
GDN_CONV = 4
GDN_CHUNK = 64
GDN_QK = GDN_HEADS * GDN_DK
GDN_V = GDN_HEADS * GDN_DV
ATT_HEADS = (D_MODEL // 2) // HEAD_DIM
ATT_GROUP = ATT_HEADS // ATT_KV_HEADS
IDX_DIM = 128
IDX_ROT = IDX_DIM // 4
Q_BLOCK = 128
MEM_HEADS = 4
MEM_HEAD_DIM = 128
MEM_WIDTH = MEM_HEADS * MEM_HEAD_DIM

LANES = 128
VMEM_LIMIT = 56 * 1024 * 1024
NEG_BIG = -1e30

IQ_W = IDX_HEADS * IDX_DIM
OFF_IQ = 0
OFF_GQ = OFF_IQ + IQ_W
OFF_GK = OFF_GQ + GDN_QK
OFF_GV = OFF_GK + GDN_QK
OFF_GZ = OFF_GV + GDN_V
OFF_AQ = OFF_GZ + GDN_V
OFF_AK = OFF_AQ + ATT_HEADS * HEAD_DIM
OFF_AV = OFF_AK + ATT_KV_HEADS * HEAD_DIM
OFF_IK = OFF_AV + ATT_KV_HEADS * HEAD_DIM
N_MAIN = OFF_IK + IDX_DIM
IN_TN = 512
N_MAIN_PAD = -(-N_MAIN // IN_TN) * IN_TN
SM_GA = 0
SM_GB = GDN_HEADS
SM_IW = 2 * GDN_HEADS
HI = lax.Precision.HIGHEST


def _params(sem):
    return pltpu.CompilerParams(dimension_semantics=sem, vmem_limit_bytes=VMEM_LIMIT)


def _pick(n, pref):
    t = min(n, pref)
    while n % t:
        t //= 2
    return t


def _sigmoid(x):
    return 1.0 / (1.0 + jnp.exp(-x))


def _rmsnorm_kernel(x_ref, g_ref, o_ref):
    x = x_ref[...]
    ms = jnp.mean(x * x, axis=-1, keepdims=True)
    o_ref[...] = (x * lax.rsqrt(ms + EPS) * g_ref[...]).astype(o_ref.dtype)


def _rmsnorm_side_kernel(x_ref, g_ref, w_ref, o_ref, s_ref):
    x = x_ref[...]
    ms = jnp.mean(x * x, axis=-1, keepdims=True)
    xn = x * lax.rsqrt(ms + EPS) * g_ref[...]
    o_ref[...] = xn.astype(o_ref.dtype)
    s_ref[...] = jnp.dot(xn, w_ref[...], precision=HI, preferred_element_type=jnp.float32)


def _rmsnorm(x2d, g, w_side=None, tm=256):
    m, d = x2d.shape
    tm = _pick(m, tm)
    g2 = g.reshape(1, d).astype(jnp.float32)
    xs = pl.BlockSpec((tm, d), lambda i: (i, 0))
    gs = pl.BlockSpec((1, d), lambda i: (0, 0))
    if w_side is None:
        return pl.pallas_call(
            _rmsnorm_kernel, grid=(m // tm,), in_specs=[xs, gs], out_specs=xs,
            out_shape=jax.ShapeDtypeStruct((m, d), jnp.bfloat16),
            compiler_params=_params(("parallel",)), name="rmsnorm")(x2d, g2)
    ns = w_side.shape[1]
    return pl.pallas_call(
        _rmsnorm_side_kernel, grid=(m // tm,),
        in_specs=[xs, gs, pl.BlockSpec((d, ns), lambda i: (0, 0))],
        out_specs=[xs, pl.BlockSpec((tm, ns), lambda i: (i, 0))],
        out_shape=[jax.ShapeDtypeStruct((m, d), jnp.bfloat16),
                   jax.ShapeDtypeStruct((m, ns), jnp.float32)],
        compiler_params=_params(("parallel",)), name="rmsnorm_side")(x2d, g2, w_side)


def _matmul_kernel(*refs, n_pairs, has_res):
    o_ref = refs[-1]
    acc = None
    for p in range(n_pairs):
        d = jnp.dot(refs[2 * p][...], refs[2 * p + 1][...], preferred_element_type=jnp.float32)
        acc = d if acc is None else acc + d
    if has_res:
        acc = acc + refs[2 * n_pairs][...]
    o_ref[...] = acc.astype(o_ref.dtype)


def _matmul(pairs, res=None, out_dtype=jnp.float32, tm=1024, tn=512, name="matmul"):
    m = pairs[0][0].shape[0]
    n = pairs[0][1].shape[1]
    tm, tn = _pick(m, tm), _pick(n, tn)
    args, specs = [], []
    for a, w in pairs:
        k = a.shape[1]
        args += [a, w]
        specs += [pl.BlockSpec((tm, k), lambda i, j: (i, 0)), pl.BlockSpec((k, tn), lambda i, j: (0, j))]
    if res is not None:
        args.append(res)
        specs.append(pl.BlockSpec((tm, tn), lambda i, j: (i, j)))
    return pl.pallas_call(
        functools.partial(_matmul_kernel, n_pairs=len(pairs), has_res=res is not None),
        grid=(m // tm, n // tn), in_specs=specs,
        out_specs=pl.BlockSpec((tm, tn), lambda i, j: (i, j)),
        out_shape=jax.ShapeDtypeStruct((m, n), out_dtype),
        compiler_params=_params(("parallel", "arbitrary")), name=name)(*args)


def _gateup_kernel(a_ref, wg_ref, wu_ref, o_ref):
    a = a_ref[...]
    g = jnp.dot(a, wg_ref[...], preferred_element_type=jnp.float32)
    u = jnp.dot(a, wu_ref[...], preferred_element_type=jnp.float32)
    o_ref[...] = (g * _sigmoid(g) * u).astype(o_ref.dtype)


def _gateup(a, wg, wu, tm=1024, tn=256):
    m, k = a.shape
    n = wg.shape[1]
    tm, tn = _pick(m, tm), _pick(n, tn)
    ws = pl.BlockSpec((k, tn), lambda i, j: (0, j))
    return pl.pallas_call(
        _gateup_kernel, grid=(m // tm, n // tn),
        in_specs=[pl.BlockSpec((tm, k), lambda i, j: (i, 0)), ws, ws],
        out_specs=pl.BlockSpec((tm, tn), lambda i, j: (i, j)),
        out_shape=jax.ShapeDtypeStruct((m, n), jnp.bfloat16),
        compiler_params=_params(("parallel", "arbitrary")), name="ffn_gateup")(a, wg, wu)


HALO = 16


def _gdn_prep_kernel(x_ref, halo_ref, w_ref, o_ref, buf_ref, *, ts, tc, blocks_per_seq, nqk_blocks):
    i = pl.program_id(0)
    j = pl.program_id(1)
    first = (i % blocks_per_seq) == 0
    halo = halo_ref[...].astype(jnp.float32)
    buf_ref[0:HALO, :] = jnp.where(first, 0.0, halo)
    buf_ref[HALO:HALO + ts, :] = x_ref[...].astype(jnp.float32)
    w = w_ref[...]
    y = None
    for k in range(GDN_CONV):
        off = HALO - (GDN_CONV - 1) + k
        term = buf_ref[off:off + ts, :] * w[k:k + 1, :]
        y = term if y is None else y + term
    y = y * _sigmoid(y)
    is_q = j < nqk_blocks
    is_v = j >= 2 * nqk_blocks
    for h in range(tc // GDN_DK):
        yh = y[:, h * GDN_DK:(h + 1) * GDN_DK]
        rs = lax.rsqrt(jnp.sum(yh * yh, axis=-1, keepdims=True) + EPS)
        rs = rs * jnp.where(is_q, GDN_DK ** -0.5, 1.0)
        fac = jnp.where(is_v, 1.0, rs)
        o_ref[:, h * GDN_DK:(h + 1) * GDN_DK] = (yh * fac).astype(o_ref.dtype)


def _gdn_prep(proj, conv_w, seq):
    t = proj.shape[0]
    ts = _pick(seq, 512)
    tc = 512
    nconv = (2 * GDN_QK + GDN_V) // tc
    col0 = OFF_GQ // tc
    rb = ts // HALO
    kern = functools.partial(_gdn_prep_kernel, ts=ts, tc=tc, blocks_per_seq=seq // ts,
                             nqk_blocks=GDN_QK // tc)
    return pl.pallas_call(
        kern, grid=(t // ts, nconv),
        in_specs=[pl.BlockSpec((ts, tc), lambda i, j: (i, col0 + j)),
                  pl.BlockSpec((HALO, tc), lambda i, j: (jnp.maximum(i * rb - 1, 0), col0 + j)),
                  pl.BlockSpec((GDN_CONV, tc), lambda i, j: (0, j))],
        out_specs=pl.BlockSpec((ts, tc), lambda i, j: (i, j)),
        out_shape=jax.ShapeDtypeStruct((t, nconv * tc), jnp.bfloat16),
        scratch_shapes=[pltpu.VMEM((HALO + ts, tc), jnp.float32)],
        compiler_params=_params(("parallel", "parallel")), name="gdn_prep")(proj, proj, conv_w)


def _gate_prep_kernel(s_ref, alog_ref, dtb_ref, o_ref, *, ts):
    s = s_ref[...]
    lane = lax.broadcasted_iota(jnp.int32, s.shape, 1)
    xa = s + dtb_ref[...]
    softplus = jnp.maximum(xa, 0.0) + jnp.log(1.0 + jnp.exp(-jnp.abs(xa)))
    g = -jnp.exp(alog_ref[...]) * softplus
    g = jnp.where(lane < GDN_HEADS, g, 0.0)
    r = lax.broadcasted_iota(jnp.int32, (ts, ts), 0)
    c = lax.broadcasted_iota(jnp.int32, (ts, ts), 1)
    tri = jnp.where((r // GDN_CHUNK == c // GDN_CHUNK) & (c <= r), 1.0, 0.0)
    gcum = jnp.dot(tri, g, precision=HI, preferred_element_type=jnp.float32)
    beta = _sigmoid(s)
    is_beta = (lane >= SM_GB) & (lane < SM_GB + GDN_HEADS)
    o_ref[...] = jnp.where(lane < GDN_HEADS, gcum, jnp.where(is_beta, beta, 0.0))


def _gate_prep(small, a_log, dt_bias):
    t, ns = small.shape
    ts = _pick(t, 256)
    pad = lambda v: jnp.zeros((1, ns), jnp.float32).at[0, :GDN_HEADS].set(v.astype(jnp.float32))
    vs = pl.BlockSpec((1, ns), lambda i: (0, 0))
    bs = pl.BlockSpec((ts, ns), lambda i: (i, 0))
    return pl.pallas_call(
        functools.partial(_gate_prep_kernel, ts=ts), grid=(t // ts,),
        in_specs=[bs, vs, vs], out_specs=bs,
        out_shape=jax.ShapeDtypeStruct((t, ns), jnp.float32),
        compiler_params=_params(("parallel",)), name="gate_prep")(small, pad(a_log), pad(dt_bias))


GDN_HB = 2


def _bdot(a, b):
    return jnp.dot(a.astype(jnp.bfloat16), b.astype(jnp.bfloat16), preferred_element_type=jnp.float32)


def _bdot_nt(a, b):
    return lax.dot_general(a.astype(jnp.bfloat16), b.astype(jnp.bfloat16), (((1,), (1,)), ((), ())),
                           preferred_element_type=jnp.float32)


def _bdot_tn(a, b):
    return lax.dot_general(a.astype(jnp.bfloat16), b.astype(jnp.bfloat16), (((0,), (0,)), ((), ())),
                           preferred_element_type=jnp.float32)


def _gdn_kernel(q_ref, k_ref, v_ref, z_ref, g_ref, gain_ref, o_ref, *, n_chunks):
    hg = pl.program_id(1)
    C = GDN_CHUNK
    lane = lax.broadcasted_iota(jnp.int32, (8, LANES), 1)
    r = lax.broadcasted_iota(jnp.int32, (C, C), 0)
    c = lax.broadcasted_iota(jnp.int32, (C, C), 1)
    tri_incl = c <= r
    tri_strict = c < r
    gain = gain_ref[...]

    def body(ci, states):
        r0 = pl.multiple_of(ci * C, C)
        gsl = g_ref[pl.ds(r0, C), :]
        new_states = []
        for hh in range(GDN_HB):
            h = hg * GDN_HB + hh
            sl = slice(hh * GDN_DK, (hh + 1) * GDN_DK)
            sel_g = jnp.where(lane == h, 1.0, 0.0)
            sel_b = jnp.where(lane == SM_GB + h, 1.0, 0.0)
            gc = jnp.sum(gsl * sel_g[0:1], axis=1, keepdims=True)
            beta = jnp.sum(gsl * sel_b[0:1], axis=1, keepdims=True)
            gc_row = lax.dot_general(sel_g, gsl, (((1,), (1,)), ((), ())), precision=HI,
                                     preferred_element_type=jnp.float32)[0:1]
            q = q_ref[pl.ds(r0, C), sl].astype(jnp.float32)
            k = k_ref[pl.ds(r0, C), sl].astype(jnp.float32)
            v = v_ref[pl.ds(r0, C), sl].astype(jnp.float32)
            decay = jnp.exp(jnp.where(tri_incl, gc - gc_row, NEG_BIG))
            kb = k * beta
            eg = jnp.exp(gc)
            low = jnp.where(tri_strict, _bdot_nt(kb, k) * decay, 0.0)
            x = jnp.concatenate([v * beta, kb * eg], axis=1)
            x = x - _bdot(low, x)
            pw = low
            for _ in range(int(math.log2(C)) - 1):
                pw = _bdot(pw, pw)
                x = x + _bdot(pw, x)
            u, w = x[:, :GDN_DV], x[:, GDN_DV:]
            qk = _bdot_nt(q, k) * decay
            g_last = gc[C - 1:C, :]
            k_dec = k * jnp.exp(g_last - gc)
            state = states[hh]
            v_new = u - _bdot(w, state)
            o = _bdot(q * eg, state) + _bdot(qk, v_new)
            new_states.append(state * jnp.exp(g_last) + _bdot_tn(k_dec, v_new))
            z = z_ref[pl.ds(r0, C), sl].astype(jnp.float32)
            on = o * lax.rsqrt(jnp.mean(o * o, axis=-1, keepdims=True) + EPS) * gain
            o_ref[pl.ds(r0, C), sl] = (on * (z * _sigmoid(z))).astype(o_ref.dtype)
        return tuple(new_states)

    init = tuple(jnp.zeros((GDN_DK, GDN_DV), jnp.float32) for _ in range(GDN_HB))
    lax.fori_loop(0, n_chunks, body, init)


def _gdn(qkv, proj, gates, gdn_norm, batch, seq):
    hw = GDN_HB * GDN_DK
    ngrp = GDN_HEADS // GDN_HB
    kq = GDN_QK // hw
    zoff = OFF_GZ // hw
    blk = lambda off: pl.BlockSpec((seq, hw), lambda b, g: (b, off + g))
    return pl.pallas_call(
        functools.partial(_gdn_kernel, n_chunks=seq // GDN_CHUNK), grid=(batch, ngrp),
        in_specs=[blk(0), blk(kq), blk(2 * kq), blk(zoff),
                  pl.BlockSpec((seq, gates.shape[1]), lambda b, g: (b, 0)),
                  pl.BlockSpec((1, GDN_DV), lambda b, g: (0, 0))],
        out_specs=pl.BlockSpec((seq, hw), lambda b, g: (b, g)),
        out_shape=jax.ShapeDtypeStruct((batch * seq, GDN_V), jnp.bfloat16),
        compiler_params=_params(("parallel", "parallel")), name="gdn")(
            qkv, qkv, qkv, proj, gates, gdn_norm.reshape(1, GDN_DV).astype(jnp.float32))


def _rope(x, cosf, sin_lo, sin_hi, half):
    return x * cosf + pltpu.roll(x, LANES - half, 1) * sin_lo + pltpu.roll(x, half, 1) * sin_hi


def _dsa_prep_kernel(aq_ref, ak_ref, iq_ref, ik_ref, sm_ref, cos_ref, slo_ref, shi_ref, gq_ref, gk_ref,
                     q_out, k_out, iq_out, ik_out, w_out):
    cosf, slo, shi = cos_ref[...], slo_ref[...], shi_ref[...]
    half = ROT_DIM // 2

    def qk_norm(x, gain):
        return x * lax.rsqrt(jnp.mean(x * x, axis=-1, keepdims=True) + EPS) * gain

    for h in range(ATT_HEADS):
        x = aq_ref[:, h * HEAD_DIM:(h + 1) * HEAD_DIM].astype(jnp.float32)
        x = _rope(qk_norm(x, gq_ref[...]), cosf, slo, shi, half) * (HEAD_DIM ** -0.5)
        q_out[h * Q_BLOCK:(h + 1) * Q_BLOCK, :] = x.astype(q_out.dtype)
    for h in range(ATT_KV_HEADS):
        x = ak_ref[:, h * HEAD_DIM:(h + 1) * HEAD_DIM].astype(jnp.float32)
        x = _rope(qk_norm(x, gk_ref[...]), cosf, slo, shi, half)
        k_out[:, h * HEAD_DIM:(h + 1) * HEAD_DIM] = x.astype(k_out.dtype)
    for h in range(IDX_HEADS):
        x = iq_ref[:, h * IDX_DIM:(h + 1) * IDX_DIM].astype(jnp.float32)
        iq_out[h * Q_BLOCK:(h + 1) * Q_BLOCK, :] = _rope(x, cosf, slo, shi, half).astype(iq_out.dtype)
    ik_out[...] = _rope(ik_ref[...].astype(jnp.float32), cosf, slo, shi, half).astype(ik_out.dtype)
    w_out[...] = sm_ref[...] * (IDX_HEADS ** -0.5 * IDX_DIM ** -0.5)


def _rope_tables(seq):
    half = ROT_DIM // 2
    inv_freq = ROPE_THETA ** (-jnp.arange(half, dtype=jnp.float32) * 2.0 / ROT_DIM)
    ang = jnp.arange(seq, dtype=jnp.float32)[:, None] * inv_freq[None, :]
    cos, sin = jnp.cos(ang), jnp.sin(ang)
    z = jnp.zeros((seq, LANES - 2 * half), jnp.float32)
    zh = jnp.zeros((seq, half), jnp.float32)
    cosf = jnp.concatenate([cos, cos, jnp.ones_like(z)], axis=1)
    sin_lo = jnp.concatenate([-sin, zh, z], axis=1)
    sin_hi = jnp.concatenate([zh, sin, z], axis=1)
    return cosf, sin_lo, sin_hi


def _dsa_prep(proj, small, att_q_norm, att_k_norm, batch, seq):
    assert ROT_DIM == IDX_ROT
    t = batch * seq
    nq = seq // Q_BLOCK
    qb = Q_BLOCK
    cosf, slo, shi = _rope_tables(seq)
    tab = pl.BlockSpec((qb, LANES), lambda b, i: (i, 0))
    row = lambda w, off: pl.BlockSpec((qb, w), lambda b, i: (b * nq + i, off // w))
    vec = pl.BlockSpec((1, HEAD_DIM), lambda b, i: (0, 0))
    aq_w, ak_w = ATT_HEADS * HEAD_DIM, ATT_KV_HEADS * HEAD_DIM
    return pl.pallas_call(
        _dsa_prep_kernel, grid=(batch, nq),
        in_specs=[row(aq_w, OFF_AQ), row(ak_w, OFF_AK), row(IQ_W, OFF_IQ), row(IDX_DIM, OFF_IK),
                  pl.BlockSpec((qb, small.shape[1]), lambda b, i: (b * nq + i, 0)),
                  tab, tab, tab, vec, vec],
        out_specs=[pl.BlockSpec((ATT_HEADS * qb, HEAD_DIM), lambda b, i: (b * nq + i, 0)),
                   pl.BlockSpec((qb, ak_w), lambda b, i: (b * nq + i, 0)),
                   pl.BlockSpec((IDX_HEADS * qb, IDX_DIM), lambda b, i: (b * nq + i, 0)),
                   pl.BlockSpec((qb, IDX_DIM), lambda b, i: (b * nq + i, 0)),
                   pl.BlockSpec((qb, small.shape[1]), lambda b, i: (b * nq + i, 0))],
        out_shape=[jax.ShapeDtypeStruct((t * ATT_HEADS, HEAD_DIM), jnp.bfloat16),
                   jax.ShapeDtypeStruct((t, ak_w), jnp.bfloat16),
                   jax.ShapeDtypeStruct((t * IDX_HEADS, IDX_DIM), jnp.bfloat16),
                   jax.ShapeDtypeStruct((t, IDX_DIM), jnp.bfloat16),
                   jax.ShapeDtypeStruct((t, small.shape[1]), jnp.float32)],
        compiler_params=_params(("parallel", "parallel")), name="dsa_prep")(
            proj, proj, proj, proj, small, cosf, slo, shi,
            att_q_norm.reshape(1, HEAD_DIM).astype(jnp.float32),
            att_k_norm.reshape(1, HEAD_DIM).astype(jnp.float32))


DSA_KB = 256


def _dsa_kernel(iq_ref, q_ref, w_ref, ik_ref, k_ref, v_ref, o_ref,
                key_ref, wb_ref, m_ref, l_ref, acc_ref, *, topk):
    qi = pl.program_id(1)
    QB, KB = Q_BLOCK, DSA_KB
    n_kb = (qi * QB + QB + KB - 1) // KB
    qpos = qi * QB + lax.broadcasted_iota(jnp.int32, (QB, KB), 0)
    kiota = lax.broadcasted_iota(jnp.int32, (QB, KB), 1)

    w = w_ref[...]
    for h in range(IDX_HEADS):
        wb_ref[h] = jnp.broadcast_to(w[:, SM_IW + h:SM_IW + h + 1], (QB, LANES))

    def score_block(kb, carry):
        k0 = pl.multiple_of(kb * KB, KB)
        ikb = ik_ref[pl.ds(k0, KB), :]

        def head(h, acc):
            r0 = pl.multiple_of(h * QB, QB)
            lg = lax.dot_general(iq_ref[pl.ds(r0, QB), :], ikb, (((1,), (1,)), ((), ())),
                                 preferred_element_type=jnp.float32)
            wb = wb_ref[h]
            wt = jnp.concatenate([wb] * (KB // LANES), axis=1)
            return acc + jnp.maximum(lg, 0.0) * wt

        sc = lax.fori_loop(0, IDX_HEADS, head, jnp.zeros((QB, KB), jnp.float32), unroll=2)
        sc = jnp.where(k0 + kiota <= qpos, sc, -jnp.inf)
        bits = pltpu.bitcast(sc, jnp.int32)
        key_ref[:, pl.ds(k0, KB)] = bits ^ ((bits >> 31) & 0x7FFFFFFF)
        return carry

    lax.fori_loop(0, n_kb, score_block, 0)

    def count_ge(cand):
        def blk(kb, cnt):
            k0 = pl.multiple_of(kb * KB, KB)
            kk = key_ref[:, pl.ds(k0, KB)]
            for g in range(KB // LANES):
                cnt = cnt + jnp.where(kk[:, g * LANES:(g + 1) * LANES] >= cand, 1, 0)
            return cnt
        cnt = lax.fori_loop(0, n_kb, blk, jnp.zeros((QB, LANES), jnp.int32))
        return jnp.sum(cnt, axis=1, keepdims=True)

    int_min = jnp.int32(-2 ** 31)
    thr0 = jnp.full((QB, 1), int_min, jnp.int32)
    thr0 = jnp.where(count_ge(jnp.zeros((QB, 1), jnp.int32)) >= topk, 0, thr0)

    def bit_step(i, thr):
        cand = thr | (jnp.int32(1) << (30 - i))
        return jnp.where(count_ge(cand) >= topk, cand, thr)

    thr = lax.fori_loop(0, 31, bit_step, thr0)

    G = ATT_GROUP
    m_ref[...] = jnp.full(m_ref.shape, NEG_BIG, jnp.float32)
    l_ref[...] = jnp.zeros(l_ref.shape, jnp.float32)
    acc_ref[...] = jnp.zeros(acc_ref.shape, jnp.float32)

    def attn_block(kb, carry):
        k0 = pl.multiple_of(kb * KB, KB)
        sel = (key_ref[:, pl.ds(k0, KB)] >= thr) & (k0 + kiota <= qpos)
        bias = jnp.where(sel, 0.0, NEG_BIG)
        bias_g = jnp.concatenate([bias] * G, axis=0)
        for hk in range(ATT_KV_HEADS):
            rows = slice(hk * G * QB, (hk + 1) * G * QB)
            cols = slice(hk * HEAD_DIM, (hk + 1) * HEAD_DIM)
            s = lax.dot_general(q_ref[rows, :], k_ref[pl.ds(k0, KB), cols], (((1,), (1,)), ((), ())),
                                preferred_element_type=jnp.float32) + bias_g
            m_old = m_ref[rows, :]
            m_new = jnp.maximum(m_old, jnp.max(s, axis=1, keepdims=True))
            alpha = jnp.exp(m_old - m_new)
            p = jnp.exp(s - m_new)
            l_ref[rows, :] = alpha * l_ref[rows, :] + jnp.sum(p, axis=1, keepdims=True)
            acc_ref[rows, :] = alpha * acc_ref[rows, :] + jnp.dot(
                p.astype(jnp.bfloat16), v_ref[pl.ds(k0, KB), cols], preferred_element_type=jnp.float32)
            m_ref[rows, :] = m_new
        return carry

    lax.fori_loop(0, n_kb, attn_block, 0)
    for h in range(ATT_HEADS):
        rows = slice(h * QB, (h + 1) * QB)
        o_ref[:, h * HEAD_DIM:(h + 1) * HEAD_DIM] = (acc_ref[rows, :] / l_ref[rows, :]).astype(o_ref.dtype)


def _dsa(iq_hm, q_hm, w_s, ik_rot, k_att, proj, batch, seq, topk):
    nq = seq // Q_BLOCK
    qb = Q_BLOCK
    kvw = ATT_KV_HEADS * HEAD_DIM
    return pl.pallas_call(
        functools.partial(_dsa_kernel, topk=topk), grid=(batch, nq),
        in_specs=[pl.BlockSpec((IDX_HEADS * qb, IDX_DIM), lambda b, i: (b * nq + i, 0)),
                  pl.BlockSpec((ATT_HEADS * qb, HEAD_DIM), lambda b, i: (b * nq + i, 0)),
                  pl.BlockSpec((qb, w_s.shape[1]), lambda b, i: (b * nq + i, 0)),
                  pl.BlockSpec((seq, IDX_DIM), lambda b, i: (b, 0)),
                  pl.BlockSpec((seq, kvw), lambda b, i: (b, 0)),
                  pl.BlockSpec((seq, kvw), lambda b, i: (b, OFF_AV // kvw))],
        out_specs=pl.BlockSpec((qb, ATT_HEADS * HEAD_DIM), lambda b, i: (b * nq + i, 0)),
        out_shape=jax.ShapeDtypeStruct((batch * seq, ATT_HEADS * HEAD_DIM), jnp.bfloat16),
        scratch_shapes=[pltpu.VMEM((qb, seq), jnp.int32),
                        pltpu.VMEM((IDX_HEADS, qb, LANES), jnp.float32),
                        pltpu.VMEM((ATT_HEADS * qb, 1), jnp.float32),
                        pltpu.VMEM((ATT_HEADS * qb, 1), jnp.float32),
                        pltpu.VMEM((ATT_HEADS * qb, HEAD_DIM), jnp.float32)],
        compiler_params=_params(("parallel", "arbitrary")), name="dsa")(
            iq_hm, q_hm, w_s, ik_rot, k_att, proj)


def _mem_attn_kernel(q_ref, kv_ref, gq_ref, gk_ref, o_ref):
    def norm(x, gain):
        return x * lax.rsqrt(jnp.mean(x * x, axis=-1, keepdims=True) + EPS) * gain

    for h in range(MEM_HEADS):
        cols = slice(h * MEM_HEAD_DIM, (h + 1) * MEM_HEAD_DIM)
        q = norm(q_ref[:, cols], gq_ref[...]) * (MEM_HEAD_DIM ** -0.5)
        k = norm(kv_ref[:, cols], gk_ref[...])
        v = kv_ref[:, MEM_WIDTH + h * MEM_HEAD_DIM:MEM_WIDTH + (h + 1) * MEM_HEAD_DIM]
        s = _bdot_nt(q, k)
        p = jnp.exp(s - jnp.max(s, axis=1, keepdims=True))
        o = _bdot(p, v) / jnp.sum(p, axis=1, keepdims=True)
        o_ref[:, cols] = o.astype(o_ref.dtype)


def _mem_attn(mq, mkv, gq, gk, batch, seq, n_mem):
    tq = _pick(seq, 512)
    nq = seq // tq
    vec = pl.BlockSpec((1, MEM_HEAD_DIM), lambda b, i: (0, 0))
    return pl.pallas_call(
        _mem_attn_kernel, grid=(batch, nq),
        in_specs=[pl.BlockSpec((tq, MEM_WIDTH), lambda b, i: (b * nq + i, 0)),
                  pl.BlockSpec((n_mem, 2 * MEM_WIDTH), lambda b, i: (b, 0)), vec, vec],
        out_specs=pl.BlockSpec((tq, MEM_WIDTH), lambda b, i: (b * nq + i, 0)),
        out_shape=jax.ShapeDtypeStruct((batch * seq, MEM_WIDTH), jnp.bfloat16),
        compiler_params=_params(("parallel", "parallel")), name="mem_attn")(
            mq, mkv, gq.reshape(1, MEM_HEAD_DIM).astype(jnp.float32),
            gk.reshape(1, MEM_HEAD_DIM).astype(jnp.float32))


def _split_w_in(w):
    sizes = (GDN_QK, GDN_QK, GDN_V, GDN_V, GDN_HEADS, GDN_HEADS, ATT_HEADS * HEAD_DIM,
             ATT_KV_HEADS * HEAD_DIM, ATT_KV_HEADS * HEAD_DIM, IQ_W, IDX_DIM, IDX_HEADS)
    offs = np.concatenate([[0], np.cumsum(sizes)])
    seg = lambda i: w[:, int(offs[i]):int(offs[i + 1])]
    gq, gk, gv, gz, ga, gb, aq, ak, av, iq, ik, iw = (seg(i) for i in range(12))
    d = w.shape[0]
    main = jnp.concatenate([iq, gq, gk, gv, gz, aq, ak, av, ik,
                            jnp.zeros((d, N_MAIN_PAD - N_MAIN), w.dtype)], axis=1).astype(jnp.bfloat16)
    nsm = 2 * GDN_HEADS + IDX_HEADS
    small = jnp.concatenate([ga, gb, iw, jnp.zeros((d, LANES - nsm), w.dtype)], axis=1)
    return main, small


def kernel(x, mem, norm_mix, w_in, conv_w, a_log, dt_bias, gdn_norm, att_q_norm, att_k_norm, w_out,
           norm_mem_q, norm_mem_kv, w_mem_q, w_mem_kv, mem_q_norm, mem_k_norm, w_mem_o, norm_ffn,
           w_gate, w_up, w_down):
    B, S, D = x.shape
    M = mem.shape[1]
    T = B * S
    topk = min(TOPK_MAX, S // 4)
    bf = jnp.bfloat16
    x2 = x.reshape(T, D)
    for l in range(w_in.shape[0]):
        w_main, w_small = _split_w_in(w_in[l])
        h, small = _rmsnorm(x2, norm_mix[l], w_small)
        proj = _matmul([(h, w_main)], out_dtype=bf, tn=IN_TN, name="in_proj")
        qkv = _gdn_prep(proj, conv_w[l], S)
        gates = _gate_prep(small, a_log[l], dt_bias[l])
        o_a = _gdn(qkv, proj, gates, gdn_norm[l], B, S)
        q_hm, k_att, iq_hm, ik_rot, w_s = _dsa_prep(proj, small, att_q_norm[l], att_k_norm[l], B, S)
        o_b = _dsa(iq_hm, q_hm, w_s, ik_rot, k_att, proj, B, S, topk)
        wo = w_out[l].astype(bf)
        x2 = _matmul([(o_a, wo[:GDN_V]), (o_b, wo[GDN_V:])], res=x2, name="out_proj")
        hq = _rmsnorm(x2, norm_mem_q[l])
        hm = _rmsnorm(mem.reshape(B * M, D), norm_mem_kv[l])
        mq = _matmul([(hq, w_mem_q[l].astype(bf))], name="mem_q")
        mkv = _matmul([(hm, w_mem_kv[l].astype(bf))], name="mem_kv")
        mo = _mem_attn(mq, mkv, mem_q_norm[l], mem_k_norm[l], B, S, M)
        x2 = _matmul([(mo, w_mem_o[l].astype(bf))], res=x2, tn=1024, name="mem_o")
        hf = _rmsnorm(x2, norm_ffn[l])
        mid = _gateup(hf, w_gate[l].astype(bf), w_up[l].astype(bf))
        x2 = _matmul([(mid, w_down[l].astype(bf))], res=x2, tm=512, tn=256, name="ffn_down")
    return x2.reshape(B, S, D)
```

```python
import functools
import math

import jax
import jax.numpy as jnp
import numpy as np
from jax import lax
from jax.experimental import pallas as pl
from jax.experimental.pallas import tpu as pltpu

D_MODEL = 4096
ATT_KV_HEADS = 4
IDX_HEADS = 32
TOPK_MAX = 256

EPS = 1e-6
ROPE_THETA = 500000.0
HEAD_DIM = 128
ROT_DIM = HEAD_DIM // 4
GDN_DK = 128
GDN_DV = 128
GDN_HEADS = (D_MODEL // 2) // GDN_DV
GDN_CONV = 4
GDN_CHUNK = 64
GDN_QK = GDN_HEADS * GDN_DK
GDN_V = GDN_HEADS * GDN_DV
ATT_HEADS = (D_MODEL // 2) // HEAD_DIM
ATT_GROUP = ATT_HEADS // ATT_KV_HEADS
IDX_DIM = 128
IDX_ROT = IDX_DIM // 4
Q_BLOCK = 128
MEM_HEADS = 4
MEM_HEAD_DIM = 128
MEM_WIDTH = MEM_HEADS * MEM_HEAD_DIM

LANES = 128
VMEM_LIMIT = 56 * 1024 * 1024
NEG_BIG = -1e30

IQ_W = IDX_HEADS * IDX_DIM
OFF_IQ = 0
OFF_GQ = OFF_IQ + IQ_W
OFF_GK = OFF_GQ + GDN_QK
OFF_GV = OFF_GK + GDN_QK
OFF_GZ = OFF_GV + GDN_V
OFF_AQ = OFF_GZ + GDN_V
OFF_AK = OFF_AQ + ATT_HEADS * HEAD_DIM
OFF_AV = OFF_AK + ATT_KV_HEADS * HEAD_DIM
OFF_IK = OFF_AV + ATT_KV_HEADS * HEAD_DIM
N_MAIN = OFF_IK + IDX_DIM
IN_TN = 512
N_MAIN_PAD = -(-N_MAIN // IN_TN) * IN_TN
SM_GA = 0
SM_GB = GDN_HEADS
SM_IW = 2 * GDN_HEADS
HI = lax.Precision.HIGHEST


def _params(sem):
    return pltpu.CompilerParams(dimension_semantics=sem, vmem_limit_bytes=VMEM_LIMIT)


def _pick(n, pref):
    t = min(n, pref)
    while n % t:
        t //= 2
    return t


def _sigmoid(x):
    return 1.0 / (1.0 + jnp.exp(-x))


def _rmsnorm_kernel(x_ref, g_ref, o_ref):
    x = x_ref[...]
    ms = jnp.mean(x * x, axis=-1, keepdims=True)
    o_ref[...] = (x * lax.rsqrt(ms + EPS) * g_ref[...]).astype(o_ref.dtype)


def _rmsnorm_side_kernel(x_ref, g_ref, w_ref, o_ref, s_ref):
    x = x_ref[...]
    ms = jnp.mean(x * x, axis=-1, keepdims=True)
    xn = x * lax.rsqrt(ms + EPS) * g_ref[...]
    o_ref[...] = xn.astype(o_ref.dtype)
    s_ref[...] = jnp.dot(xn, w_ref[...], precision=HI, preferred_element_type=jnp.float32)


def _rmsnorm(x2d, g, w_side=None, tm=256):
    m, d = x2d.shape
    tm = _pick(m, tm)
    g2 = g.reshape(1, d).astype(jnp.float32)
    xs = pl.BlockSpec((tm, d), lambda i: (i, 0))
    gs = pl.BlockSpec((1, d), lambda i: (0, 0))
    if w_side is None:
        return pl.pallas_call(
            _rmsnorm_kernel, grid=(m // tm,), in_specs=[xs, gs], out_specs=xs,
            out_shape=jax.ShapeDtypeStruct((m, d), jnp.bfloat16),
            compiler_params=_params(("parallel",)), name="rmsnorm")(x2d, g2)
    ns = w_side.shape[1]
    return pl.pallas_call(
        _rmsnorm_side_kernel, grid=(m // tm,),
        in_specs=[xs, gs, pl.BlockSpec((d, ns), lambda i: (0, 0))],
        out_specs=[xs, pl.BlockSpec((tm, ns), lambda i: (i, 0))],
        out_shape=[jax.ShapeDtypeStruct((m, d), jnp.bfloat16),
                   jax.ShapeDtypeStruct((m, ns), jnp.float32)],
        compiler_params=_params(("parallel",)), name="rmsnorm_side")(x2d, g2, w_side)


def _matmul_kernel(*refs, n_pairs, has_res):
    o_ref = refs[-1]
    acc = None
    for p in range(n_pairs):
        d = jnp.dot(refs[2 * p][...], refs[2 * p + 1][...], preferred_element_type=jnp.float32)
        acc = d if acc is None else acc + d
    if has_res:
        acc = acc + refs[2 * n_pairs][...]
    o_ref[...] = acc.astype(o_ref.dtype)


def _matmul(pairs, res=None, out_dtype=jnp.float32, tm=1024, tn=512, name="matmul"):
    m = pairs[0][0].shape[0]
    n = pairs[0][1].shape[1]
    tm, tn = _pick(m, tm), _pick(n, tn)
    args, specs = [], []
    for a, w in pairs:
        k = a.shape[1]
        args += [a, w]
        specs += [pl.BlockSpec((tm, k), lambda i, j: (i, 0)), pl.BlockSpec((k, tn), lambda i, j: (0, j))]
    if res is not None:
        args.append(res)
        specs.append(pl.BlockSpec((tm, tn), lambda i, j: (i, j)))
    return pl.pallas_call(
        functools.partial(_matmul_kernel, n_pairs=len(pairs), has_res=res is not None),
        grid=(m // tm, n // tn), in_specs=specs,
        out_specs=pl.BlockSpec((tm, tn), lambda i, j: (i, j)),
        out_shape=jax.ShapeDtypeStruct((m, n), out_dtype),
        compiler_params=_params(("parallel", "arbitrary")), name=name)(*args)


def _gateup_kernel(a_ref, wg_ref, wu_ref, o_ref):
    a = a_ref[...]
    g = jnp.dot(a, wg_ref[...], preferred_element_type=jnp.float32)
    u = jnp.dot(a, wu_ref[...], preferred_element_type=jnp.float32)
    o_ref[...] = (g * _sigmoid(g) * u).astype(o_ref.dtype)


def _gateup(a, wg, wu, tm=1024, tn=256):
    m, k = a.shape
    n = wg.shape[1]
    tm, tn = _pick(m, tm), _pick(n, tn)
    ws = pl.BlockSpec((k, tn), lambda i, j: (0, j))
    return pl.pallas_call(
        _gateup_kernel, grid=(m // tm, n // tn),
        in_specs=[pl.BlockSpec((tm, k), lambda i, j: (i, 0)), ws, ws],
        out_specs=pl.BlockSpec((tm, tn), lambda i, j: (i, j)),
        out_shape=jax.ShapeDtypeStruct((m, n), jnp.bfloat16),
        compiler_params=_params(("parallel", "arbitrary")), name="ffn_gateup")(a, wg, wu)


HALO = 16


def _gdn_prep_kernel(x_ref, halo_ref, w_ref, o_ref, buf_ref, *, ts, tc, blocks_per_seq, nqk_blocks):
    i = pl.program_id(0)
    j = pl.program_id(1)
    first = (i % blocks_per_seq) == 0
    halo = halo_ref[...].astype(jnp.float32)
    buf_ref[0:HALO, :] = jnp.where(first, 0.0, halo)
    buf_ref[HALO:HALO + ts, :] = x_ref[...].astype(jnp.float32)
    w = w_ref[...]
    y = None
    for k in range(GDN_CONV):
        off = HALO - (GDN_CONV - 1) + k
        term = buf_ref[off:off + ts, :] * w[k:k + 1, :]
        y = term if y is None else y + term
    y = y * _sigmoid(y)
    is_q = j < nqk_blocks
    is_v = j >= 2 * nqk_blocks
    for h in range(tc // GDN_DK):
        yh = y[:, h * GDN_DK:(h + 1) * GDN_DK]
        rs = lax.rsqrt(jnp.sum(yh * yh, axis=-1, keepdims=True) + EPS)
        rs = rs * jnp.where(is_q, GDN_DK ** -0.5, 1.0)
        fac = jnp.where(is_v, 1.0, rs)
        o_ref[:, h * GDN_DK:(h + 1) * GDN_DK] = (yh * fac).astype(o_ref.dtype)


def _gdn_prep(proj, conv_w, seq):
    t = proj.shape[0]
    ts = _pick(seq, 512)
    tc = 512
    nconv = (2 * GDN_QK + GDN_V) // tc
    col0 = OFF_GQ // tc
    rb = ts // HALO
    kern = functools.partial(_gdn_prep_kernel, ts=ts, tc=tc, blocks_per_seq=seq // ts,
                             nqk_blocks=GDN_QK // tc)
    return pl.pallas_call(
        kern, grid=(t // ts, nconv),
        in_specs=[pl.BlockSpec((ts, tc), lambda i, j: (i, col0 + j)),
                  pl.BlockSpec((HALO, tc), lambda i, j: (jnp.maximum(i * rb - 1, 0), col0 + j)),
                  pl.BlockSpec((GDN_CONV, tc), lambda i, j: (0, j))],
        out_specs=pl.BlockSpec((ts, tc), lambda i, j: (i, j)),
        out_shape=jax.ShapeDtypeStruct((t, nconv * tc), jnp.bfloat16),
        scratch_shapes=[pltpu.VMEM((HALO + ts, tc), jnp.float32)],
        compiler_params=_params(("parallel", "parallel")), name="gdn_prep")(proj, proj, conv_w)


def _gate_prep_kernel(s_ref, alog_ref, dtb_ref, o_ref, dec_ref, *, ts):
    C = GDN_CHUNK
    s = s_ref[...]
    lane = lax.broadcasted_iota(jnp.int32, s.shape, 1)
    xa = s + dtb_ref[...]
    softplus = jnp.maximum(xa, 0.0) + jnp.log(1.0 + jnp.exp(-jnp.abs(xa)))
    g = -jnp.exp(alog_ref[...]) * softplus
    g = jnp.where(lane < GDN_HEADS, g, 0.0)
    r = lax.broadcasted_iota(jnp.int32, (ts, ts), 0)
    c = lax.broadcasted_iota(jnp.int32, (ts, ts), 1)
    tri = jnp.where((r // C == c // C) & (c <= r), 1.0, 0.0)
    gcum = jnp.dot(tri, g, precision=HI, preferred_element_type=jnp.float32)
    beta = _sigmoid(s)
    is_beta = (lane >= SM_GB) & (lane < SM_GB + GDN_HEADS)
    o_ref[...] = jnp.where(lane < GDN_HEADS, gcum, jnp.where(is_beta, beta, 0.0))
    gcum_t = gcum.T
    incl = lax.broadcasted_iota(jnp.int32, (C, C), 1) <= lax.broadcasted_iota(jnp.int32, (C, C), 0)
    sub =lax.broadcasted_iota(jnp.int32, (8, C), 0)
    for h in range(GDN_HEADS):
        for ci in range(ts // C):
            col = gcum[ci * C:(ci + 1) * C, h:h + 1]
            tile = gcum_t[8 * (h // 8):8 * (h // 8) + 8, ci * C:(ci + 1) * C]
            row = jnp.sum(jnp.where(sub == h % 8, tile, 0.0), axis=0, keepdims=True)
            dec_ref[ci * C:(ci + 1) * C, h * C:(h + 1) * C] = jnp.exp(jnp.where(incl, col - row, NEG_BIG))


def _gate_prep(small, a_log, dt_bias):
    t, ns = small.shape
    ts = _pick(t, 256)
    pad = lambda v: jnp.zeros((1, ns), jnp.float32).at[0, :GDN_HEADS].set(v.astype(jnp.float32))
    vs = pl.BlockSpec((1, ns), lambda i: (0, 0))
    bs = pl.BlockSpec((ts, ns), lambda i: (i, 0))
    dw = GDN_HEADS * GDN_CHUNK
    return pl.pallas_call(
        functools.partial(_gate_prep_kernel, ts=ts), grid=(t // ts,),
        in_specs=[bs, vs, vs], out_specs=[bs, pl.BlockSpec((ts, dw), lambda i: (i, 0))],
        out_shape=[jax.ShapeDtypeStruct((t, ns), jnp.float32), jax.ShapeDtypeStruct((t, dw), jnp.float32)],
        compiler_params=_params(("parallel",)), name="gate_prep")(small, pad(a_log), pad(dt_bias))


GDN_WY_HEADS = 4
GDN_WY_ROWS = 512
GDN_WY_CG = 4
GDN_SCAN_HEADS = 8
GDN_SCAN_ROWS = 1024


def _bdot(a, b):
    return jnp.dot(a.astype(jnp.bfloat16), b.astype(jnp.bfloat16), preferred_element_type=jnp.float32)


def _bdot_nt(a, b):
    return lax.dot_general(a.astype(jnp.bfloat16), b.astype(jnp.bfloat16), (((1,), (1,)), ((), ())),
                           preferred_element_type=jnp.float32)


def _head_col(gsl, idx):
    lane = lax.broadcasted_iota(jnp.int32, (1, LANES), 1)
    return jnp.sum(gsl * jnp.where(lane == idx, 1.0, 0.0), axis=1, keepdims=True)


def _gdn_wy_kernel(q_ref, k_ref, v_ref, g_ref, dec_ref, u_ref, w_ref, qd_ref, kd_ref, qk_ref, *, hb):
    hg = pl.program_id(1)
    C, CG = GDN_CHUNK, GDN_WY_CG
    r = lax.broadcasted_iota(jnp.int32, (C, C), 0)
    c = lax.broadcasted_iota(jnp.int32, (C, C), 1)
    off_diag = c != r

    def body(it, carry):
        where, q, k, v, gc, beta, decay = [], [], [], [], [], [], []
        for cc in range(CG):
            r0 = pl.multiple_of((it * CG + cc) * C, C)
            gsl = g_ref[pl.ds(r0, C), :]
            for hh in range(hb):
                sl = slice(hh * GDN_DK, (hh + 1) * GDN_DK)
                h = hg * hb + hh
                where.append((r0, hh))
                q.append(q_ref[pl.ds(r0, C), sl].astype(jnp.float32))
                k.append(k_ref[pl.ds(r0, C), sl].astype(jnp.float32))
                v.append(v_ref[pl.ds(r0, C), sl].astype(jnp.float32))
                gc.append(_head_col(gsl, h))
                beta.append(_head_col(gsl, SM_GB + h))
                decay.append(dec_ref[pl.ds(r0, C), hh * C:(hh + 1) * C])
        n = len(where)
        ch = range(n)
        kb = [k[i] * beta[i] for i in ch]
        eg = [jnp.exp(gc[i]) for i in ch]
        kk = [_bdot_nt(kb[i], k[i]) for i in ch]
        qk = [_bdot_nt(q[i], k[i]) for i in ch]
        pw = [jnp.where(off_diag, kk[i] * decay[i], 0.0) for i in ch]
        x = [jnp.concatenate([v[i] * beta[i], kb[i] * eg[i]], axis=1) for i in ch]
        n_lvl = int(math.log2(C))
        for lvl in range(n_lvl):
            px = [_bdot(pw[i], x[i]) for i in ch]
            if lvl + 1 < n_lvl:
                pw = [_bdot(pw[i], pw[i]) for i in ch]
            x = [x[i] - px[i] if lvl == 0 else x[i] + px[i] for i in ch]
        for i, (r0, hh) in enumerate(where):
            sl = slice(hh * GDN_DK, (hh + 1) * GDN_DK)
            u_ref[pl.ds(r0, C), sl] = x[i][:, :GDN_DV]
            w_ref[pl.ds(r0, C), sl] = x[i][:, GDN_DV:].astype(w_ref.dtype)
            qd_ref[pl.ds(r0, C), sl] = (q[i] * eg[i]).astype(qd_ref.dtype)
            kd_ref[pl.ds(r0, C), sl] = (k[i] * jnp.exp(gc[i][C - 1:C, :] - gc[i])).astype(kd_ref.dtype)
            qk_ref[pl.ds(r0, C), hh * C:(hh + 1) * C] = (qk[i] * decay[i]).astype(qk_ref.dtype)
        return carry

    lax.fori_loop(0, q_ref.shape[0] // (CG * C), body, 0)


def _gdn_scan_kernel(u_ref, w_ref, qd_ref, kd_ref, qk_ref, z_ref, g_ref, gain_ref, o_ref, state_ref, *, hb):
    hg = pl.program_id(1)
    C = GDN_CHUNK
    sub8 = lax.broadcasted_iota(jnp.int32, (8, LANES), 0)
    gain = gain_ref[...]

    @pl.when(pl.program_id(2) == 0)
    def _():
        state_ref[...] = jnp.zeros(state_ref.shape, state_ref.dtype)

    def body(ci, states):
        r0 = pl.multiple_of(ci * C, C)
        g_tail = g_ref[pl.ds(r0 + C - 8, 8), :]
        g_last_row = jnp.sum(jnp.where(sub8 == 7, g_tail, 0.0), axis=0, keepdims=True)
        loaded = []
        for hh in range(hb):
            sl = slice(hh * GDN_DK, (hh + 1) * GDN_DK)
            loaded.append((u_ref[pl.ds(r0, C), sl], w_ref[pl.ds(r0, C), sl], qd_ref[pl.ds(r0, C), sl],
                           kd_ref[pl.ds(r0, C), sl], qk_ref[pl.ds(r0, C), hh * C:(hh + 1) * C],
                           z_ref[pl.ds(r0, C), sl].astype(jnp.float32)))
        hs = range(hb)
        f32 = jnp.float32
        u, w, qd, kd, qk, z = zip(*loaded)
        sb = [states[i].astype(jnp.bfloat16) for i in hs]
        ws = [jnp.dot(w[i], sb[i], preferred_element_type=f32) for i in hs]
        qs = [jnp.dot(qd[i], sb[i], preferred_element_type=f32) for i in hs]
        vb = [(u[i] - ws[i]).astype(jnp.bfloat16) for i in hs]
        kv = [lax.dot_general(kd[i], vb[i], (((0,), (0,)), ((), ())), preferred_element_type=f32) for i in hs]
        qv = [jnp.dot(qk[i], vb[i], preferred_element_type=f32) for i in hs]
        a = [jnp.exp(_head_col(g_last_row, hg * hb + i)) for i in hs]
        new_states = tuple(states[i] * a[i] + kv[i] for i in hs)
        for i in hs:
            o = qs[i] + qv[i]
            on = o * lax.rsqrt(jnp.mean(o * o, axis=-1, keepdims=True) + EPS) * gain
            o_ref[pl.ds(r0, C), i * GDN_DV:(i + 1) * GDN_DV] = (on * (z[i] * _sigmoid(z[i]))).astype(o_ref.dtype)
        return new_states

    init = tuple(state_ref[hh] for hh in range(hb))
    final = lax.fori_loop(0, u_ref.shape[0] // C, body, init)
    for hh in range(hb):
        state_ref[hh] = final[hh]


def _gdn(qkv, proj, gates, decay, gdn_norm, batch, seq):
    t = batch * seq
    bf = jnp.bfloat16
    C = GDN_CHUNK
    hb = min(GDN_WY_HEADS, GDN_HEADS)
    ts = _pick(seq, GDN_WY_ROWS)
    assert GDN_HEADS % hb == 0 and ts % (GDN_WY_CG * C) == 0
    hw = hb * GDN_DK
    kq = GDN_QK // hw
    blk = lambda off: pl.BlockSpec((ts, hw), lambda i, g: (i, off + g))
    oblk = pl.BlockSpec((ts, hw), lambda i, g: (i, g))
    cblk = pl.BlockSpec((ts, hb * C), lambda i, g: (i, g))
    u, w, qd, kd, qk = pl.pallas_call(
        functools.partial(_gdn_wy_kernel, hb=hb), grid=(t // ts, GDN_HEADS // hb),
        in_specs=[blk(0), blk(kq), blk(2 * kq),
                  pl.BlockSpec((ts, gates.shape[1]), lambda i, g: (i, 0)), cblk],
        out_specs=[oblk, oblk, oblk, oblk, cblk],
        out_shape=[jax.ShapeDtypeStruct((t, GDN_V), jnp.float32), jax.ShapeDtypeStruct((t, GDN_QK), bf),
                   jax.ShapeDtypeStruct((t, GDN_QK), bf), jax.ShapeDtypeStruct((t, GDN_QK), bf),
                   jax.ShapeDtypeStruct((t, GDN_HEADS * C), bf)],
        compiler_params=_params(("parallel", "parallel")), name="gdn_wy")(qkv, qkv, qkv, gates, decay)
    hb = min(GDN_SCAN_HEADS, GDN_HEADS)
    ts = _pick(seq, GDN_SCAN_ROWS)
    assert GDN_HEADS % hb == 0
    nsb = seq // ts
    hw = hb * GDN_DK
    zoff = OFF_GZ // hw
    blk = pl.BlockSpec((ts, hw), lambda b, g, s: (b * nsb + s, g))
    return pl.pallas_call(
        functools.partial(_gdn_scan_kernel, hb=hb), grid=(batch, GDN_HEADS // hb, nsb),
        in_specs=[blk, blk, blk, blk,
                  pl.BlockSpec((ts, hb * C), lambda b, g, s: (b * nsb + s, g)),
                  pl.BlockSpec((ts, hw), lambda b, g, s: (b * nsb + s, zoff + g)),
                  pl.BlockSpec((ts, gates.shape[1]), lambda b, g, s: (b * nsb + s, 0)),
                  pl.BlockSpec((1, GDN_DV), lambda b, g, s: (0, 0))],
        out_specs=blk,
        out_shape=jax.ShapeDtypeStruct((t, GDN_V), bf),
        scratch_shapes=[pltpu.VMEM((hb, GDN_DK, GDN_DV), jnp.float32)],
        compiler_params=_params(("parallel", "parallel", "arbitrary")), name="gdn_scan")(
            u, w, qd, kd, qk, proj, gates, gdn_norm.reshape(1, GDN_DV).astype(jnp.float32))


def _rope(x, cosf, sin_lo, sin_hi, half):
    return x * cosf + pltpu.roll(x, LANES - half, 1) * sin_lo + pltpu.roll(x, half, 1) * sin_hi


def _dsa_prep_kernel(aq_ref, ak_ref, iq_ref, ik_ref, sm_ref, cos_ref, slo_ref, shi_ref, gq_ref, gk_ref,
                     q_out, k_out, iq_out, ik_out, w_out):
    cosf, slo, shi = cos_ref[...], slo_ref[...], shi_ref[...]
    half = ROT_DIM // 2

    def qk_norm(x, gain):
        return x * lax.rsqrt(jnp.mean(x * x, axis=-1, keepdims=True) + EPS) * gain

    for h in range(ATT_HEADS):
        x = aq_ref[:, h * HEAD_DIM:(h + 1) * HEAD_DIM].astype(jnp.float32)
        x = _rope(qk_norm(x, gq_ref[...]), cosf, slo, shi, half) * (HEAD_DIM ** -0.5)
        q_out[h * Q_BLOCK:(h + 1) * Q_BLOCK, :] = x.astype(q_out.dtype)
    for h in range(ATT_KV_HEADS):
        x = ak_ref[:, h * HEAD_DIM:(h + 1) * HEAD_DIM].astype(jnp.float32)
        x = _rope(qk_norm(x, gk_ref[...]), cosf, slo, shi, half)
        k_out[:, h * HEAD_DIM:(h + 1) * HEAD_DIM] = x.astype(k_out.dtype)
    for h in range(IDX_HEADS):
        x = iq_ref[:, h * IDX_DIM:(h + 1) * IDX_DIM].astype(jnp.float32)
        iq_out[h * Q_BLOCK:(h + 1) * Q_BLOCK, :] = _rope(x, cosf, slo, shi, half).astype(iq_out.dtype)
    ik_out[...] = _rope(ik_ref[...].astype(jnp.float32), cosf, slo, shi, half).astype(ik_out.dtype)
    w_out[...] = sm_ref[...] * (IDX_HEADS ** -0.5 * IDX_DIM ** -0.5)


def _rope_tables(seq):
    half = ROT_DIM // 2
    inv_freq = ROPE_THETA ** (-jnp.arange(half, dtype=jnp.float32) * 2.0 / ROT_DIM)
    ang = jnp.arange(seq, dtype=jnp.float32)[:, None] * inv_freq[None, :]
    cos, sin = jnp.cos(ang), jnp.sin(ang)
    z = jnp.zeros((seq, LANES - 2 * half), jnp.float32)
    zh = jnp.zeros((seq, half), jnp.float32)
    cosf = jnp.concatenate([cos, cos, jnp.ones_like(z)], axis=1)
    sin_lo = jnp.concatenate([-sin, zh, z], axis=1)
    sin_hi = jnp.concatenate([zh, sin, z], axis=1)
    return cosf, sin_lo, sin_hi


def _dsa_prep(proj, small, att_q_norm, att_k_norm, batch, seq):
    assert ROT_DIM == IDX_ROT
    t = batch * seq
    nq = seq // Q_BLOCK
    qb = Q_BLOCK
    cosf, slo, shi = _rope_tables(seq)
    tab = pl.BlockSpec((qb, LANES), lambda b, i: (i, 0))
    row = lambda w, off: pl.BlockSpec((qb, w), lambda b, i: (b * nq + i, off // w))
    vec = pl.BlockSpec((1, HEAD_DIM), lambda b, i: (0, 0))
    aq_w, ak_w = ATT_HEADS * HEAD_DIM, ATT_KV_HEADS * HEAD_DIM
    return pl.pallas_call(
        _dsa_prep_kernel, grid=(batch, nq),
        in_specs=[row(aq_w, OFF_AQ), row(ak_w, OFF_AK), row(IQ_W, OFF_IQ), row(IDX_DIM, OFF_IK),
                  pl.BlockSpec((qb, small.shape[1]), lambda b, i: (b * nq + i, 0)),
                  tab, tab, tab, vec, vec],
        out_specs=[pl.BlockSpec((ATT_HEADS * qb, HEAD_DIM), lambda b, i: (b * nq + i, 0)),
                   pl.BlockSpec((qb, ak_w), lambda b, i: (b * nq + i, 0)),
                   pl.BlockSpec((IDX_HEADS * qb, IDX_DIM), lambda b, i: (b * nq + i, 0)),
                   pl.BlockSpec((qb, IDX_DIM), lambda b, i: (b * nq + i, 0)),
                   pl.BlockSpec((qb, small.shape[1]), lambda b, i: (b * nq + i, 0))],
        out_shape=[jax.ShapeDtypeStruct((t * ATT_HEADS, HEAD_DIM), jnp.bfloat16),
                   jax.ShapeDtypeStruct((t, ak_w), jnp.bfloat16),
                   jax.ShapeDtypeStruct((t * IDX_HEADS, IDX_DIM), jnp.bfloat16),
                   jax.ShapeDtypeStruct((t, IDX_DIM), jnp.bfloat16),
                   jax.ShapeDtypeStruct((t, small.shape[1]), jnp.float32)],
        compiler_params=_params(("parallel", "parallel")), name="dsa_prep")(
            proj, proj, proj, proj, small, cosf, slo, shi,
            att_q_norm.reshape(1, HEAD_DIM).astype(jnp.float32),
            att_k_norm.reshape(1, HEAD_DIM).astype(jnp.float32))


DSA_KB = 512
DSA_SB = 256
INT_MIN = -2 ** 31


def _dsa_kernel(iq_ref, q_ref, w_ref, ik_ref, k_ref, v_ref, o_ref,
                key_ref, wb_ref, m_ref, l_ref, acc_ref, *, topk):
    qi = pl.program_id(1)
    QB, KB, SB = Q_BLOCK, DSA_KB, DSA_SB
    n_kb = (qi * QB + QB + KB - 1) // KB
    qpos = qi * QB + lax.broadcasted_iota(jnp.int32, (QB, SB), 0)
    kiota = lax.broadcasted_iota(jnp.int32, (QB, SB), 1)

    w = w_ref[...]
    for h in range(IDX_HEADS):
        wb_ref[h] = jnp.broadcast_to(w[:, SM_IW + h:SM_IW + h + 1], (QB, LANES))

    def score_block(sb, carry):
        k0 = pl.multiple_of(sb * SB, SB)
        lg = lax.dot_general(iq_ref[...], ik_ref[pl.ds(k0, SB), :], (((1,), (1,)), ((), ())),
                             preferred_element_type=jnp.float32)
        sc = None
        for h in range(IDX_HEADS):
            wb = wb_ref[h]
            term = jnp.maximum(lg[h * QB:(h + 1) * QB], 0.0) * jnp.concatenate([wb] * (SB // LANES), axis=1)
            sc = term if sc is None else sc + term
        bits = pltpu.bitcast(sc, jnp.int32)
        keys = bits ^ ((bits >> 31) & 0x7FFFFFFF)
        key_ref[:, pl.ds(k0, SB)] = jnp.where(k0 + kiota <= qpos, keys, INT_MIN)
        return carry

    lax.fori_loop(0, n_kb * (KB // SB), score_block, 0)

    def count_ge(cand):
        def blk(kb, cnt):
            k0 = pl.multiple_of(kb * KB, KB)
            kk = key_ref[:, pl.ds(k0, KB)]
            for g in range(KB // LANES):
                cnt = cnt + jnp.where(kk[:, g * LANES:(g + 1) * LANES] >= cand, 1, 0)
            return cnt
        cnt = lax.fori_loop(0, n_kb, blk, jnp.zeros((QB, LANES), jnp.int32))
        return jnp.sum(cnt, axis=1, keepdims=True)

    thr0 = jnp.full((QB, 1), INT_MIN, jnp.int32)
    thr0 = jnp.where(count_ge(jnp.zeros((QB, 1), jnp.int32)) >= topk, 0, thr0)

    def bit_step(i, thr):
        cand = thr | (jnp.int32(1) << (30 - i))
        return jnp.where(count_ge(cand) >= topk, cand, thr)

    thr = lax.fori_loop(0, 31, bit_step, thr0)
    thr = jnp.maximum(thr, INT_MIN + 1)

    G = ATT_GROUP
    NG = KB // LANES
    m_ref[...] = jnp.full(m_ref.shape, NEG_BIG, jnp.float32)
    l_ref[...] = jnp.zeros(l_ref.shape, jnp.float32)
    acc_ref[...] = jnp.zeros(acc_ref.shape, jnp.float32)

    def lane_fold(x, op):
        parts = [x[:, g * LANES:(g + 1) * LANES] for g in range(NG)]
        while len(parts) > 1:
            parts = [op(parts[i], parts[i + 1]) for i in range(0, len(parts), 2)]
        return parts[0]

    def attn_block(kb, carry):
        k0 = pl.multiple_of(kb * KB, KB)
        bias = jnp.where(key_ref[:, pl.ds(k0, KB)] >= thr, 0.0, NEG_BIG)
        bias_g = jnp.concatenate([bias] * G, axis=0)
        grp = range(ATT_KV_HEADS)
        rows = [slice(hk * G * QB, (hk + 1) * G * QB) for hk in grp]
        cols = [slice(hk * HEAD_DIM, (hk + 1) * HEAD_DIM) for hk in grp]
        s = [lax.dot_general(q_ref[rows[i], :], k_ref[pl.ds(k0, KB), cols[i]], (((1,), (1,)), ((), ())),
                             preferred_element_type=jnp.float32) for i in grp]
        p, alpha, row_sum, m_new = [], [], [], []
        for i in grp:
            si = s[i] + bias_g
            m_old = m_ref[rows[i], :]
            mi = jnp.maximum(m_old, jnp.max(lane_fold(si, jnp.maximum), axis=1, keepdims=True))
            pi = jnp.exp(si - jnp.concatenate([mi] * NG, axis=1))
            alpha.append(jnp.exp(m_old - mi))
            row_sum.append(jnp.sum(lane_fold(pi, jnp.add), axis=1, keepdims=True))
            p.append(pi.astype(jnp.bfloat16))
            m_new.append(mi)
        pv = [jnp.dot(p[i], v_ref[pl.ds(k0, KB), cols[i]], preferred_element_type=jnp.float32) for i in grp]
        for i in grp:
            l_ref[rows[i], :] = alpha[i] * l_ref[rows[i], :] + row_sum[i]
            acc_ref[rows[i], :] = alpha[i] * acc_ref[rows[i], :] + pv[i]
            m_ref[rows[i], :] = m_new[i]
        return carry

    lax.fori_loop(0, n_kb, attn_block, 0)
    for h in range(ATT_HEADS):
        rows = slice(h * QB, (h + 1) * QB)
        o_ref[:, h * HEAD_DIM:(h + 1) * HEAD_DIM] = (acc_ref[rows, :] / l_ref[rows, :]).astype(o_ref.dtype)


def _dsa(iq_hm, q_hm, w_s, ik_rot, k_att, proj, batch, seq, topk):
    nq = seq // Q_BLOCK
    qb = Q_BLOCK
    kvw = ATT_KV_HEADS * HEAD_DIM
    return pl.pallas_call(
        functools.partial(_dsa_kernel, topk=topk), grid=(batch, nq),
        in_specs=[pl.BlockSpec((IDX_HEADS * qb, IDX_DIM), lambda b, i: (b * nq + i, 0)),
                  pl.BlockSpec((ATT_HEADS * qb, HEAD_DIM), lambda b, i: (b * nq + i, 0)),
                  pl.BlockSpec((qb, w_s.shape[1]), lambda b, i: (b * nq + i, 0)),
                  pl.BlockSpec((seq, IDX_DIM), lambda b, i: (b, 0)),
                  pl.BlockSpec((seq, kvw), lambda b, i: (b, 0)),
                  pl.BlockSpec((seq, kvw), lambda b, i: (b, OFF_AV // kvw))],
        out_specs=pl.BlockSpec((qb, ATT_HEADS * HEAD_DIM), lambda b, i: (b * nq + i, 0)),
        out_shape=jax.ShapeDtypeStruct((batch * seq, ATT_HEADS * HEAD_DIM), jnp.bfloat16),
        scratch_shapes=[pltpu.VMEM((qb, seq), jnp.int32),
                        pltpu.VMEM((IDX_HEADS, qb, LANES), jnp.float32),
                        pltpu.VMEM((ATT_HEADS * qb, LANES), jnp.float32),
                        pltpu.VMEM((ATT_HEADS * qb, LANES), jnp.float32),
                        pltpu.VMEM((ATT_HEADS * qb, HEAD_DIM), jnp.float32)],
        compiler_params=_params(("parallel", "arbitrary")), name="dsa")(
            iq_hm, q_hm, w_s, ik_rot, k_att, proj)


def _mem_attn_kernel(q_ref, kv_ref, gq_ref, gk_ref, o_ref):
    def norm(x, gain):
        return x * lax.rsqrt(jnp.mean(x * x, axis=-1, keepdims=True) + EPS) * gain

    for h in range(MEM_HEADS):
        cols = slice(h * MEM_HEAD_DIM, (h + 1) * MEM_HEAD_DIM)
        q = norm(q_ref[:, cols], gq_ref[...]) * (MEM_HEAD_DIM ** -0.5)
        k = norm(kv_ref[:, cols], gk_ref[...])
        v = kv_ref[:, MEM_WIDTH + h * MEM_HEAD_DIM:MEM_WIDTH + (h + 1) * MEM_HEAD_DIM]
        s = _bdot_nt(q, k)
        p = jnp.exp(s - jnp.max(s, axis=1, keepdims=True))
        o = _bdot(p, v) / jnp.sum(p, axis=1, keepdims=True)
        o_ref[:, cols] = o.astype(o_ref.dtype)


def _mem_attn(mq, mkv, gq, gk, batch, seq, n_mem):
    tq = _pick(seq, 512)
    nq = seq // tq
    vec = pl.BlockSpec((1, MEM_HEAD_DIM), lambda b, i: (0, 0))
    return pl.pallas_call(
        _mem_attn_kernel, grid=(batch, nq),
        in_specs=[pl.BlockSpec((tq, MEM_WIDTH), lambda b, i: (b * nq + i, 0)),
                  pl.BlockSpec((n_mem, 2 * MEM_WIDTH), lambda b, i: (b, 0)), vec, vec],
        out_specs=pl.BlockSpec((tq, MEM_WIDTH), lambda b, i: (b * nq + i, 0)),
        out_shape=jax.ShapeDtypeStruct((batch * seq, MEM_WIDTH), jnp.bfloat16),
        compiler_params=_params(("parallel", "parallel")), name="mem_attn")(
            mq, mkv, gq.reshape(1, MEM_HEAD_DIM).astype(jnp.float32),
            gk.reshape(1, MEM_HEAD_DIM).astype(jnp.float32))


def _split_w_in(w):
    sizes = (GDN_QK, GDN_QK, GDN_V, GDN_V, GDN_HEADS, GDN_HEADS, ATT_HEADS * HEAD_DIM,
             ATT_KV_HEADS * HEAD_DIM, ATT_KV_HEADS * HEAD_DIM, IQ_W, IDX_DIM, IDX_HEADS)
    offs = np.concatenate([[0], np.cumsum(sizes)])
    seg = lambda i: w[:, int(offs[i]):int(offs[i + 1])]
    gq, gk, gv, gz, ga, gb, aq, ak, av, iq, ik, iw = (seg(i) for i in range(12))
    d = w.shape[0]
    main = jnp.concatenate([iq, gq, gk, gv, gz, aq, ak, av, ik,
                            jnp.zeros((d, N_MAIN_PAD - N_MAIN), w.dtype)], axis=1).astype(jnp.bfloat16)
    nsm = 2 * GDN_HEADS + IDX_HEADS
    small = jnp.concatenate([ga, gb, iw, jnp.zeros((d, LANES - nsm), w.dtype)], axis=1)
    return main, small


def kernel(x, mem, norm_mix, w_in, conv_w, a_log, dt_bias, gdn_norm, att_q_norm, att_k_norm, w_out,
           norm_mem_q, norm_mem_kv, w_mem_q, w_mem_kv, mem_q_norm, mem_k_norm, w_mem_o, norm_ffn,
           w_gate, w_up, w_down):
    B, S, D = x.shape
    M = mem.shape[1]
    T = B * S
    topk = min(TOPK_MAX, S // 4)
    bf = jnp.bfloat16
    x2 = x.reshape(T, D)
    for l in range(w_in.shape[0]):
        w_main, w_small = _split_w_in(w_in[l])
        h, small = _rmsnorm(x2, norm_mix[l], w_small)
        proj = _matmul([(h, w_main)], out_dtype=bf, tn=IN_TN, name="in_proj")
        qkv = _gdn_prep(proj, conv_w[l], S)
        gates, decay = _gate_prep(small, a_log[l], dt_bias[l])
        o_a = _gdn(qkv, proj, gates, decay, gdn_norm[l], B, S)
        q_hm, k_att, iq_hm, ik_rot, w_s = _dsa_prep(proj, small, att_q_norm[l], att_k_norm[l], B, S)
        o_b = _dsa(iq_hm, q_hm, w_s, ik_rot, k_att, proj, B, S, topk)
        wo = w_out[l].astype(bf)
        x2 = _matmul([(o_a, wo[:GDN_V]), (o_b, wo[GDN_V:])], res=x2, name="out_proj")
        hq = _rmsnorm(x2, norm_mem_q[l])
        hm = _rmsnorm(mem.reshape(B * M, D), norm_mem_kv[l])
        mq = _matmul([(hq, w_mem_q[l].astype(bf))], name="mem_q")
        mkv = _matmul([(hm, w_mem_kv[l].astype(bf))], name="mem_kv")
        mo = _mem_attn(mq, mkv, mem_q_norm[l], mem_k_norm[l], B, S, M)
        x2 = _matmul([(mo, w_mem_o[l].astype(bf))], res=x2, tn=1024, name="mem_o")
        hf = _rmsnorm(x2, norm_ffn[l])
        mid = _gateup(hf, w_gate[l].astype(bf), w_up[l].astype(bf))
        x2 = _matmul([(mid, w_down[l].astype(bf))], res=x2, tm=512, tn=256, name="ffn_down")
    return x2.reshape(B, S, D)
```

```python
import functools
import math

import jax
import jax.numpy as jnp
import numpy as np
from jax import lax
from jax.experimental import pallas as pl
from jax.experimental.pallas import tpu as pltpu

D_MODEL = 4096
ATT_KV_HEADS = 4
IDX_HEADS = 32
TOPK_MAX = 256

EPS = 1e-6
ROPE_THETA = 500000.0
HEAD_DIM = 128
ROT_DIM = HEAD_DIM // 4
GDN_DK = 128
GDN_DV = 128
GDN_HEADS = (D_MODEL // 2) // GDN_DV
GDN_CONV = 4
GDN_CHUNK = 64
GDN_QK = GDN_HEADS * GDN_DK
GDN_V = GDN_HEADS * GDN_DV
ATT_HEADS = (D_MODEL // 2) // HEAD_DIM
ATT_GROUP = ATT_HEADS // ATT_KV_HEADS
IDX_DIM = 128
IDX_ROT = IDX_DIM // 4
Q_BLOCK = 128
MEM_HEADS = 4
MEM_HEAD_DIM = 128
MEM_WIDTH = MEM_HEADS * MEM_HEAD_DIM

LANES = 128
VMEM_LIMIT = 56 * 1024 * 1024
NEG_BIG = -1e30

IQ_W = IDX_HEADS * IDX_DIM
OFF_IQ = 0
OFF_GQ = OFF_IQ + IQ_W
OFF_GK = OFF_GQ + GDN_QK
OFF_GV = OFF_GK + GDN_QK
OFF_GZ = OFF_GV + GDN_V
OFF_AQ = OFF_GZ + GDN_V
OFF_AK = OFF_AQ + ATT_HEADS * HEAD_DIM
OFF_AV = OFF_AK + ATT_KV_HEADS * HEAD_DIM
OFF_IK = OFF_AV + ATT_KV_HEADS * HEAD_DIM
N_MAIN = OFF_IK + IDX_DIM
IN_TN = 512
N_MAIN_PAD = -(-N_MAIN // IN_TN) * IN_TN
SM_GA = 0
SM_GB = GDN_HEADS
SM_IW = 2 * GDN_HEADS
HI = lax.Precision.HIGHEST


def _params(sem):
    return pltpu.CompilerParams(dimension_semantics=sem, vmem_limit_bytes=VMEM_LIMIT)


def _pick(n, pref):
    t = min(n, pref)
    while n % t:
        t //= 2
    return t


def _sigmoid(x):
    return 1.0 / (1.0 + jnp.exp2(x * (-math.log2(math.e))))


def _rmsnorm_kernel(x_ref, g_ref, o_ref):
    x = x_ref[...]
    ms = jnp.mean(x * x, axis=-1, keepdims=True)
    o_ref[...] = (x * lax.rsqrt(ms + EPS) * g_ref[...]).astype(o_ref.dtype)


def _rmsnorm_side_kernel(x_ref, g_ref, w_ref, o_ref, s_ref):
    x = x_ref[...]
    ms = jnp.mean(x * x, axis=-1, keepdims=True)
    xn = x * lax.rsqrt(ms + EPS) * g_ref[...]
    o_ref[...] = xn.astype(o_ref.dtype)
    s_ref[...] = jnp.dot(xn, w_ref[...], precision=HI, preferred_element_type=jnp.float32)


def _rmsnorm(x2d, g, w_side=None, tm=256):
    m, d = x2d.shape
    tm = _pick(m, tm)
    g2 = g.reshape(1, d).astype(jnp.float32)
    xs = pl.BlockSpec((tm, d), lambda i: (i, 0))
    gs = pl.BlockSpec((1, d), lambda i: (0, 0))
    if w_side is None:
        return pl.pallas_call(
            _rmsnorm_kernel, grid=(m // tm,), in_specs=[xs, gs], out_specs=xs,
            out_shape=jax.ShapeDtypeStruct((m, d), jnp.bfloat16),
            compiler_params=_params(("parallel",)), name="rmsnorm")(x2d, g2)
    ns = w_side.shape[1]
    return pl.pallas_call(
        _rmsnorm_side_kernel, grid=(m // tm,),
        in_specs=[xs, gs, pl.BlockSpec((d, ns), lambda i: (0, 0))],
        out_specs=[xs, pl.BlockSpec((tm, ns), lambda i: (i, 0))],
        out_shape=[jax.ShapeDtypeStruct((m, d), jnp.bfloat16),
                   jax.ShapeDtypeStruct((m, ns), jnp.float32)],
        compiler_params=_params(("parallel",)), name="rmsnorm_side")(x2d, g2, w_side)


def _matmul_kernel(*refs, n_pairs, has_res):
    o_ref = refs[-1]
    acc = None
    for p in range(n_pairs):
        d = jnp.dot(refs[2 * p][...], refs[2 * p + 1][...], preferred_element_type=jnp.float32)
        acc = d if acc is None else acc + d
    if has_res:
        acc = acc + refs[2 * n_pairs][...]
    o_ref[...] = acc.astype(o_ref.dtype)


def _matmul(pairs, res=None, out_dtype=jnp.float32, tm=1024, tn=512, name="matmul"):
    m = pairs[0][0].shape[0]
    n = pairs[0][1].shape[1]
    tm, tn = _pick(m, tm), _pick(n, tn)
    args, specs = [], []
    for a, w in pairs:
        k = a.shape[1]
        args += [a, w]
        specs += [pl.BlockSpec((tm, k), lambda i, j: (i, 0)), pl.BlockSpec((k, tn), lambda i, j: (0, j))]
    if res is not None:
        args.append(res)
        specs.append(pl.BlockSpec((tm, tn), lambda i, j: (i, j)))
    return pl.pallas_call(
        functools.partial(_matmul_kernel, n_pairs=len(pairs), has_res=res is not None),
        grid=(m // tm, n // tn), in_specs=specs,
        out_specs=pl.BlockSpec((tm, tn), lambda i, j: (i, j)),
        out_shape=jax.ShapeDtypeStruct((m, n), out_dtype),
        compiler_params=_params(("parallel", "arbitrary")), name=name)(*args)


def _gateup_kernel(a_ref, wg_ref, wu_ref, o_ref):
    a = a_ref[...]
    g = jnp.dot(a, wg_ref[...].astype(a.dtype), preferred_element_type=jnp.float32)
    u = jnp.dot(a, wu_ref[...].astype(a.dtype), preferred_element_type=jnp.float32)
    o_ref[...] = (g * _sigmoid(g) * u).astype(o_ref.dtype)


def _gateup(a, wg, wu, tm=1024, tn=256):
    m, k = a.shape
    n = wg.shape[1]
    tm, tn = _pick(m, tm), _pick(n, tn)
    ws = pl.BlockSpec((k, tn), lambda i, j: (0, j))
    return pl.pallas_call(
        _gateup_kernel, grid=(m // tm, n // tn),
        in_specs=[pl.BlockSpec((tm, k), lambda i, j: (i, 0)), ws, ws],
        out_specs=pl.BlockSpec((tm, tn), lambda i, j: (i, j)),
        out_shape=jax.ShapeDtypeStruct((m, n), jnp.bfloat16),
        compiler_params=_params(("parallel", "arbitrary")), name="ffn_gateup")(a, wg, wu)


HALO = 16


def _gdn_prep_kernel(x_ref, halo_ref, w_ref, o_ref, *, ts, tc, blocks_per_seq, nqk_blocks):
    i = pl.program_id(0)
    j = pl.program_id(1)
    first = (i % blocks_per_seq) == 0
    x = x_ref[...].astype(jnp.float32)
    halo8 = jnp.where(first, 0.0, halo_ref[HALO - 8:HALO, :].astype(jnp.float32))
    w = w_ref[...]
    row8 = lax.broadcasted_iota(jnp.int32, (8, tc), 0)
    y = x * w[GDN_CONV - 1:GDN_CONV, :]
    for d in range(1, GDN_CONV):
        xs = pltpu.roll(x, d, 0)
        head = jnp.where(row8 < d, pltpu.roll(halo8, d, 0), xs[0:8])
        xs = jnp.concatenate([head, xs[8:]], axis=0)
        y = y + xs * w[GDN_CONV - 1 - d:GDN_CONV - d, :]
    y = y * _sigmoid(y)
    is_q = j < nqk_blocks
    is_v = j >= 2 * nqk_blocks
    for h in range(tc // GDN_DK):
        yh = y[:, h * GDN_DK:(h + 1) * GDN_DK]
        rs = lax.rsqrt(jnp.sum(yh * yh, axis=-1, keepdims=True) + EPS)
        rs = rs * jnp.where(is_q, GDN_DK ** -0.5, 1.0)
        fac = jnp.where(is_v, 1.0, rs)
        o_ref[:, h * GDN_DK:(h + 1) * GDN_DK] = (yh * fac).astype(o_ref.dtype)


def _gdn_prep(proj, conv_w, seq):
    t = proj.shape[0]
    ts = _pick(seq, 512)
    tc = 512
    nconv = (2 * GDN_QK + GDN_V) // tc
    col0 = OFF_GQ // tc
    rb = ts // HALO
    kern = functools.partial(_gdn_prep_kernel, ts=ts, tc=tc, blocks_per_seq=seq // ts,
                             nqk_blocks=GDN_QK // tc)
    return pl.pallas_call(
        kern, grid=(t // ts, nconv),
        in_specs=[pl.BlockSpec((ts, tc), lambda i, j: (i, col0 + j)),
                  pl.BlockSpec((HALO, tc), lambda i, j: (jnp.maximum(i * rb - 1, 0), col0 + j)),
                  pl.BlockSpec((GDN_CONV, tc), lambda i, j: (0, j))],
        out_specs=pl.BlockSpec((ts, tc), lambda i, j: (i, j)),
        out_shape=jax.ShapeDtypeStruct((t, nconv * tc), jnp.bfloat16),
        compiler_params=_params(("parallel", "parallel")), name="gdn_prep")(proj, proj, conv_w)


def _gate_prep_kernel(s_ref, alog_ref, dtb_ref, o_ref, dec_ref, *, ts):
    C = GDN_CHUNK
    s = s_ref[...]
    lane = lax.broadcasted_iota(jnp.int32, s.shape, 1)
    xa = s + dtb_ref[...]
    softplus = jnp.maximum(xa, 0.0) + jnp.log(1.0 + jnp.exp(-jnp.abs(xa)))
    g = -jnp.exp(alog_ref[...]) * softplus
    g = jnp.where(lane < GDN_HEADS, g, 0.0)
    r = lax.broadcasted_iota(jnp.int32, (ts, ts), 0)
    c = lax.broadcasted_iota(jnp.int32, (ts, ts), 1)
    tri = jnp.where((r // C == c // C) & (c <= r), 1.0, 0.0)
    gcum = jnp.dot(tri, g, precision=HI, preferred_element_type=jnp.float32)
    beta = _sigmoid(s)
    is_beta = (lane >= SM_GB) & (lane < SM_GB + GDN_HEADS)
    o_ref[...] = jnp.where(lane < GDN_HEADS, gcum, jnp.where(is_beta, beta, 0.0))
    gcum_t = gcum.T
    incl = lax.broadcasted_iota(jnp.int32, (C, C), 1) <= lax.broadcasted_iota(jnp.int32, (C, C), 0)
    sub =lax.broadcasted_iota(jnp.int32, (8, C), 0)
    hc = [(h, ci) for h in range(GDN_HEADS) for ci in range(ts // C)]
    cols = [jnp.broadcast_to(gcum[ci * C:(ci + 1) * C, h:h + 1], (C, C)) for h, ci in hc]
    tiles = [gcum_t[8 * (h // 8):8 * (h // 8) + 8, ci * C:(ci + 1) * C] for h, ci in hc]
    rows = [jnp.sum(jnp.where(sub == h % 8, t, 0.0), axis=0, keepdims=True) for (h, ci), t in zip(hc, tiles)]
    for (h, ci), col, row in zip(hc, cols, rows):
        dec_ref[ci * C:(ci + 1) * C, h * C:(h + 1) * C] = jnp.exp(jnp.where(incl, col - row, NEG_BIG))


def _gate_prep(small, a_log, dt_bias):
    t, ns = small.shape
    ts = _pick(t, 256)
    pad = lambda v: jnp.zeros((1, ns), jnp.float32).at[0, :GDN_HEADS].set(v.astype(jnp.float32))
    vs = pl.BlockSpec((1, ns), lambda i: (0, 0))
    bs = pl.BlockSpec((ts, ns), lambda i: (i, 0))
    dw = GDN_HEADS * GDN_CHUNK
    return pl.pallas_call(
        functools.partial(_gate_prep_kernel, ts=ts), grid=(t // ts,),
        in_specs=[bs, vs, vs], out_specs=[bs, pl.BlockSpec((ts, dw), lambda i: (i, 0))],
        out_shape=[jax.ShapeDtypeStruct((t, ns), jnp.float32), jax.ShapeDtypeStruct((t, dw), jnp.float32)],
        compiler_params=_params(("parallel",)), name="gate_prep")(small, pad(a_log), pad(dt_bias))


GDN_WY_HEADS = 4
GDN_WY_ROWS = 512
GDN_WY_CG = 4
GDN_SCAN_HEADS = 8
GDN_SCAN_ROWS = 1024


def _bdot(a, b):
    return jnp.dot(a.astype(jnp.bfloat16), b.astype(jnp.bfloat16), preferred_element_type=jnp.float32)


def _bdot_nt(a, b):
    return lax.dot_general(a.astype(jnp.bfloat16), b.astype(jnp.bfloat16), (((1,), (1,)), ((), ())),
                           preferred_element_type=jnp.float32)


def _head_col(gsl, idx):
    lane = lax.broadcasted_iota(jnp.int32, (1, LANES), 1)
    return jnp.sum(gsl * jnp.where(lane == idx, 1.0, 0.0), axis=1, keepdims=True)


def _gdn_wy_kernel(q_ref, k_ref, v_ref, g_ref, dec_ref, u_ref, w_ref, qd_ref, kd_ref, qk_ref, *, hb):
    hg = pl.program_id(1)
    C, CG = GDN_CHUNK, GDN_WY_CG
    r = lax.broadcasted_iota(jnp.int32, (C, C), 0)
    c = lax.broadcasted_iota(jnp.int32, (C, C), 1)
    off_diag = c != r

    def body(it, carry):
        where, q, k, v, gc, beta, decay = [], [], [], [], [], [], []
        for cc in range(CG):
            r0 = pl.multiple_of((it * CG + cc) * C, C)
            gsl = g_ref[pl.ds(r0, C), :]
            for hh in range(hb):
                sl = slice(hh * GDN_DK, (hh + 1) * GDN_DK)
                h = hg * hb + hh
                where.append((r0, hh))
                q.append(q_ref[pl.ds(r0, C), sl].astype(jnp.float32))
                k.append(k_ref[pl.ds(r0, C), sl].astype(jnp.float32))
                v.append(v_ref[pl.ds(r0, C), sl].astype(jnp.float32))
                gc.append(_head_col(gsl, h))
                beta.append(_head_col(gsl, SM_GB + h))
                decay.append(dec_ref[pl.ds(r0, C), hh * C:(hh + 1) * C])
        n = len(where)
        ch = range(n)
        kb = [k[i] * beta[i] for i in ch]
        eg = [jnp.exp(gc[i]) for i in ch]
        kk = [_bdot_nt(kb[i], k[i]) for i in ch]
        qk = [_bdot_nt(q[i], k[i]) for i in ch]
        pw = [jnp.where(off_diag, kk[i] * decay[i], 0.0) for i in ch]
        x = [jnp.concatenate([v[i] * beta[i], kb[i] * eg[i]], axis=1) for i in ch]
        n_lvl = int(math.log2(C))
        for lvl in range(n_lvl):
            px = [_bdot(pw[i], x[i]) for i in ch]
            if lvl + 1 < n_lvl:
                pw = [_bdot(pw[i], pw[i]) for i in ch]
            x = [x[i] - px[i] if lvl == 0 else x[i] + px[i] for i in ch]
        for i, (r0, hh) in enumerate(where):
            sl = slice(hh * GDN_DK, (hh + 1) * GDN_DK)
            u_ref[pl.ds(r0, C), sl] = x[i][:, :GDN_DV]
            w_ref[pl.ds(r0, C), sl] = x[i][:, GDN_DV:].astype(w_ref.dtype)
            qd_ref[pl.ds(r0, C), sl] = (q[i] * eg[i]).astype(qd_ref.dtype)
            kd_ref[pl.ds(r0, C), sl] = (k[i] * jnp.exp(gc[i][C - 1:C, :] - gc[i])).astype(kd_ref.dtype)
            qk_ref[pl.ds(r0, C), hh * C:(hh + 1) * C] = (qk[i] * decay[i]).astype(qk_ref.dtype)
        return carry

    lax.fori_loop(0, q_ref.shape[0] // (CG * C), body, 0)


def _gdn_scan_kernel(u_ref, w_ref, qd_ref, kd_ref, qk_ref, z_ref, g_ref, gain_ref, o_ref, state_ref, *, hb):
    hg = pl.program_id(1)
    C = GDN_CHUNK
    sub8 = lax.broadcasted_iota(jnp.int32, (8, LANES), 0)
    gain = gain_ref[...]

    @pl.when(pl.program_id(2) == 0)
    def _():
        state_ref[...] = jnp.zeros(state_ref.shape, state_ref.dtype)

    def body(ci, states):
        r0 = pl.multiple_of(ci * C, C)
        g_tail = g_ref[pl.ds(r0 + C - 8, 8), :]
        g_last_row = jnp.sum(jnp.where(sub8 == 7, g_tail, 0.0), axis=0, keepdims=True)
        loaded = []
        for hh in range(hb):
            sl = slice(hh * GDN_DK, (hh + 1) * GDN_DK)
            loaded.append((u_ref[pl.ds(r0, C), sl], w_ref[pl.ds(r0, C), sl], qd_ref[pl.ds(r0, C), sl],
                           kd_ref[pl.ds(r0, C), sl], qk_ref[pl.ds(r0, C), hh * C:(hh + 1) * C],
                           z_ref[pl.ds(r0, C), sl].astype(jnp.float32)))
        hs = range(hb)
        f32 = jnp.float32
        u, w, qd, kd, qk, z = zip(*loaded)
        sb = [states[i].astype(jnp.bfloat16) for i in hs]
        ws = [jnp.dot(w[i], sb[i], preferred_element_type=f32) for i in hs]
        qs = [jnp.dot(qd[i], sb[i], preferred_element_type=f32) for i in hs]
        vb = [(u[i] - ws[i]).astype(jnp.bfloat16) for i in hs]
        kv = [lax.dot_general(kd[i], vb[i], (((0,), (0,)), ((), ())), preferred_element_type=f32) for i in hs]
        qv = [jnp.dot(qk[i], vb[i], preferred_element_type=f32) for i in hs]
        a = [jnp.exp(_head_col(g_last_row, hg * hb + i)) for i in hs]
        new_states = tuple(states[i] * a[i] + kv[i] for i in hs)
        for i in hs:
            o = qs[i] + qv[i]
            on = o * lax.rsqrt(jnp.mean(o * o, axis=-1, keepdims=True) + EPS) * gain
            o_ref[pl.ds(r0, C), i * GDN_DV:(i + 1) * GDN_DV] = (on * (z[i] * _sigmoid(z[i]))).astype(o_ref.dtype)
        return new_states

    init = tuple(state_ref[hh] for hh in range(hb))
    final = lax.fori_loop(0, u_ref.shape[0] // C, body, init)
    for hh in range(hb):
        state_ref[hh] = final[hh]


def _gdn(qkv, proj, gates, decay, gdn_norm, batch, seq):
    t = batch * seq
    bf = jnp.bfloat16
    C = GDN_CHUNK
    hb = min(GDN_WY_HEADS, GDN_HEADS)
    ts = _pick(seq, GDN_WY_ROWS)
    assert GDN_HEADS % hb == 0 and ts % (GDN_WY_CG * C) == 0
    hw = hb * GDN_DK
    kq = GDN_QK // hw
    blk = lambda off: pl.BlockSpec((ts, hw), lambda i, g: (i, off + g))
    oblk = pl.BlockSpec((ts, hw), lambda i, g: (i, g))
    cblk = pl.BlockSpec((ts, hb * C), lambda i, g: (i, g))
    u, w, qd, kd, qk = pl.pallas_call(
        functools.partial(_gdn_wy_kernel, hb=hb), grid=(t // ts, GDN_HEADS // hb),
        in_specs=[blk(0), blk(kq), blk(2 * kq),
                  pl.BlockSpec((ts, gates.shape[1]), lambda i, g: (i, 0)), cblk],
        out_specs=[oblk, oblk, oblk, oblk, cblk],
        out_shape=[jax.ShapeDtypeStruct((t, GDN_V), jnp.float32), jax.ShapeDtypeStruct((t, GDN_QK), bf),
                   jax.ShapeDtypeStruct((t, GDN_QK), bf), jax.ShapeDtypeStruct((t, GDN_QK), bf),
                   jax.ShapeDtypeStruct((t, GDN_HEADS * C), bf)],
        compiler_params=_params(("parallel", "parallel")), name="gdn_wy")(qkv, qkv, qkv, gates, decay)
    hb = min(GDN_SCAN_HEADS, GDN_HEADS)
    ts = _pick(seq, GDN_SCAN_ROWS)
    assert GDN_HEADS % hb == 0
    nsb = seq // ts
    hw = hb * GDN_DK
    zoff = OFF_GZ // hw
    blk = pl.BlockSpec((ts, hw), lambda b, g, s: (b * nsb + s, g))
    return pl.pallas_call(
        functools.partial(_gdn_scan_kernel, hb=hb), grid=(batch, GDN_HEADS // hb, nsb),
        in_specs=[blk, blk, blk, blk,
                  pl.BlockSpec((ts, hb * C), lambda b, g, s: (b * nsb + s, g)),
                  pl.BlockSpec((ts, hw), lambda b, g, s: (b * nsb + s, zoff + g)),
                  pl.BlockSpec((ts, gates.shape[1]), lambda b, g, s: (b * nsb + s, 0)),
                  pl.BlockSpec((1, GDN_DV), lambda b, g, s: (0, 0))],
        out_specs=blk,
        out_shape=jax.ShapeDtypeStruct((t, GDN_V), bf),
        scratch_shapes=[pltpu.VMEM((hb, GDN_DK, GDN_DV), jnp.float32)],
        compiler_params=_params(("parallel", "parallel", "arbitrary")), name="gdn_scan")(
            u, w, qd, kd, qk, proj, gates, gdn_norm.reshape(1, GDN_DV).astype(jnp.float32))


def _dsa_prep_kernel(aq_ref, ak_ref, iq_ref, ik_ref, sm_ref, cos_ref, slo_ref, shi_ref, gq_ref, gk_ref,
                     q_out, k_out, iq_out, ik_out, w_out):
    cosf, slo, shi = cos_ref[...], slo_ref[...], shi_ref[...]
    half = ROT_DIM // 2

    xs = ([aq_ref[:, h * HEAD_DIM:(h + 1) * HEAD_DIM].astype(jnp.float32) for h in range(ATT_HEADS)]
          + [ak_ref[:, h * HEAD_DIM:(h + 1) * HEAD_DIM].astype(jnp.float32) for h in range(ATT_KV_HEADS)])
    gains = [gq_ref[...]] * ATT_HEADS + [gk_ref[...]] * ATT_KV_HEADS
    ms = [jnp.mean(x * x, axis=-1, keepdims=True) for x in xs]
    xn = [x * lax.rsqrt(m + EPS) * g for x, m, g in zip(xs, ms, gains)]
    lo = [pltpu.roll(x, LANES - half, 1) for x in xn]
    hi = [pltpu.roll(x, half, 1) for x in xn]
    ro = [x * cosf + a * slo + b * shi for x, a, b in zip(xn, lo, hi)]
    for h in range(ATT_HEADS):
        q_out[h * Q_BLOCK:(h + 1) * Q_BLOCK, :] = (ro[h] * (HEAD_DIM ** -0.5 * math.log2(math.e))).astype(q_out.dtype)
    for h in range(ATT_KV_HEADS):
        k_out[:, h * HEAD_DIM:(h + 1) * HEAD_DIM] = ro[ATT_HEADS + h].astype(k_out.dtype)
    r = lax.broadcasted_iota(jnp.int32, (IDX_DIM, IDX_DIM), 0)
    c = lax.broadcasted_iota(jnp.int32, (IDX_DIM, IDX_DIM), 1)
    swap = jnp.where(((c < half) & (r == c + half)) | ((c >= half) & (c < 2 * half) & (r == c - half)),
                     1.0, 0.0).astype(jnp.bfloat16)
    sin_signed = slo + shi
    xs = jnp.concatenate([iq_ref[:, h * IDX_DIM:(h + 1) * IDX_DIM] for h in range(IDX_HEADS)]
                         + [ik_ref[...]], axis=0)
    sw = jnp.dot(xs, swap, preferred_element_type=jnp.float32)
    nh = IDX_HEADS + 1
    rot = (xs.astype(jnp.float32).reshape(nh, Q_BLOCK, IDX_DIM) * cosf[None]
           + sw.reshape(nh, Q_BLOCK, IDX_DIM) * sin_signed[None]).reshape(nh * Q_BLOCK, IDX_DIM)
    iq_out[...] = rot[:IDX_HEADS * Q_BLOCK].astype(iq_out.dtype)
    ik_out[...] = rot[IDX_HEADS * Q_BLOCK:].astype(ik_out.dtype)
    w_out[...] = sm_ref[...] * (IDX_HEADS ** -0.5 * IDX_DIM ** -0.5)


def _rope_tables(seq):
    half = ROT_DIM // 2
    inv_freq = ROPE_THETA ** (-jnp.arange(half, dtype=jnp.float32) * 2.0 / ROT_DIM)
    ang = jnp.arange(seq, dtype=jnp.float32)[:, None] * inv_freq[None, :]
    cos, sin = jnp.cos(ang), jnp.sin(ang)
    z = jnp.zeros((seq, LANES - 2 * half), jnp.float32)
    zh = jnp.zeros((seq, half), jnp.float32)
    cosf = jnp.concatenate([cos, cos, jnp.ones_like(z)], axis=1)
    sin_lo = jnp.concatenate([-sin, zh, z], axis=1)
    sin_hi = jnp.concatenate([zh, sin, z], axis=1)
    return cosf, sin_lo, sin_hi


def _dsa_prep(proj, small, att_q_norm, att_k_norm, batch, seq):
    assert ROT_DIM == IDX_ROT
    t = batch * seq
    nq = seq // Q_BLOCK
    qb = Q_BLOCK
    cosf, slo, shi = _rope_tables(seq)
    tab = pl.BlockSpec((qb, LANES), lambda b, i: (i, 0))
    row = lambda w, off: pl.BlockSpec((qb, w), lambda b, i: (b * nq + i, off // w))
    vec = pl.BlockSpec((1, HEAD_DIM), lambda b, i: (0, 0))
    aq_w, ak_w = ATT_HEADS * HEAD_DIM, ATT_KV_HEADS * HEAD_DIM
    return pl.pallas_call(
        _dsa_prep_kernel, grid=(batch, nq),
        in_specs=[row(aq_w, OFF_AQ), row(ak_w, OFF_AK), row(IQ_W, OFF_IQ), row(IDX_DIM, OFF_IK),
                  pl.BlockSpec((qb, small.shape[1]), lambda b, i: (b * nq + i, 0)),
                  tab, tab, tab, vec, vec],
        out_specs=[pl.BlockSpec((ATT_HEADS * qb, HEAD_DIM), lambda b, i: (b * nq + i, 0)),
                   pl.BlockSpec((qb, ak_w), lambda b, i: (b * nq + i, 0)),
                   pl.BlockSpec((IDX_HEADS * qb, IDX_DIM), lambda b, i: (b * nq + i, 0)),
                   pl.BlockSpec((qb, IDX_DIM), lambda b, i: (b * nq + i, 0)),
                   pl.BlockSpec((qb, small.shape[1]), lambda b, i: (b * nq + i, 0))],
        out_shape=[jax.ShapeDtypeStruct((t * ATT_HEADS, HEAD_DIM), jnp.bfloat16),
                   jax.ShapeDtypeStruct((t, ak_w), jnp.bfloat16),
                   jax.ShapeDtypeStruct((t * IDX_HEADS, IDX_DIM), jnp.bfloat16),
                   jax.ShapeDtypeStruct((t, IDX_DIM), jnp.bfloat16),
                   jax.ShapeDtypeStruct((t, small.shape[1]), jnp.float32)],
        compiler_params=_params(("parallel", "parallel")), name="dsa_prep")(
            proj, proj, proj, proj, small, cosf, slo, shi,
            att_q_norm.reshape(1, HEAD_DIM).astype(jnp.float32),
            att_k_norm.reshape(1, HEAD_DIM).astype(jnp.float32))


DSA_KB = 512
DSA_SB = 256
INT_MIN = -2 ** 31


def _dsa_kernel(iq_ref, q_ref, w_ref, ik_ref, k_ref, v_ref, o_ref,
                key_ref, keyt_ref, wb_ref, m_ref, l_ref, acc_ref, *, topk):
    qi = pl.program_id(1)
    QB, KB, SB = Q_BLOCK, DSA_KB, DSA_SB
    n_kb = (qi * QB + QB + KB - 1) // KB
    qpos = qi * QB + lax.broadcasted_iota(jnp.int32, (QB, SB), 0)
    kiota = lax.broadcasted_iota(jnp.int32, (QB, SB), 1)

    w = w_ref[...]
    for h in range(IDX_HEADS):
        wb_ref[h] = jnp.broadcast_to(w[:, SM_IW + h:SM_IW + h + 1], (QB, LANES))

    def score_block(sb, carry):
        k0 = pl.multiple_of(sb * SB, SB)
        lg = lax.dot_general(iq_ref[...], ik_ref[pl.ds(k0, SB), :], (((1,), (1,)), ((), ())),
                             preferred_element_type=jnp.float32)
        sc = None
        for h in range(IDX_HEADS):
            wb = wb_ref[h]
            term = jnp.maximum(lg[h * QB:(h + 1) * QB], 0.0) * jnp.concatenate([wb] * (SB // LANES), axis=1)
            sc = term if sc is None else sc + term
        bits = pltpu.bitcast(sc, jnp.int32)
        keys = bits ^ ((bits >> 31) & 0x7FFFFFFF)
        keys = jnp.where(k0 + kiota <= qpos, keys, INT_MIN)
        key_ref[:, pl.ds(k0, SB)] = keys
        keyt_ref[pl.ds(k0, SB), :] = keys.T
        return carry

    n_sb = (qi * QB + QB + SB - 1) // SB
    lax.fori_loop(0, n_sb, score_block, 0)

    @pl.when(n_sb < n_kb * (KB // SB))
    def _():
        k0 = pl.multiple_of(n_sb * SB, SB)
        key_ref[:, pl.ds(k0, SB)] = jnp.full((QB, SB), INT_MIN, jnp.int32)
        keyt_ref[pl.ds(k0, SB), :] = jnp.full((SB, QB), INT_MIN, jnp.int32)

    def count_ge(cand):
        def blk(kb, cnt):
            k0 = pl.multiple_of(kb * KB, KB)
            parts = [jnp.where(keyt_ref[pl.ds(k0 + 8 * j, 8), :] >= cand, 1.0, 0.0) for j in range(KB // 8)]
            while len(parts) > 1:
                parts = [parts[j] + parts[j + 1] for j in range(0, len(parts), 2)]
            return cnt + parts[0]
        cnt = lax.fori_loop(0, n_kb, blk, jnp.zeros((8, LANES), jnp.float32))
        return jnp.sum(cnt, axis=0, keepdims=True)

    thr0 = jnp.full((8, LANES), INT_MIN, jnp.int32)
    thr0 = jnp.where(count_ge(jnp.zeros((8, LANES), jnp.int32)) >= topk, 0, thr0)

    def bit_step(i, thr):
        cand = thr | (jnp.int32(1) << (30 - i))
        return jnp.where(count_ge(cand) >= topk, cand, thr)

    thr = lax.fori_loop(0, 31, bit_step, thr0)
    thr = jnp.maximum(thr, INT_MIN + 1)
    thr = jnp.concatenate([thr] * (QB // 8), axis=0).T

    G = ATT_GROUP
    NG = KB // LANES
    m_ref[...] = jnp.full(m_ref.shape, NEG_BIG, jnp.float32)
    l_ref[...] = jnp.zeros(l_ref.shape, jnp.float32)
    acc_ref[...] = jnp.zeros(acc_ref.shape, jnp.float32)

    def lane_fold(x, op):
        parts = [x[:, g * LANES:(g + 1) * LANES] for g in range(NG)]
        while len(parts) > 1:
            parts = [op(parts[i], parts[i + 1]) for i in range(0, len(parts), 2)]
        return parts[0]

    def attn_block(kb, carry):
        k0 = pl.multiple_of(kb * KB, KB)
        bias = jnp.where(key_ref[:, pl.ds(k0, KB)] >= jnp.concatenate([thr] * NG, axis=1), 0.0, NEG_BIG)
        bias_g = jnp.concatenate([bias] * G, axis=0)
        grp = range(ATT_KV_HEADS)
        rows = [slice(hk * G * QB, (hk + 1) * G * QB) for hk in grp]
        cols = [slice(hk * HEAD_DIM, (hk + 1) * HEAD_DIM) for hk in grp]
        s = [lax.dot_general(q_ref[rows[i], :], k_ref[pl.ds(k0, KB), cols[i]], (((1,), (1,)), ((), ())),
                             preferred_element_type=jnp.float32) for i in grp]
        p, alpha, m_new = [], [], []
        for i in grp:
            si = s[i] + bias_g
            m_old = m_ref[rows[i], :]
            mi = jnp.maximum(m_old, jnp.max(lane_fold(si, jnp.maximum), axis=1, keepdims=True))
            p.append(jnp.exp2(si - jnp.concatenate([mi] * NG, axis=1)).astype(jnp.bfloat16))
            alpha.append(jnp.exp2(m_old - mi))
            m_new.append(mi)
        ones = jnp.ones((KB, LANES), jnp.bfloat16)
        pv = [jnp.dot(p[i], jnp.concatenate([v_ref[pl.ds(k0, KB), cols[i]], ones], axis=1),
                      preferred_element_type=jnp.float32) for i in grp]
        for i in grp:
            l_ref[rows[i], :] = alpha[i] * l_ref[rows[i], :] + pv[i][:, HEAD_DIM:]
            acc_ref[rows[i], :] = alpha[i] * acc_ref[rows[i], :] + pv[i][:, :HEAD_DIM]
            m_ref[rows[i], :] = m_new[i]
        return carry

    lax.fori_loop(0, n_kb, attn_block, 0)
    for h in range(ATT_HEADS):
        rows = slice(h * QB, (h + 1) * QB)
        o_ref[:, h * HEAD_DIM:(h + 1) * HEAD_DIM] = (acc_ref[rows, :] / l_ref[rows, :]).astype(o_ref.dtype)


def _dsa(iq_hm, q_hm, w_s, ik_rot, k_att, proj, batch, seq, topk):
    nq = seq // Q_BLOCK
    qb = Q_BLOCK
    kvw = ATT_KV_HEADS * HEAD_DIM
    return pl.pallas_call(
        functools.partial(_dsa_kernel, topk=topk), grid=(batch, nq),
        in_specs=[pl.BlockSpec((IDX_HEADS * qb, IDX_DIM), lambda b, i: (b * nq + i, 0)),
                  pl.BlockSpec((ATT_HEADS * qb, HEAD_DIM), lambda b, i: (b * nq + i, 0)),
                  pl.BlockSpec((qb, w_s.shape[1]), lambda b, i: (b * nq + i, 0)),
                  pl.BlockSpec((seq, IDX_DIM), lambda b, i: (b, 0)),
                  pl.BlockSpec((seq, kvw), lambda b, i: (b, 0)),
                  pl.BlockSpec((seq, kvw), lambda b, i: (b, OFF_AV // kvw))],
        out_specs=pl.BlockSpec((qb, ATT_HEADS * HEAD_DIM), lambda b, i: (b * nq + i, 0)),
        out_shape=jax.ShapeDtypeStruct((batch * seq, ATT_HEADS * HEAD_DIM), jnp.bfloat16),
        scratch_shapes=[pltpu.VMEM((qb, seq), jnp.int32),
                        pltpu.VMEM((seq, qb), jnp.int32),
                        pltpu.VMEM((IDX_HEADS, qb, LANES), jnp.float32),
                        pltpu.VMEM((ATT_HEADS * qb, LANES), jnp.float32),
                        pltpu.VMEM((ATT_HEADS * qb, LANES), jnp.float32),
                        pltpu.VMEM((ATT_HEADS * qb, HEAD_DIM), jnp.float32)],
        compiler_params=_params(("parallel", "arbitrary")), name="dsa")(
            iq_hm, q_hm, w_s, ik_rot, k_att, proj)


def _mem_attn_kernel(q_ref, kv_ref, gq_ref, gk_ref, o_ref):
    def norm(x, gain):
        return x * lax.rsqrt(jnp.mean(x * x, axis=-1, keepdims=True) + EPS) * gain

    for h in range(MEM_HEADS):
        cols = slice(h * MEM_HEAD_DIM, (h + 1) * MEM_HEAD_DIM)
        q = norm(q_ref[:, cols], gq_ref[...]) * (MEM_HEAD_DIM ** -0.5)
        k = norm(kv_ref[:, cols], gk_ref[...])
        v = kv_ref[:, MEM_WIDTH + h * MEM_HEAD_DIM:MEM_WIDTH + (h + 1) * MEM_HEAD_DIM]
        s = _bdot_nt(q, k)
        p = jnp.exp(s - jnp.max(s, axis=1, keepdims=True))
        o = _bdot(p, v) / jnp.sum(p, axis=1, keepdims=True)
        o_ref[:, cols] = o.astype(o_ref.dtype)


def _mem_attn(mq, mkv, gq, gk, batch, seq, n_mem):
    tq = _pick(seq, 512)
    nq = seq // tq
    vec = pl.BlockSpec((1, MEM_HEAD_DIM), lambda b, i: (0, 0))
    return pl.pallas_call(
        _mem_attn_kernel, grid=(batch, nq),
        in_specs=[pl.BlockSpec((tq, MEM_WIDTH), lambda b, i: (b * nq + i, 0)),
                  pl.BlockSpec((n_mem, 2 * MEM_WIDTH), lambda b, i: (b, 0)), vec, vec],
        out_specs=pl.BlockSpec((tq, MEM_WIDTH), lambda b, i: (b * nq + i, 0)),
        out_shape=jax.ShapeDtypeStruct((batch * seq, MEM_WIDTH), jnp.bfloat16),
        compiler_params=_params(("parallel", "parallel")), name="mem_attn")(
            mq, mkv, gq.reshape(1, MEM_HEAD_DIM).astype(jnp.float32),
            gk.reshape(1, MEM_HEAD_DIM).astype(jnp.float32))


def _split_w_in(w):
    sizes = (GDN_QK, GDN_QK, GDN_V, GDN_V, GDN_HEADS, GDN_HEADS, ATT_HEADS * HEAD_DIM,
             ATT_KV_HEADS * HEAD_DIM, ATT_KV_HEADS * HEAD_DIM, IQ_W, IDX_DIM, IDX_HEADS)
    offs = np.concatenate([[0], np.cumsum(sizes)])
    seg = lambda i: w[:, int(offs[i]):int(offs[i + 1])]
    gq, gk, gv, gz, ga, gb, aq, ak, av, iq, ik, iw = (seg(i) for i in range(12))
    d = w.shape[0]
    main = jnp.concatenate([iq, gq, gk, gv, gz, aq, ak, av, ik,
                            jnp.zeros((d, N_MAIN_PAD - N_MAIN), w.dtype)], axis=1).astype(jnp.bfloat16)
    nsm = 2 * GDN_HEADS + IDX_HEADS
    small = jnp.concatenate([ga, gb, iw, jnp.zeros((d, LANES - nsm), w.dtype)], axis=1)
    return main, small


def kernel(x, mem, norm_mix, w_in, conv_w, a_log, dt_bias, gdn_norm, att_q_norm, att_k_norm, w_out,
           norm_mem_q, norm_mem_kv, w_mem_q, w_mem_kv, mem_q_norm, mem_k_norm, w_mem_o, norm_ffn,
           w_gate, w_up, w_down):
    B, S, D = x.shape
    M = mem.shape[1]
    T = B * S
    topk = min(TOPK_MAX, S // 4)
    bf = jnp.bfloat16
    x2 = x.reshape(T, D)
    for l in range(w_in.shape[0]):
        w_main, w_small = _split_w_in(w_in[l])
        h, small = _rmsnorm(x2, norm_mix[l], w_small)
        proj = _matmul([(h, w_main)], out_dtype=bf, tn=IN_TN, name="in_proj")
        qkv = _gdn_prep(proj, conv_w[l], S)
        gates, decay = _gate_prep(small, a_log[l], dt_bias[l])
        o_a = _gdn(qkv, proj, gates, decay, gdn_norm[l], B, S)
        q_hm, k_att, iq_hm, ik_rot, w_s = _dsa_prep(proj, small, att_q_norm[l], att_k_norm[l], B, S)
        o_b = _dsa(iq_hm, q_hm, w_s, ik_rot, k_att, proj, B, S, topk)
        wo = w_out[l].astype(bf)
        x2 = _matmul([(o_a, wo[:GDN_V]), (o_b, wo[GDN_V:])], res=x2, name="out_proj")
        hq = _rmsnorm(x2, norm_mem_q[l])
        hm = _rmsnorm(mem.reshape(B * M, D), norm_mem_kv[l])
        mq = _matmul([(hq, w_mem_q[l].astype(bf))], name="mem_q")
        mkv = _matmul([(hm, w_mem_kv[l].astype(bf))], name="mem_kv")
        mo = _mem_attn(mq, mkv, mem_q_norm[l], mem_k_norm[l], B, S, M)
        x2 = _matmul([(mo, w_mem_o[l].astype(bf))], res=x2, tn=1024, name="mem_o")
        hf = _rmsnorm(x2, norm_ffn[l])
        mid = _gateup(hf, w_gate[l], w_up[l])
        x2 = _matmul([(mid, w_down[l].astype(bf))], res=x2, tm=512, tn=256, name="ffn_down")
    return x2.reshape(B, S, D)
```

```python
import functools
import math

import jax
import jax.numpy as jnp
import numpy as np
from jax import lax
from jax.experimental import pallas as pl
from jax.experimental.pallas import tpu as pltpu

D_MODEL = 4096
ATT_KV_HEADS = 4
IDX_HEADS = 32
TOPK_MAX = 256

EPS = 1e-6
ROPE_THETA = 500000.0
HEAD_DIM = 128
ROT_DIM = HEAD_DIM // 4
GDN_DK = 128
GDN_DV = 128
GDN_HEADS = (D_MODEL // 2) // GDN_DV
GDN_CONV = 4
GDN_CHUNK = 64
GDN_QK = GDN_HEADS * GDN_DK
GDN_V = GDN_HEADS * GDN_DV
ATT_HEADS = (D_MODEL // 2) // HEAD_DIM
ATT_GROUP = ATT_HEADS // ATT_KV_HEADS
IDX_DIM = 128
IDX_ROT = IDX_DIM // 4
Q_BLOCK = 128
MEM_HEADS = 4
MEM_HEAD_DIM = 128
MEM_WIDTH = MEM_HEADS * MEM_HEAD_DIM

LANES = 128
VMEM_LIMIT = 56 * 1024 * 1024
NEG_BIG = -1e30

IQ_W = IDX_HEADS * IDX_DIM
OFF_IQ = 0
OFF_GQ = OFF_IQ + IQ_W
OFF_GK = OFF_GQ + GDN_QK
OFF_GV = OFF_GK + GDN_QK
OFF_GZ = OFF_GV + GDN_V
OFF_AQ = OFF_GZ + GDN_V
OFF_AK = OFF_AQ + ATT_HEADS * HEAD_DIM
OFF_AV = OFF_AK + ATT_KV_HEADS * HEAD_DIM
OFF_IK = OFF_AV + ATT_KV_HEADS * HEAD_DIM
N_MAIN = OFF_IK + IDX_DIM
IN_TN = 512
N_MAIN_PAD = -(-N_MAIN // IN_TN) * IN_TN
SM_GA = 0
SM_GB = GDN_HEADS
SM_IW = 2 * GDN_HEADS
HI = lax.Precision.HIGHEST


def _params(sem):
    return pltpu.CompilerParams(dimension_semantics=sem, vmem_limit_bytes=VMEM_LIMIT)


def _pick(n, pref):
    t = min(n, pref)
    while n % t:
        t //= 2
    return t


def _sigmoid(x):
    return 1.0 / (1.0 + jnp.exp2(x * (-math.log2(math.e))))


def _rmsnorm_kernel(x_ref, g_ref, o_ref):
    x = x_ref[...]
    ms = jnp.mean(x * x, axis=-1, keepdims=True)
    o_ref[...] = (x * lax.rsqrt(ms + EPS) * g_ref[...]).astype(o_ref.dtype)


def _rmsnorm_side_kernel(x_ref, g_ref, whi_ref, wlo_ref, o_ref, s_ref):
    x = x_ref[...]
    ms = jnp.mean(x * x, axis=-1, keepdims=True)
    xn = x * lax.rsqrt(ms + EPS) * g_ref[...]
    hi = xn.astype(o_ref.dtype)
    o_ref[...] = hi
    lo = (xn - hi.astype(jnp.float32)).astype(hi.dtype)
    f32 = jnp.float32
    s_ref[...] = (jnp.dot(hi, whi_ref[...], preferred_element_type=f32)
                  + jnp.dot(lo, whi_ref[...], preferred_element_type=f32)
                  + jnp.dot(hi, wlo_ref[...], preferred_element_type=f32))


def _rmsnorm(x2d, g, w_side=None, tm=256):
    m, d = x2d.shape
    tm = _pick(m, tm)
    g2 = g.reshape(1, d).astype(jnp.float32)
    xs = pl.BlockSpec((tm, d), lambda i: (i, 0))
    gs = pl.BlockSpec((1, d), lambda i: (0, 0))
    if w_side is None:
        return pl.pallas_call(
            _rmsnorm_kernel, grid=(m // tm,), in_specs=[xs, gs], out_specs=xs,
            out_shape=jax.ShapeDtypeStruct((m, d), jnp.bfloat16),
            compiler_params=_params(("parallel",)), name="rmsnorm")(x2d, g2)
    w_hi, w_lo = w_side
    ns = w_hi.shape[1]
    ws = pl.BlockSpec((d, ns), lambda i: (0, 0))
    return pl.pallas_call(
        _rmsnorm_side_kernel, grid=(m // tm,),
        in_specs=[xs, gs, ws, ws],
        out_specs=[xs, pl.BlockSpec((tm, ns), lambda i: (i, 0))],
        out_shape=[jax.ShapeDtypeStruct((m, d), jnp.bfloat16),
                   jax.ShapeDtypeStruct((m, ns), jnp.float32)],
        compiler_params=_params(("parallel",)), name="rmsnorm_side")(x2d, g2, w_hi, w_lo)


def _matmul_kernel(*refs, n_pairs, has_res):
    o_ref = refs[-1]
    acc = None
    for p in range(n_pairs):
        d = jnp.dot(refs[2 * p][...], refs[2 * p + 1][...], preferred_element_type=jnp.float32)
        acc = d if acc is None else acc + d
    if has_res:
        acc = acc + refs[2 * n_pairs][...]
    o_ref[...] = acc.astype(o_ref.dtype)


def _matmul(pairs, res=None, out_dtype=jnp.float32, tm=1024, tn=512, name="matmul"):
    m = pairs[0][0].shape[0]
    n = pairs[0][1].shape[1]
    tm, tn = _pick(m, tm), _pick(n, tn)
    args, specs = [], []
    for a, w in pairs:
        k = a.shape[1]
        args += [a, w]
        specs += [pl.BlockSpec((tm, k), lambda i, j: (i, 0)), pl.BlockSpec((k, tn), lambda i, j: (0, j))]
    if res is not None:
        args.append(res)
        specs.append(pl.BlockSpec((tm, tn), lambda i, j: (i, j)))
    return pl.pallas_call(
        functools.partial(_matmul_kernel, n_pairs=len(pairs), has_res=res is not None),
        grid=(m // tm, n // tn), in_specs=specs,
        out_specs=pl.BlockSpec((tm, tn), lambda i, j: (i, j)),
        out_shape=jax.ShapeDtypeStruct((m, n), out_dtype),
        compiler_params=_params(("parallel", "arbitrary")), name=name)(*args)


def _gateup_kernel(a_ref, wg_ref, wu_ref, o_ref):
    a = a_ref[...]
    g = jnp.dot(a, wg_ref[...].astype(a.dtype), preferred_element_type=jnp.float32)
    u = jnp.dot(a, wu_ref[...].astype(a.dtype), preferred_element_type=jnp.float32)
    o_ref[...] = (g * _sigmoid(g) * u).astype(o_ref.dtype)


def _gateup(a, wg, wu, tm=1024, tn=256):
    m, k = a.shape
    n = wg.shape[1]
    tm, tn = _pick(m, tm), _pick(n, tn)
    ws = pl.BlockSpec((k, tn), lambda i, j: (0, j))
    return pl.pallas_call(
        _gateup_kernel, grid=(m // tm, n // tn),
        in_specs=[pl.BlockSpec((tm, k), lambda i, j: (i, 0)), ws, ws],
        out_specs=pl.BlockSpec((tm, tn), lambda i, j: (i, j)),
        out_shape=jax.ShapeDtypeStruct((m, n), jnp.bfloat16),
        compiler_params=_params(("parallel", "arbitrary")), name="ffn_gateup")(a, wg, wu)


HALO = 16


def _gdn_prep_kernel(x_ref, halo_ref, w_ref, o_ref, *, ts, tc, blocks_per_seq, nqk_blocks):
    i = pl.program_id(0)
    j = pl.program_id(1)
    first = (i % blocks_per_seq) == 0
    x = x_ref[...].astype(jnp.float32)
    halo8 = jnp.where(first, 0.0, halo_ref[HALO - 8:HALO, :].astype(jnp.float32))
    w = w_ref[...]
    row8 = lax.broadcasted_iota(jnp.int32, (8, tc), 0)
    y = x * w[GDN_CONV - 1:GDN_CONV, :]
    for d in range(1, GDN_CONV):
        xs = pltpu.roll(x, d, 0)
        head = jnp.where(row8 < d, pltpu.roll(halo8, d, 0), xs[0:8])
        xs = jnp.concatenate([head, xs[8:]], axis=0)
        y = y + xs * w[GDN_CONV - 1 - d:GDN_CONV - d, :]
    y = y * _sigmoid(y)
    is_q = j < nqk_blocks
    is_v = j >= 2 * nqk_blocks
    for h in range(tc // GDN_DK):
        yh = y[:, h * GDN_DK:(h + 1) * GDN_DK]
        rs = lax.rsqrt(jnp.sum(yh * yh, axis=-1, keepdims=True) + EPS)
        rs = rs * jnp.where(is_q, GDN_DK ** -0.5, 1.0)
        fac = jnp.where(is_v, 1.0, rs)
        o_ref[:, h * GDN_DK:(h + 1) * GDN_DK] = (yh * fac).astype(o_ref.dtype)


def _gdn_prep(proj, conv_w, seq):
    t = proj.shape[0]
    ts = _pick(seq, 512)
    tc = 512
    nconv = (2 * GDN_QK + GDN_V) // tc
    col0 = OFF_GQ // tc
    rb = ts // HALO
    kern = functools.partial(_gdn_prep_kernel, ts=ts, tc=tc, blocks_per_seq=seq // ts,
                             nqk_blocks=GDN_QK // tc)
    return pl.pallas_call(
        kern, grid=(t // ts, nconv),
        in_specs=[pl.BlockSpec((ts, tc), lambda i, j: (i, col0 + j)),
                  pl.BlockSpec((HALO, tc), lambda i, j: (jnp.maximum(i * rb - 1, 0), col0 + j)),
                  pl.BlockSpec((GDN_CONV, tc), lambda i, j: (0, j))],
        out_specs=pl.BlockSpec((ts, tc), lambda i, j: (i, j)),
        out_shape=jax.ShapeDtypeStruct((t, nconv * tc), jnp.bfloat16),
        compiler_params=_params(("parallel", "parallel")), name="gdn_prep")(proj, proj, conv_w)


def _gate_prep_kernel(s_ref, alog_ref, dtb_ref, o_ref, dec_ref, *, ts):
    C = GDN_CHUNK
    s = s_ref[...]
    lane = lax.broadcasted_iota(jnp.int32, s.shape, 1)
    xa = s + dtb_ref[...]
    softplus = jnp.maximum(xa, 0.0) + jnp.log(1.0 + jnp.exp(-jnp.abs(xa)))
    g = -jnp.exp(alog_ref[...]) * softplus
    g = jnp.where(lane < GDN_HEADS, g, 0.0)
    r = lax.broadcasted_iota(jnp.int32, (ts, ts), 0)
    c = lax.broadcasted_iota(jnp.int32, (ts, ts), 1)
    tri = jnp.where((r // C == c // C) & (c <= r), 1.0, 0.0)
    gcum = jnp.dot(tri, g, precision=HI, preferred_element_type=jnp.float32)
    beta = _sigmoid(s)
    is_beta = (lane >= SM_GB) & (lane < SM_GB + GDN_HEADS)
    o_ref[...] = jnp.where(lane < GDN_HEADS, gcum, jnp.where(is_beta, beta, 0.0))
    gcum_t = gcum.T
    incl = lax.broadcasted_iota(jnp.int32, (C, C), 1) <= lax.broadcasted_iota(jnp.int32, (C, C), 0)
    sub =lax.broadcasted_iota(jnp.int32, (8, C), 0)
    hc = [(h, ci) for h in range(GDN_HEADS) for ci in range(ts // C)]
    cols = [jnp.broadcast_to(gcum[ci * C:(ci + 1) * C, h:h + 1], (C, C)) for h, ci in hc]
    tiles = [gcum_t[8 * (h // 8):8 * (h // 8) + 8, ci * C:(ci + 1) * C] for h, ci in hc]
    rows = [jnp.sum(jnp.where(sub == h % 8, t, 0.0), axis=0, keepdims=True) for (h, ci), t in zip(hc, tiles)]
    for (h, ci), col, row in zip(hc, cols, rows):
        dec_ref[ci * C:(ci + 1) * C, h * C:(h + 1) * C] = jnp.exp(jnp.where(incl, col - row, NEG_BIG))


def _gate_prep(small, a_log, dt_bias):
    t, ns = small.shape
    ts = _pick(t, 256)
    pad = lambda v: jnp.zeros((1, ns), jnp.float32).at[0, :GDN_HEADS].set(v.astype(jnp.float32))
    vs = pl.BlockSpec((1, ns), lambda i: (0, 0))
    bs = pl.BlockSpec((ts, ns), lambda i: (i, 0))
    dw = GDN_HEADS * GDN_CHUNK
    return pl.pallas_call(
        functools.partial(_gate_prep_kernel, ts=ts), grid=(t // ts,),
        in_specs=[bs, vs, vs], out_specs=[bs, pl.BlockSpec((ts, dw), lambda i: (i, 0))],
        out_shape=[jax.ShapeDtypeStruct((t, ns), jnp.float32), jax.ShapeDtypeStruct((t, dw), jnp.float32)],
        compiler_params=_params(("parallel",)), name="gate_prep")(small, pad(a_log), pad(dt_bias))


GDN_WY_HEADS = 4
GDN_WY_ROWS = 512
GDN_WY_CG = 4
GDN_SCAN_HEADS = 8
GDN_SCAN_ROWS = 1024


def _bdot(a, b):
    return jnp.dot(a.astype(jnp.bfloat16), b.astype(jnp.bfloat16), preferred_element_type=jnp.float32)


def _bdot_nt(a, b):
    return lax.dot_general(a.astype(jnp.bfloat16), b.astype(jnp.bfloat16), (((1,), (1,)), ((), ())),
                           preferred_element_type=jnp.float32)


def _head_col(gsl, idx):
    lane = lax.broadcasted_iota(jnp.int32, (1, LANES), 1)
    return jnp.sum(gsl * jnp.where(lane == idx, 1.0, 0.0), axis=1, keepdims=True)


def _gdn_wy_kernel(q_ref, k_ref, v_ref, g_ref, dec_ref, u_ref, w_ref, qd_ref, kd_ref, qk_ref, *, hb):
    hg = pl.program_id(1)
    C, CG = GDN_CHUNK, GDN_WY_CG
    r = lax.broadcasted_iota(jnp.int32, (C, C), 0)
    c = lax.broadcasted_iota(jnp.int32, (C, C), 1)
    off_diag = c != r

    def body(it, carry):
        where, q, k, v, gc, beta, decay = [], [], [], [], [], [], []
        for cc in range(CG):
            r0 = pl.multiple_of((it * CG + cc) * C, C)
            gsl = g_ref[pl.ds(r0, C), :]
            for hh in range(hb):
                sl = slice(hh * GDN_DK, (hh + 1) * GDN_DK)
                h = hg * hb + hh
                where.append((r0, hh))
                q.append(q_ref[pl.ds(r0, C), sl].astype(jnp.float32))
                k.append(k_ref[pl.ds(r0, C), sl].astype(jnp.float32))
                v.append(v_ref[pl.ds(r0, C), sl].astype(jnp.float32))
                gc.append(_head_col(gsl, h))
                beta.append(_head_col(gsl, SM_GB + h))
                decay.append(dec_ref[pl.ds(r0, C), hh * C:(hh + 1) * C])
        n = len(where)
        ch = range(n)
        kb = [k[i] * beta[i] for i in ch]
        eg = [jnp.exp(gc[i]) for i in ch]
        kk = [_bdot_nt(kb[i], k[i]) for i in ch]
        qk = [_bdot_nt(q[i], k[i]) for i in ch]
        pw = [jnp.where(off_diag, kk[i] * decay[i], 0.0) for i in ch]
        x = [jnp.concatenate([v[i] * beta[i], kb[i] * eg[i]], axis=1) for i in ch]
        n_lvl = int(math.log2(C))
        for lvl in range(n_lvl):
            px = [_bdot(pw[i], x[i]) for i in ch]
            if lvl + 1 < n_lvl:
                pw = [_bdot(pw[i], pw[i]) for i in ch]
            x = [x[i] - px[i] if lvl == 0 else x[i] + px[i] for i in ch]
        for i, (r0, hh) in enumerate(where):
            sl = slice(hh * GDN_DK, (hh + 1) * GDN_DK)
            u_ref[pl.ds(r0, C), sl] = x[i][:, :GDN_DV]
            w_ref[pl.ds(r0, C), sl] = x[i][:, GDN_DV:].astype(w_ref.dtype)
            qd_ref[pl.ds(r0, C), sl] = (q[i] * eg[i]).astype(qd_ref.dtype)
            kd_ref[pl.ds(r0, C), sl] = (k[i] * jnp.exp(gc[i][C - 1:C, :] - gc[i])).astype(kd_ref.dtype)
            qk_ref[pl.ds(r0, C), hh * C:(hh + 1) * C] = (qk[i] * decay[i]).astype(qk_ref.dtype)
        return carry

    lax.fori_loop(0, q_ref.shape[0] // (CG * C), body, 0)


def _gdn_scan_kernel(u_ref, w_ref, qd_ref, kd_ref, qk_ref, z_ref, g_ref, gain_ref, o_ref, state_ref, *, hb):
    hg = pl.program_id(1)
    C = GDN_CHUNK
    sub8 = lax.broadcasted_iota(jnp.int32, (8, LANES), 0)
    gain = gain_ref[...]

    @pl.when(pl.program_id(2) == 0)
    def _():
        state_ref[...] = jnp.zeros(state_ref.shape, state_ref.dtype)

    def body(ci, states):
        r0 = pl.multiple_of(ci * C, C)
        g_tail = g_ref[pl.ds(r0 + C - 8, 8), :]
        g_last_row = jnp.sum(jnp.where(sub8 == 7, g_tail, 0.0), axis=0, keepdims=True)
        loaded = []
        for hh in range(hb):
            sl = slice(hh * GDN_DK, (hh + 1) * GDN_DK)
            loaded.append((u_ref[pl.ds(r0, C), sl], w_ref[pl.ds(r0, C), sl], qd_ref[pl.ds(r0, C), sl],
                           kd_ref[pl.ds(r0, C), sl], qk_ref[pl.ds(r0, C), hh * C:(hh + 1) * C],
                           z_ref[pl.ds(r0, C), sl].astype(jnp.float32)))
        hs = range(hb)
        f32 = jnp.float32
        u, w, qd, kd, qk, z = zip(*loaded)
        sb = [states[i].astype(jnp.bfloat16) for i in hs]
        ws = [jnp.dot(w[i], sb[i], preferred_element_type=f32) for i in hs]
        qs = [jnp.dot(qd[i], sb[i], preferred_element_type=f32) for i in hs]
        vb = [(u[i] - ws[i]).astype(jnp.bfloat16) for i in hs]
        kv = [lax.dot_general(kd[i], vb[i], (((0,), (0,)), ((), ())), preferred_element_type=f32) for i in hs]
        qv = [jnp.dot(qk[i], vb[i], preferred_element_type=f32) for i in hs]
        a = [jnp.exp(_head_col(g_last_row, hg * hb + i)) for i in hs]
        new_states = tuple(states[i] * a[i] + kv[i] for i in hs)
        for i in hs:
            o = qs[i] + qv[i]
            on = o * lax.rsqrt(jnp.mean(o * o, axis=-1, keepdims=True) + EPS) * gain
            o_ref[pl.ds(r0, C), i * GDN_DV:(i + 1) * GDN_DV] = (on * (z[i] * _sigmoid(z[i]))).astype(o_ref.dtype)
        return new_states

    init = tuple(state_ref[hh] for hh in range(hb))
    final = lax.fori_loop(0, u_ref.shape[0] // C, body, init)
    for hh in range(hb):
        state_ref[hh] = final[hh]


def _gdn(qkv, proj, gates, decay, gdn_norm, batch, seq):
    t = batch * seq
    bf = jnp.bfloat16
    C = GDN_CHUNK
    hb = min(GDN_WY_HEADS, GDN_HEADS)
    ts = _pick(seq, GDN_WY_ROWS)
    assert GDN_HEADS % hb == 0 and ts % (GDN_WY_CG * C) == 0
    hw = hb * GDN_DK
    kq = GDN_QK // hw
    blk = lambda off: pl.BlockSpec((ts, hw), lambda i, g: (i, off + g))
    oblk = pl.BlockSpec((ts, hw), lambda i, g: (i, g))
    cblk = pl.BlockSpec((ts, hb * C), lambda i, g: (i, g))
    u, w, qd, kd, qk = pl.pallas_call(
        functools.partial(_gdn_wy_kernel, hb=hb), grid=(t // ts, GDN_HEADS // hb),
        in_specs=[blk(0), blk(kq), blk(2 * kq),
                  pl.BlockSpec((ts, gates.shape[1]), lambda i, g: (i, 0)), cblk],
        out_specs=[oblk, oblk, oblk, oblk, cblk],
        out_shape=[jax.ShapeDtypeStruct((t, GDN_V), jnp.float32), jax.ShapeDtypeStruct((t, GDN_QK), bf),
                   jax.ShapeDtypeStruct((t, GDN_QK), bf), jax.ShapeDtypeStruct((t, GDN_QK), bf),
                   jax.ShapeDtypeStruct((t, GDN_HEADS * C), bf)],
        compiler_params=_params(("parallel", "parallel")), name="gdn_wy")(qkv, qkv, qkv, gates, decay)
    hb = min(GDN_SCAN_HEADS, GDN_HEADS)
    ts = _pick(seq, GDN_SCAN_ROWS)
    assert GDN_HEADS % hb == 0
    nsb = seq // ts
    hw = hb * GDN_DK
    zoff = OFF_GZ // hw
    blk = pl.BlockSpec((ts, hw), lambda b, g, s: (b * nsb + s, g))
    return pl.pallas_call(
        functools.partial(_gdn_scan_kernel, hb=hb), grid=(batch, GDN_HEADS // hb, nsb),
        in_specs=[blk, blk, blk, blk,
                  pl.BlockSpec((ts, hb * C), lambda b, g, s: (b * nsb + s, g)),
                  pl.BlockSpec((ts, hw), lambda b, g, s: (b * nsb + s, zoff + g)),
                  pl.BlockSpec((ts, gates.shape[1]), lambda b, g, s: (b * nsb + s, 0)),
                  pl.BlockSpec((1, GDN_DV), lambda b, g, s: (0, 0))],
        out_specs=blk,
        out_shape=jax.ShapeDtypeStruct((t, GDN_V), bf),
        scratch_shapes=[pltpu.VMEM((hb, GDN_DK, GDN_DV), jnp.float32)],
        compiler_params=_params(("parallel", "parallel", "arbitrary")), name="gdn_scan")(
            u, w, qd, kd, qk, proj, gates, gdn_norm.reshape(1, GDN_DV).astype(jnp.float32))


def _dsa_prep_kernel(aq_ref, ak_ref, iq_ref, ik_ref, sm_ref, cos_ref, slo_ref, shi_ref, gq_ref, gk_ref,
                     q_out, k_out, iq_out, ik_out, w_out):
    cosf, slo, shi = cos_ref[...], slo_ref[...], shi_ref[...]
    half = ROT_DIM // 2

    xs = ([aq_ref[:, h * HEAD_DIM:(h + 1) * HEAD_DIM].astype(jnp.float32) for h in range(ATT_HEADS)]
          + [ak_ref[:, h * HEAD_DIM:(h + 1) * HEAD_DIM].astype(jnp.float32) for h in range(ATT_KV_HEADS)])
    gains = [gq_ref[...]] * ATT_HEADS + [gk_ref[...]] * ATT_KV_HEADS
    ms = [jnp.mean(x * x, axis=-1, keepdims=True) for x in xs]
    xn = [x * lax.rsqrt(m + EPS) * g for x, m, g in zip(xs, ms, gains)]
    lo = [pltpu.roll(x, LANES - half, 1) for x in xn]
    hi = [pltpu.roll(x, half, 1) for x in xn]
    ro = [x * cosf + a * slo + b * shi for x, a, b in zip(xn, lo, hi)]
    for h in range(ATT_HEADS):
        q_out[h * Q_BLOCK:(h + 1) * Q_BLOCK, :] = (ro[h] * (HEAD_DIM ** -0.5 * math.log2(math.e))).astype(q_out.dtype)
    for h in range(ATT_KV_HEADS):
        k_out[:, h * HEAD_DIM:(h + 1) * HEAD_DIM] = ro[ATT_HEADS + h].astype(k_out.dtype)
    r = lax.broadcasted_iota(jnp.int32, (IDX_DIM, IDX_DIM), 0)
    c = lax.broadcasted_iota(jnp.int32, (IDX_DIM, IDX_DIM), 1)
    swap = jnp.where(((c < half) & (r == c + half)) | ((c >= half) & (c < 2 * half) & (r == c - half)),
                     1.0, 0.0).astype(jnp.bfloat16)
    sin_signed = slo + shi
    xs = jnp.concatenate([iq_ref[:, h * IDX_DIM:(h + 1) * IDX_DIM] for h in range(IDX_HEADS)]
                         + [ik_ref[...]], axis=0)
    sw = jnp.dot(xs, swap, preferred_element_type=jnp.float32)
    nh = IDX_HEADS + 1
    rot = (xs.astype(jnp.float32).reshape(nh, Q_BLOCK, IDX_DIM) * cosf[None]
           + sw.reshape(nh, Q_BLOCK, IDX_DIM) * sin_signed[None]).reshape(nh * Q_BLOCK, IDX_DIM)
    iq_out[...] = rot[:IDX_HEADS * Q_BLOCK].astype(iq_out.dtype)
    ik_out[...] = rot[IDX_HEADS * Q_BLOCK:].astype(ik_out.dtype)
    w_out[...] = sm_ref[...] * (IDX_HEADS ** -0.5 * IDX_DIM ** -0.5)


def _rope_tables(seq):
    half = ROT_DIM // 2
    inv_freq = ROPE_THETA ** (-jnp.arange(half, dtype=jnp.float32) * 2.0 / ROT_DIM)
    ang = jnp.arange(seq, dtype=jnp.float32)[:, None] * inv_freq[None, :]
    cos, sin = jnp.cos(ang), jnp.sin(ang)
    z = jnp.zeros((seq, LANES - 2 * half), jnp.float32)
    zh = jnp.zeros((seq, half), jnp.float32)
    cosf = jnp.concatenate([cos, cos, jnp.ones_like(z)], axis=1)
    sin_lo = jnp.concatenate([-sin, zh, z], axis=1)
    sin_hi = jnp.concatenate([zh, sin, z], axis=1)
    return cosf, sin_lo, sin_hi


def _dsa_prep(proj, small, att_q_norm, att_k_norm, batch, seq):
    assert ROT_DIM == IDX_ROT
    t = batch * seq
    nq = seq // Q_BLOCK
    qb = Q_BLOCK
    cosf, slo, shi = _rope_tables(seq)
    tab = pl.BlockSpec((qb, LANES), lambda b, i: (i, 0))
    row = lambda w, off: pl.BlockSpec((qb, w), lambda b, i: (b * nq + i, off // w))
    vec = pl.BlockSpec((1, HEAD_DIM), lambda b, i: (0, 0))
    aq_w, ak_w = ATT_HEADS * HEAD_DIM, ATT_KV_HEADS * HEAD_DIM
    return pl.pallas_call(
        _dsa_prep_kernel, grid=(batch, nq),
        in_specs=[row(aq_w, OFF_AQ), row(ak_w, OFF_AK), row(IQ_W, OFF_IQ), row(IDX_DIM, OFF_IK),
                  pl.BlockSpec((qb, small.shape[1]), lambda b, i: (b * nq + i, 0)),
                  tab, tab, tab, vec, vec],
        out_specs=[pl.BlockSpec((ATT_HEADS * qb, HEAD_DIM), lambda b, i: (b * nq + i, 0)),
                   pl.BlockSpec((qb, ak_w), lambda b, i: (b * nq + i, 0)),
                   pl.BlockSpec((IDX_HEADS * qb, IDX_DIM), lambda b, i: (b * nq + i, 0)),
                   pl.BlockSpec((qb, IDX_DIM), lambda b, i: (b * nq + i, 0)),
                   pl.BlockSpec((qb, small.shape[1]), lambda b, i: (b * nq + i, 0))],
        out_shape=[jax.ShapeDtypeStruct((t * ATT_HEADS, HEAD_DIM), jnp.bfloat16),
                   jax.ShapeDtypeStruct((t, ak_w), jnp.bfloat16),
                   jax.ShapeDtypeStruct((t * IDX_HEADS, IDX_DIM), jnp.bfloat16),
                   jax.ShapeDtypeStruct((t, IDX_DIM), jnp.bfloat16),
                   jax.ShapeDtypeStruct((t, small.shape[1]), jnp.float32)],
        compiler_params=_params(("parallel", "parallel")), name="dsa_prep")(
            proj, proj, proj, proj, small, cosf, slo, shi,
            att_q_norm.reshape(1, HEAD_DIM).astype(jnp.float32),
            att_k_norm.reshape(1, HEAD_DIM).astype(jnp.float32))


DSA_KB = 512
DSA_SB = 512
DSA_HEAD_GROUPS = 8
INT_MIN = -2 ** 31


def _dsa_kernel(iq_ref, q_ref, w_ref, ik_ref, k_ref, v_ref, o_ref,
                key_ref, keyt_ref, wb_ref, m_ref, l_ref, acc_ref, *, topk):
    qi = pl.program_id(1)
    QB, KB, SB = Q_BLOCK, DSA_KB, DSA_SB
    n_kb = (qi * QB + QB + KB - 1) // KB
    qpos = qi * QB + lax.broadcasted_iota(jnp.int32, (QB, SB), 0)
    kiota = lax.broadcasted_iota(jnp.int32, (QB, SB), 1)

    w = w_ref[...]
    for h in range(IDX_HEADS):
        wb_ref[h] = jnp.broadcast_to(w[:, SM_IW + h:SM_IW + h + 1], (QB, LANES))

    def score_block(sb, carry):
        k0 = pl.multiple_of(sb * SB, SB)
        ikb = ik_ref[pl.ds(k0, SB), :]
        hg = IDX_HEADS // DSA_HEAD_GROUPS
        lgs = [lax.dot_general(iq_ref[g * hg * QB:(g + 1) * hg * QB, :], ikb, (((1,), (1,)), ((), ())),
                               preferred_element_type=jnp.float32) for g in range(DSA_HEAD_GROUPS)]
        sc = None
        for g, lg in enumerate(lgs):
            for hh in range(hg):
                wb = wb_ref[g * hg + hh]
                term = jnp.maximum(lg[hh * QB:(hh + 1) * QB], 0.0) * jnp.concatenate([wb] * (SB // LANES), axis=1)
                sc = term if sc is None else sc + term
        bits = pltpu.bitcast(sc + 0.0, jnp.int32)
        keys = bits ^ ((bits >> 31) & 0x7FFFFFFF)
        keys = jnp.where(k0 + kiota <= qpos, keys, INT_MIN)
        key_ref[:, pl.ds(k0, SB)] = keys
        keyt_ref[pl.ds(k0, SB), :] = keys.T
        return carry

    n_sb = (qi * QB + QB + SB - 1) // SB
    lax.fori_loop(0, n_sb, score_block, 0)

    @pl.when(n_sb < n_kb * (KB // SB))
    def _():
        k0 = pl.multiple_of(n_sb * SB, SB)
        key_ref[:, pl.ds(k0, SB)] = jnp.full((QB, SB), INT_MIN, jnp.int32)
        keyt_ref[pl.ds(k0, SB), :] = jnp.full((SB, QB), INT_MIN, jnp.int32)

    def count_ge(cand):
        def blk(kb, cnt):
            k0 = pl.multiple_of(kb * KB, KB)
            parts = [jnp.where(keyt_ref[pl.ds(k0 + 8 * j, 8), :] >= cand, 1.0, 0.0) for j in range(KB // 8)]
            while len(parts) > 1:
                parts = [parts[j] + parts[j + 1] for j in range(0, len(parts), 2)]
            return cnt + parts[0]
        cnt = lax.fori_loop(0, n_kb, blk, jnp.zeros((8, LANES), jnp.float32))
        return jnp.sum(cnt, axis=0, keepdims=True)

    def to_cols(x):
        return jnp.concatenate([x] * (QB // 8), axis=0).T

    tile = lambda x: jnp.broadcast_to(x, (8, LANES))
    cnt0 = tile(count_ge(jnp.zeros((8, LANES), jnp.int32)))
    thr0 = jnp.where(cnt0 >= topk, 0, jnp.full((8, LANES), INT_MIN, jnp.int32))

    def bit_step(i, carry):
        thr, cnt_thr = carry
        cand = thr | (jnp.int32(1) << (30 - i))
        cnt = tile(count_ge(cand))
        return jnp.where(cnt >= topk, cand, thr), jnp.where(cnt >= topk, cnt, cnt_thr)

    thr, cnt_thr = lax.fori_loop(0, 31, bit_step, (thr0, cnt0))

    surplus = jnp.where((thr > INT_MIN) & (cnt_thr > topk), 1.0, 0.0)

    @pl.when(jnp.max(surplus) > 0.0)
    def _():
        need = topk - tile(count_ge(thr + 1))
        sub8 = lax.broadcasted_iota(jnp.int32, (8, LANES), 0)

        def count_tied_before(idx):
            def blk(kb, cnt):
                k0 = pl.multiple_of(kb * KB, KB)
                for j in range(KB // 8):
                    kk = keyt_ref[pl.ds(k0 + 8 * j, 8), :]
                    hit = jnp.where(kk == thr, jnp.where(k0 + 8 * j + sub8 < idx, 1.0, 0.0), 0.0)
                    cnt = cnt + hit
                return cnt
            cnt = lax.fori_loop(0, n_kb, blk, jnp.zeros((8, LANES), jnp.float32))
            return tile(jnp.sum(cnt, axis=0, keepdims=True))

        n_bits = (key_ref.shape[1] - 1).bit_length()

        def idx_step(i, last):
            cand = last | (jnp.int32(1) << (n_bits - 1 - i))
            return jnp.where(count_tied_before(cand) < need, cand, last)

        last = lax.fori_loop(0, n_bits, idx_step, jnp.zeros((8, LANES), jnp.int32))
        thr_c = jnp.concatenate([to_cols(thr)] * (KB // LANES), axis=1)
        last_c = jnp.concatenate([to_cols(last)] * (KB // LANES), axis=1)
        flag_c = jnp.concatenate([to_cols(surplus)] * (KB // LANES), axis=1)
        pos = lax.broadcasted_iota(jnp.int32, (QB, KB), 1)

        def demote(kb, carry):
            k0 = pl.multiple_of(kb * KB, KB)
            kk = key_ref[:, pl.ds(k0, KB)]
            drop = jnp.where(kk == thr_c, jnp.where(k0 + pos > last_c, flag_c, 0.0), 0.0)
            key_ref[:, pl.ds(k0, KB)] = jnp.where(drop > 0.0, thr_c - 1, kk)
            return carry

        lax.fori_loop(0, n_kb, demote, 0)

    thr = to_cols(jnp.maximum(thr, INT_MIN + 1))

    G = ATT_GROUP
    NG = KB // LANES
    m_ref[...] = jnp.full(m_ref.shape, NEG_BIG, jnp.float32)
    l_ref[...] = jnp.zeros(l_ref.shape, jnp.float32)
    acc_ref[...] = jnp.zeros(acc_ref.shape, jnp.float32)

    def lane_fold(x, op):
        parts = [x[:, g * LANES:(g + 1) * LANES] for g in range(NG)]
        while len(parts) > 1:
            parts = [op(parts[i], parts[i + 1]) for i in range(0, len(parts), 2)]
        return parts[0]

    def attn_block(kb, carry):
        k0 = pl.multiple_of(kb * KB, KB)
        bias = jnp.where(key_ref[:, pl.ds(k0, KB)] >= jnp.concatenate([thr] * NG, axis=1), 0.0, NEG_BIG)
        bias_g = jnp.concatenate([bias] * G, axis=0)
        grp = range(ATT_KV_HEADS)
        rows = [slice(hk * G * QB, (hk + 1) * G * QB) for hk in grp]
        cols = [slice(hk * HEAD_DIM, (hk + 1) * HEAD_DIM) for hk in grp]
        s = [lax.dot_general(q_ref[rows[i], :], k_ref[pl.ds(k0, KB), cols[i]], (((1,), (1,)), ((), ())),
                             preferred_element_type=jnp.float32) for i in grp]
        p, alpha, m_new = [], [], []
        for i in grp:
            si = s[i] + bias_g
            m_old = m_ref[rows[i], :]
            mi = jnp.maximum(m_old, jnp.max(lane_fold(si, jnp.maximum), axis=1, keepdims=True))
            p.append(jnp.exp2(si - jnp.concatenate([mi] * NG, axis=1)).astype(jnp.bfloat16))
            alpha.append(jnp.exp2(m_old - mi))
            m_new.append(mi)
        ones = jnp.ones((KB, LANES), jnp.bfloat16)
        pv = [jnp.dot(p[i], jnp.concatenate([v_ref[pl.ds(k0, KB), cols[i]], ones], axis=1),
                      preferred_element_type=jnp.float32) for i in grp]
        for i in grp:
            l_ref[rows[i], :] = alpha[i] * l_ref[rows[i], :] + pv[i][:, HEAD_DIM:]
            acc_ref[rows[i], :] = alpha[i] * acc_ref[rows[i], :] + pv[i][:, :HEAD_DIM]
            m_ref[rows[i], :] = m_new[i]
        return carry

    lax.fori_loop(0, n_kb, attn_block, 0)
    for h in range(ATT_HEADS):
        rows = slice(h * QB, (h + 1) * QB)
        o_ref[:, h * HEAD_DIM:(h + 1) * HEAD_DIM] = (acc_ref[rows, :] / l_ref[rows, :]).astype(o_ref.dtype)


def _dsa(iq_hm, q_hm, w_s, ik_rot, k_att, proj, batch, seq, topk):
    nq = seq // Q_BLOCK
    qb = Q_BLOCK
    kvw = ATT_KV_HEADS * HEAD_DIM
    return pl.pallas_call(
        functools.partial(_dsa_kernel, topk=topk), grid=(batch, nq),
        in_specs=[pl.BlockSpec((IDX_HEADS * qb, IDX_DIM), lambda b, i: (b * nq + i, 0)),
                  pl.BlockSpec((ATT_HEADS * qb, HEAD_DIM), lambda b, i: (b * nq + i, 0)),
                  pl.BlockSpec((qb, w_s.shape[1]), lambda b, i: (b * nq + i, 0)),
                  pl.BlockSpec((seq, IDX_DIM), lambda b, i: (b, 0)),
                  pl.BlockSpec((seq, kvw), lambda b, i: (b, 0)),
                  pl.BlockSpec((seq, kvw), lambda b, i: (b, OFF_AV // kvw))],
        out_specs=pl.BlockSpec((qb, ATT_HEADS * HEAD_DIM), lambda b, i: (b * nq + i, 0)),
        out_shape=jax.ShapeDtypeStruct((batch * seq, ATT_HEADS * HEAD_DIM), jnp.bfloat16),
        scratch_shapes=[pltpu.VMEM((qb, seq), jnp.int32),
                        pltpu.VMEM((seq, qb), jnp.int32),
                        pltpu.VMEM((IDX_HEADS, qb, LANES), jnp.float32),
                        pltpu.VMEM((ATT_HEADS * qb, LANES), jnp.float32),
                        pltpu.VMEM((ATT_HEADS * qb, LANES), jnp.float32),
                        pltpu.VMEM((ATT_HEADS * qb, HEAD_DIM), jnp.float32)],
        compiler_params=_params(("parallel", "arbitrary")), name="dsa")(
            iq_hm, q_hm, w_s, ik_rot, k_att, proj)


def _mem_attn_kernel(q_ref, kv_ref, gq_ref, gk_ref, o_ref):
    def norm(x, gain):
        return x * lax.rsqrt(jnp.mean(x * x, axis=-1, keepdims=True) + EPS) * gain

    for h in range(MEM_HEADS):
        cols = slice(h * MEM_HEAD_DIM, (h + 1) * MEM_HEAD_DIM)
        q = norm(q_ref[:, cols], gq_ref[...]) * (MEM_HEAD_DIM ** -0.5)
        k = norm(kv_ref[:, cols], gk_ref[...])
        v = kv_ref[:, MEM_WIDTH + h * MEM_HEAD_DIM:MEM_WIDTH + (h + 1) * MEM_HEAD_DIM]
        s = _bdot_nt(q, k)
        p = jnp.exp(s - jnp.max(s, axis=1, keepdims=True))
        o = _bdot(p, v) / jnp.sum(p, axis=1, keepdims=True)
        o_ref[:, cols] = o.astype(o_ref.dtype)


def _mem_attn(mq, mkv, gq, gk, batch, seq, n_mem):
    tq = _pick(seq, 512)
    nq = seq // tq
    vec = pl.BlockSpec((1, MEM_HEAD_DIM), lambda b, i: (0, 0))
    return pl.pallas_call(
        _mem_attn_kernel, grid=(batch, nq),
        in_specs=[pl.BlockSpec((tq, MEM_WIDTH), lambda b, i: (b * nq + i, 0)),
                  pl.BlockSpec((n_mem, 2 * MEM_WIDTH), lambda b, i: (b, 0)), vec, vec],
        out_specs=pl.BlockSpec((tq, MEM_WIDTH), lambda b, i: (b * nq + i, 0)),
        out_shape=jax.ShapeDtypeStruct((batch * seq, MEM_WIDTH), jnp.bfloat16),
        compiler_params=_params(("parallel", "parallel")), name="mem_attn")(
            mq, mkv, gq.reshape(1, MEM_HEAD_DIM).astype(jnp.float32),
            gk.reshape(1, MEM_HEAD_DIM).astype(jnp.float32))


def _split_w_in(w):
    sizes = (GDN_QK, GDN_QK, GDN_V, GDN_V, GDN_HEADS, GDN_HEADS, ATT_HEADS * HEAD_DIM,
             ATT_KV_HEADS * HEAD_DIM, ATT_KV_HEADS * HEAD_DIM, IQ_W, IDX_DIM, IDX_HEADS)
    offs = np.concatenate([[0], np.cumsum(sizes)])
    seg = lambda i: w[:, int(offs[i]):int(offs[i + 1])]
    gq, gk, gv, gz, ga, gb, aq, ak, av, iq, ik, iw = (seg(i) for i in range(12))
    d = w.shape[0]
    bf = jnp.bfloat16
    main = jnp.concatenate([s.astype(bf) for s in (iq, gq, gk, gv, gz, aq, ak, av, ik)]
                           + [jnp.zeros((d, N_MAIN_PAD - N_MAIN), bf)], axis=1)
    nsm = 2 * GDN_HEADS + IDX_HEADS
    small = jnp.concatenate([ga, gb, iw, jnp.zeros((d, LANES - nsm), w.dtype)], axis=1)
    small_hi = small.astype(bf)
    small_lo = (small - small_hi.astype(jnp.float32)).astype(bf)
    return main, small_hi, small_lo


def kernel(x, mem, norm_mix, w_in, conv_w, a_log, dt_bias, gdn_norm, att_q_norm, att_k_norm, w_out,
           norm_mem_q, norm_mem_kv, w_mem_q, w_mem_kv, mem_q_norm, mem_k_norm, w_mem_o, norm_ffn,
           w_gate, w_up, w_down):
    B, S, D = x.shape
    M = mem.shape[1]
    T = B * S
    topk = min(TOPK_MAX, S // 4)
    bf = jnp.bfloat16
    x2 = x.reshape(T, D)
    for l in range(w_in.shape[0]):
        w_main, *w_small = _split_w_in(w_in[l])
        h, small = _rmsnorm(x2, norm_mix[l], w_small)
        proj = _matmul([(h, w_main)], out_dtype=bf, tn=IN_TN, name="in_proj")
        qkv = _gdn_prep(proj, conv_w[l], S)
        gates, decay = _gate_prep(small, a_log[l], dt_bias[l])
        o_a = _gdn(qkv, proj, gates, decay, gdn_norm[l], B, S)
        q_hm, k_att, iq_hm, ik_rot, w_s = _dsa_prep(proj, small, att_q_norm[l], att_k_norm[l], B, S)
        o_b = _dsa(iq_hm, q_hm, w_s, ik_rot, k_att, proj, B, S, topk)
        wo = w_out[l].astype(bf)
        x2 = _matmul([(o_a, wo[:GDN_V]), (o_b, wo[GDN_V:])], res=x2, name="out_proj")
        hq = _rmsnorm(x2, norm_mem_q[l])
        hm = _rmsnorm(mem.reshape(B * M, D), norm_mem_kv[l])
        mq = _matmul([(hq, w_mem_q[l].astype(bf))], name="mem_q")
        mkv = _matmul([(hm, w_mem_kv[l].astype(bf))], name="mem_kv")
        mo = _mem_attn(mq, mkv, mem_q_norm[l], mem_k_norm[l], B, S, M)
        x2 = _matmul([(mo, w_mem_o[l].astype(bf))], res=x2, tn=1024, name="mem_o")
        hf = _rmsnorm(x2, norm_ffn[l])
        mid = _gateup(hf, w_gate[l], w_up[l])
        x2 = _matmul([(mid, w_down[l].astype(bf))], res=x2, tm=512, tn=512, name="ffn_down")
    return x2.reshape(B, S, D)
```

```python
import functools
import math

import jax
import jax.numpy as jnp
import numpy as np
from jax import lax
from jax.experimental import pallas as pl
from jax.experimental.pallas import tpu as pltpu

D_MODEL = 4096
ATT_KV_HEADS = 4
IDX_HEADS = 32
TOPK_MAX = 256

EPS = 1e-6
ROPE_THETA = 500000.0
HEAD_DIM = 128
ROT_DIM = HEAD_DIM // 4
GDN_DK = 128
GDN_DV = 128
GDN_HEADS = (D_MODEL // 2) // GDN_DV
GDN_CONV = 4
GDN_CHUNK = 64
GDN_QK = GDN_HEADS * GDN_DK
GDN_V = GDN_HEADS * GDN_DV
ATT_HEADS = (D_MODEL // 2) // HEAD_DIM
ATT_GROUP = ATT_HEADS // ATT_KV_HEADS
IDX_DIM = 128
IDX_ROT = IDX_DIM // 4
Q_BLOCK = 128
MEM_HEADS = 4
MEM_HEAD_DIM = 128
MEM_WIDTH = MEM_HEADS * MEM_HEAD_DIM

LANES = 128
VMEM_LIMIT = 56 * 1024 * 1024
NEG_BIG = -1e30

IQ_W = IDX_HEADS * IDX_DIM
OFF_IQ = 0
OFF_GQ = OFF_IQ + IQ_W
OFF_GK = OFF_GQ + GDN_QK
OFF_GV = OFF_GK + GDN_QK
OFF_GZ = OFF_GV + GDN_V
OFF_AQ = OFF_GZ + GDN_V
OFF_AK = OFF_AQ + ATT_HEADS * HEAD_DIM
OFF_AV = OFF_AK + ATT_KV_HEADS * HEAD_DIM
OFF_IK = OFF_AV + ATT_KV_HEADS * HEAD_DIM
N_MAIN = OFF_IK + IDX_DIM
IN_TN = 512
N_MAIN_PAD = -(-N_MAIN // IN_TN) * IN_TN
SM_GA = 0
SM_GB = GDN_HEADS
SM_IW = 2 * GDN_HEADS
HI = lax.Precision.HIGHEST


def _params(sem):
    return pltpu.CompilerParams(dimension_semantics=sem, vmem_limit_bytes=VMEM_LIMIT)


def _pick(n, pref):
    t = min(n, pref)
    while n % t:
        t //= 2
    return t


def _sigmoid(x):
    return 1.0 / (1.0 + jnp.exp2(x * (-math.log2(math.e))))


def _rmsnorm_kernel(x_ref, g_ref, o_ref):
    x = x_ref[...]
    ms = jnp.mean(x * x, axis=-1, keepdims=True)
    o_ref[...] = (x * lax.rsqrt(ms + EPS) * g_ref[...]).astype(o_ref.dtype)


def _rmsnorm_side_kernel(x_ref, g_ref, whi_ref, wlo_ref, o_ref, s_ref):
    x = x_ref[...]
    ms = jnp.mean(x * x, axis=-1, keepdims=True)
    xn = x * lax.rsqrt(ms + EPS) * g_ref[...]
    hi = xn.astype(o_ref.dtype)
    o_ref[...] = hi
    lo = (xn - hi.astype(jnp.float32)).astype(hi.dtype)
    dot_t = lambda a, wt: lax.dot_general(a, wt, (((1,), (1,)), ((), ())), preferred_element_type=jnp.float32)
    s_ref[...] = dot_t(hi, whi_ref[...]) + dot_t(lo, whi_ref[...]) + dot_t(hi, wlo_ref[...])


def _rmsnorm(x2d, g, w_side=None, tm=256):
    m, d = x2d.shape
    tm = _pick(m, tm)
    g2 = g.reshape(1, d).astype(jnp.float32)
    xs = pl.BlockSpec((tm, d), lambda i: (i, 0))
    gs = pl.BlockSpec((1, d), lambda i: (0, 0))
    if w_side is None:
        return pl.pallas_call(
            _rmsnorm_kernel, grid=(m // tm,), in_specs=[xs, gs], out_specs=xs,
            out_shape=jax.ShapeDtypeStruct((m, d), jnp.bfloat16),
            compiler_params=_params(("parallel",)), name="rmsnorm")(x2d, g2)
    w_hi, w_lo = w_side
    ns = w_hi.shape[0]
    ws = pl.BlockSpec((ns, d), lambda i: (0, 0))
    return pl.pallas_call(
        _rmsnorm_side_kernel, grid=(m // tm,),
        in_specs=[xs, gs, ws, ws],
        out_specs=[xs, pl.BlockSpec((tm, ns), lambda i: (i, 0))],
        out_shape=[jax.ShapeDtypeStruct((m, d), jnp.bfloat16),
                   jax.ShapeDtypeStruct((m, ns), jnp.float32)],
        compiler_params=_params(("parallel",)), name="rmsnorm_side")(x2d, g2, w_hi, w_lo)


def _matmul_kernel(*refs, n_pairs, has_res, w_t):
    o_ref = refs[-1]
    rhs_dim = 1 if w_t else 0
    acc = None
    for p in range(n_pairs):
        d = lax.dot_general(refs[2 * p][...], refs[2 * p + 1][...], (((1,), (rhs_dim,)), ((), ())),
                            preferred_element_type=jnp.float32)
        acc = d if acc is None else acc + d
    if has_res:
        acc = acc + refs[2 * n_pairs][...]
    o_ref[...] = acc.astype(o_ref.dtype)


def _matmul(pairs, res=None, out_dtype=jnp.float32, tm=1024, tn=512, w_t=False, name="matmul"):
    m = pairs[0][0].shape[0]
    n = pairs[0][1].shape[0 if w_t else 1]
    tm, tn = _pick(m, tm), _pick(n, tn)
    args, specs = [], []
    for a, w in pairs:
        k = a.shape[1]
        args += [a, w]
        wspec = pl.BlockSpec((tn, k), lambda i, j: (j, 0)) if w_t else pl.BlockSpec((k, tn), lambda i, j: (0, j))
        specs += [pl.BlockSpec((tm, k), lambda i, j: (i, 0)), wspec]
    if res is not None:
        args.append(res)
        specs.append(pl.BlockSpec((tm, tn), lambda i, j: (i, j)))
    return pl.pallas_call(
        functools.partial(_matmul_kernel, n_pairs=len(pairs), has_res=res is not None, w_t=w_t),
        grid=(m // tm, n // tn), in_specs=specs,
        out_specs=pl.BlockSpec((tm, tn), lambda i, j: (i, j)),
        out_shape=jax.ShapeDtypeStruct((m, n), out_dtype),
        compiler_params=_params(("parallel", "arbitrary")), name=name)(*args)


def _gateup_kernel(a_ref, wg_ref, wu_ref, o_ref):
    a = a_ref[...]
    g = jnp.dot(a, wg_ref[...].astype(a.dtype), preferred_element_type=jnp.float32)
    u = jnp.dot(a, wu_ref[...].astype(a.dtype), preferred_element_type=jnp.float32)
    o_ref[...] = (g * _sigmoid(g) * u).astype(o_ref.dtype)


def _gateup(a, wg, wu, tm=1024, tn=256):
    m, k = a.shape
    n = wg.shape[1]
    tm, tn = _pick(m, tm), _pick(n, tn)
    ws = pl.BlockSpec((k, tn), lambda i, j: (0, j))
    return pl.pallas_call(
        _gateup_kernel, grid=(m // tm, n // tn),
        in_specs=[pl.BlockSpec((tm, k), lambda i, j: (i, 0)), ws, ws],
        out_specs=pl.BlockSpec((tm, tn), lambda i, j: (i, j)),
        out_shape=jax.ShapeDtypeStruct((m, n), jnp.bfloat16),
        compiler_params=_params(("parallel", "arbitrary")), name="ffn_gateup")(a, wg, wu)


HALO = 16


def _gdn_prep_kernel(x_ref, halo_ref, w_ref, o_ref, *, ts, tc, blocks_per_seq, nqk_blocks):
    i = pl.program_id(0)
    j = pl.program_id(1)
    first = (i % blocks_per_seq) == 0
    x = x_ref[...].astype(jnp.float32)
    halo8 = jnp.where(first, 0.0, halo_ref[HALO - 8:HALO, :].astype(jnp.float32))
    w = w_ref[...]
    row8 = lax.broadcasted_iota(jnp.int32, (8, tc), 0)
    y = x * w[GDN_CONV - 1:GDN_CONV, :]
    for d in range(1, GDN_CONV):
        xs = pltpu.roll(x, d, 0)
        head = jnp.where(row8 < d, pltpu.roll(halo8, d, 0), xs[0:8])
        xs = jnp.concatenate([head, xs[8:]], axis=0)
        y = y + xs * w[GDN_CONV - 1 - d:GDN_CONV - d, :]
    y = y * _sigmoid(y)
    is_q = j < nqk_blocks
    is_v = j >= 2 * nqk_blocks
    for h in range(tc // GDN_DK):
        yh = y[:, h * GDN_DK:(h + 1) * GDN_DK]
        rs = lax.rsqrt(jnp.sum(yh * yh, axis=-1, keepdims=True) + EPS)
        rs = rs * jnp.where(is_q, GDN_DK ** -0.5, 1.0)
        fac = jnp.where(is_v, 1.0, rs)
        o_ref[:, h * GDN_DK:(h + 1) * GDN_DK] = (yh * fac).astype(o_ref.dtype)


def _gdn_prep(proj, conv_w, seq):
    t = proj.shape[0]
    ts = _pick(seq, 512)
    tc = 512
    nconv = (2 * GDN_QK + GDN_V) // tc
    col0 = OFF_GQ // tc
    rb = ts // HALO
    kern = functools.partial(_gdn_prep_kernel, ts=ts, tc=tc, blocks_per_seq=seq // ts,
                             nqk_blocks=GDN_QK // tc)
    return pl.pallas_call(
        kern, grid=(t // ts, nconv),
        in_specs=[pl.BlockSpec((ts, tc), lambda i, j: (i, col0 + j)),
                  pl.BlockSpec((HALO, tc), lambda i, j: (jnp.maximum(i * rb - 1, 0), col0 + j)),
                  pl.BlockSpec((GDN_CONV, tc), lambda i, j: (0, j))],
        out_specs=pl.BlockSpec((ts, tc), lambda i, j: (i, j)),
        out_shape=jax.ShapeDtypeStruct((t, nconv * tc), jnp.bfloat16),
        compiler_params=_params(("parallel", "parallel")), name="gdn_prep")(proj, proj, conv_w)


def _gate_prep_kernel(s_ref, alog_ref, dtb_ref, o_ref, dec_ref, *, ts):
    C = GDN_CHUNK
    s = s_ref[...]
    lane = lax.broadcasted_iota(jnp.int32, s.shape, 1)
    xa = s + dtb_ref[...]
    softplus = jnp.maximum(xa, 0.0) + jnp.log(1.0 + jnp.exp(-jnp.abs(xa)))
    g = -jnp.exp(alog_ref[...]) * softplus
    g = jnp.where(lane < GDN_HEADS, g, 0.0)
    r = lax.broadcasted_iota(jnp.int32, (ts, ts), 0)
    c = lax.broadcasted_iota(jnp.int32, (ts, ts), 1)
    tri = jnp.where((r // C == c // C) & (c <= r), 1.0, 0.0)
    gcum = jnp.dot(tri, g, precision=HI, preferred_element_type=jnp.float32)
    beta = _sigmoid(s)
    is_beta = (lane >= SM_GB) & (lane < SM_GB + GDN_HEADS)
    o_ref[...] = jnp.where(lane < GDN_HEADS, gcum, jnp.where(is_beta, beta, 0.0))
    gcum_t = gcum.T
    incl = lax.broadcasted_iota(jnp.int32, (C, C), 1) <= lax.broadcasted_iota(jnp.int32, (C, C), 0)
    sub =lax.broadcasted_iota(jnp.int32, (8, C), 0)
    hc = [(h, ci) for h in range(GDN_HEADS) for ci in range(ts // C)]
    cols = [jnp.broadcast_to(gcum[ci * C:(ci + 1) * C, h:h + 1], (C, C)) for h, ci in hc]
    tiles = [gcum_t[8 * (h // 8):8 * (h // 8) + 8, ci * C:(ci + 1) * C] for h, ci in hc]
    rows = [jnp.sum(jnp.where(sub == h % 8, t, 0.0), axis=0, keepdims=True) for (h, ci), t in zip(hc, tiles)]
    for (h, ci), col, row in zip(hc, cols, rows):
        dec_ref[ci * C:(ci + 1) * C, h * C:(h + 1) * C] = jnp.exp(jnp.where(incl, col - row, NEG_BIG))


def _gate_prep(small, a_log, dt_bias):
    t, ns = small.shape
    ts = _pick(t, 256)
    pad = lambda v: jnp.zeros((1, ns), jnp.float32).at[0, :GDN_HEADS].set(v.astype(jnp.float32))
    vs = pl.BlockSpec((1, ns), lambda i: (0, 0))
    bs = pl.BlockSpec((ts, ns), lambda i: (i, 0))
    dw = GDN_HEADS * GDN_CHUNK
    return pl.pallas_call(
        functools.partial(_gate_prep_kernel, ts=ts), grid=(t // ts,),
        in_specs=[bs, vs, vs], out_specs=[bs, pl.BlockSpec((ts, dw), lambda i: (i, 0))],
        out_shape=[jax.ShapeDtypeStruct((t, ns), jnp.float32), jax.ShapeDtypeStruct((t, dw), jnp.float32)],
        compiler_params=_params(("parallel",)), name="gate_prep")(small, pad(a_log), pad(dt_bias))


GDN_WY_HEADS = 4
GDN_WY_ROWS = 512
GDN_WY_CG = 4
GDN_SCAN_HEADS = 8
GDN_SCAN_ROWS = 1024


def _bdot(a, b):
    return jnp.dot(a.astype(jnp.bfloat16), b.astype(jnp.bfloat16), preferred_element_type=jnp.float32)


def _bdot_nt(a, b):
    return lax.dot_general(a.astype(jnp.bfloat16), b.astype(jnp.bfloat16), (((1,), (1,)), ((), ())),
                           preferred_element_type=jnp.float32)


def _head_col(gsl, idx):
    lane = lax.broadcasted_iota(jnp.int32, (1, LANES), 1)
    return jnp.sum(gsl * jnp.where(lane == idx, 1.0, 0.0), axis=1, keepdims=True)


def _gdn_wy_kernel(q_ref, k_ref, v_ref, g_ref, dec_ref, u_ref, w_ref, qd_ref, kd_ref, qk_ref, *, hb):
    hg = pl.program_id(1)
    C, CG = GDN_CHUNK, GDN_WY_CG
    r = lax.broadcasted_iota(jnp.int32, (C, C), 0)
    c = lax.broadcasted_iota(jnp.int32, (C, C), 1)
    off_diag = c != r

    def body(it, carry):
        where, q, k, v, gc, beta, decay = [], [], [], [], [], [], []
        for cc in range(CG):
            r0 = pl.multiple_of((it * CG + cc) * C, C)
            gsl = g_ref[pl.ds(r0, C), :]
            for hh in range(hb):
                sl = slice(hh * GDN_DK, (hh + 1) * GDN_DK)
                h = hg * hb + hh
                where.append((r0, hh))
                q.append(q_ref[pl.ds(r0, C), sl].astype(jnp.float32))
                k.append(k_ref[pl.ds(r0, C), sl].astype(jnp.float32))
                v.append(v_ref[pl.ds(r0, C), sl].astype(jnp.float32))
                gc.append(_head_col(gsl, h))
                beta.append(_head_col(gsl, SM_GB + h))
                decay.append(dec_ref[pl.ds(r0, C), hh * C:(hh + 1) * C])
        n = len(where)
        ch = range(n)
        kb = [k[i] * beta[i] for i in ch]
        eg = [jnp.exp(gc[i]) for i in ch]
        kk = [_bdot_nt(kb[i], k[i]) for i in ch]
        qk = [_bdot_nt(q[i], k[i]) for i in ch]
        pw = [jnp.where(off_diag, kk[i] * decay[i], 0.0) for i in ch]
        x = [jnp.concatenate([v[i] * beta[i], kb[i] * eg[i]], axis=1) for i in ch]
        n_lvl = int(math.log2(C))
        for lvl in range(n_lvl):
            px = [_bdot(pw[i], x[i]) for i in ch]
            if lvl + 1 < n_lvl:
                pw = [_bdot(pw[i], pw[i]) for i in ch]
            x = [x[i] - px[i] if lvl == 0 else x[i] + px[i] for i in ch]
        for i, (r0, hh) in enumerate(where):
            sl = slice(hh * GDN_DK, (hh + 1) * GDN_DK)
            u_ref[pl.ds(r0, C), sl] = x[i][:, :GDN_DV]
            w_ref[pl.ds(r0, C), sl] = x[i][:, GDN_DV:].astype(w_ref.dtype)
            qd_ref[pl.ds(r0, C), sl] = (q[i] * eg[i]).astype(qd_ref.dtype)
            kd_ref[pl.ds(r0, C), sl] = (k[i] * jnp.exp(gc[i][C - 1:C, :] - gc[i])).astype(kd_ref.dtype)
            qk_ref[pl.ds(r0, C), hh * C:(hh + 1) * C] = (qk[i] * decay[i]).astype(qk_ref.dtype)
        return carry

    lax.fori_loop(0, q_ref.shape[0] // (CG * C), body, 0)


def _gdn_scan_kernel(u_ref, w_ref, qd_ref, kd_ref, qk_ref, z_ref, g_ref, gain_ref, o_ref, state_ref, *, hb):
    hg = pl.program_id(1)
    C = GDN_CHUNK
    sub8 = lax.broadcasted_iota(jnp.int32, (8, LANES), 0)
    gain = gain_ref[...]

    @pl.when(pl.program_id(2) == 0)
    def _():
        state_ref[...] = jnp.zeros(state_ref.shape, state_ref.dtype)

    def body(ci, states):
        r0 = pl.multiple_of(ci * C, C)
        g_tail = g_ref[pl.ds(r0 + C - 8, 8), :]
        g_last_row = jnp.sum(jnp.where(sub8 == 7, g_tail, 0.0), axis=0, keepdims=True)
        loaded = []
        for hh in range(hb):
            sl = slice(hh * GDN_DK, (hh + 1) * GDN_DK)
            loaded.append((u_ref[pl.ds(r0, C), sl], w_ref[pl.ds(r0, C), sl], qd_ref[pl.ds(r0, C), sl],
                           kd_ref[pl.ds(r0, C), sl], qk_ref[pl.ds(r0, C), hh * C:(hh + 1) * C],
                           z_ref[pl.ds(r0, C), sl].astype(jnp.float32)))
        hs = range(hb)
        f32 = jnp.float32
        u, w, qd, kd, qk, z = zip(*loaded)
        sb = [states[i].astype(jnp.bfloat16) for i in hs]
        ws = [jnp.dot(w[i], sb[i], preferred_element_type=f32) for i in hs]
        qs = [jnp.dot(qd[i], sb[i], preferred_element_type=f32) for i in hs]
        vb = [(u[i] - ws[i]).astype(jnp.bfloat16) for i in hs]
        kv = [lax.dot_general(kd[i], vb[i], (((0,), (0,)), ((), ())), preferred_element_type=f32) for i in hs]
        qv = [jnp.dot(qk[i], vb[i], preferred_element_type=f32) for i in hs]
        a = [jnp.exp(_head_col(g_last_row, hg * hb + i)) for i in hs]
        new_states = tuple(states[i] * a[i] + kv[i] for i in hs)
        for i in hs:
            o = qs[i] + qv[i]
            on = o * lax.rsqrt(jnp.mean(o * o, axis=-1, keepdims=True) + EPS) * gain
            o_ref[pl.ds(r0, C), i * GDN_DV:(i + 1) * GDN_DV] = (on * (z[i] * _sigmoid(z[i]))).astype(o_ref.dtype)
        return new_states

    init = tuple(state_ref[hh] for hh in range(hb))
    final = lax.fori_loop(0, u_ref.shape[0] // C, body, init)
    for hh in range(hb):
        state_ref[hh] = final[hh]


def _gdn(qkv, proj, gates, decay, gdn_norm, batch, seq):
    t = batch * seq
    bf = jnp.bfloat16
    C = GDN_CHUNK
    hb = min(GDN_WY_HEADS, GDN_HEADS)
    ts = _pick(seq, GDN_WY_ROWS)
    assert GDN_HEADS % hb == 0 and ts % (GDN_WY_CG * C) == 0
    hw = hb * GDN_DK
    kq = GDN_QK // hw
    blk = lambda off: pl.BlockSpec((ts, hw), lambda i, g: (i, off + g))
    oblk = pl.BlockSpec((ts, hw), lambda i, g: (i, g))
    cblk = pl.BlockSpec((ts, hb * C), lambda i, g: (i, g))
    u, w, qd, kd, qk = pl.pallas_call(
        functools.partial(_gdn_wy_kernel, hb=hb), grid=(t // ts, GDN_HEADS // hb),
        in_specs=[blk(0), blk(kq), blk(2 * kq),
                  pl.BlockSpec((ts, gates.shape[1]), lambda i, g: (i, 0)), cblk],
        out_specs=[oblk, oblk, oblk, oblk, cblk],
        out_shape=[jax.ShapeDtypeStruct((t, GDN_V), jnp.float32), jax.ShapeDtypeStruct((t, GDN_QK), bf),
                   jax.ShapeDtypeStruct((t, GDN_QK), bf), jax.ShapeDtypeStruct((t, GDN_QK), bf),
                   jax.ShapeDtypeStruct((t, GDN_HEADS * C), bf)],
        compiler_params=_params(("parallel", "parallel")), name="gdn_wy")(qkv, qkv, qkv, gates, decay)
    hb = min(GDN_SCAN_HEADS, GDN_HEADS)
    ts = _pick(seq, GDN_SCAN_ROWS)
    assert GDN_HEADS % hb == 0
    nsb = seq // ts
    hw = hb * GDN_DK
    zoff = OFF_GZ // hw
    blk = pl.BlockSpec((ts, hw), lambda b, g, s: (b * nsb + s, g))
    return pl.pallas_call(
        functools.partial(_gdn_scan_kernel, hb=hb), grid=(batch, GDN_HEADS // hb, nsb),
        in_specs=[blk, blk, blk, blk,
                  pl.BlockSpec((ts, hb * C), lambda b, g, s: (b * nsb + s, g)),
                  pl.BlockSpec((ts, hw), lambda b, g, s: (b * nsb + s, zoff + g)),
                  pl.BlockSpec((ts, gates.shape[1]), lambda b, g, s: (b * nsb + s, 0)),
                  pl.BlockSpec((1, GDN_DV), lambda b, g, s: (0, 0))],
        out_specs=blk,
        out_shape=jax.ShapeDtypeStruct((t, GDN_V), bf),
        scratch_shapes=[pltpu.VMEM((hb, GDN_DK, GDN_DV), jnp.float32)],
        compiler_params=_params(("parallel", "parallel", "arbitrary")), name="gdn_scan")(
            u, w, qd, kd, qk, proj, gates, gdn_norm.reshape(1, GDN_DV).astype(jnp.float32))


def _dsa_prep_kernel(aq_ref, ak_ref, iq_ref, ik_ref, sm_ref, cos_ref, slo_ref, shi_ref, gq_ref, gk_ref,
                     q_out, k_out, iq_out, ik_out, w_out):
    cosf, slo, shi = cos_ref[...], slo_ref[...], shi_ref[...]
    half = ROT_DIM // 2

    xs = ([aq_ref[:, h * HEAD_DIM:(h + 1) * HEAD_DIM].astype(jnp.float32) for h in range(ATT_HEADS)]
          + [ak_ref[:, h * HEAD_DIM:(h + 1) * HEAD_DIM].astype(jnp.float32) for h in range(ATT_KV_HEADS)])
    gains = [gq_ref[...]] * ATT_HEADS + [gk_ref[...]] * ATT_KV_HEADS
    ms = [jnp.mean(x * x, axis=-1, keepdims=True) for x in xs]
    xn = [x * lax.rsqrt(m + EPS) * g for x, m, g in zip(xs, ms, gains)]
    lo = [pltpu.roll(x, LANES - half, 1) for x in xn]
    hi = [pltpu.roll(x, half, 1) for x in xn]
    ro = [x * cosf + a * slo + b * shi for x, a, b in zip(xn, lo, hi)]
    for h in range(ATT_HEADS):
        q_out[h * Q_BLOCK:(h + 1) * Q_BLOCK, :] = (ro[h] * (HEAD_DIM ** -0.5 * math.log2(math.e))).astype(q_out.dtype)
    for h in range(ATT_KV_HEADS):
        k_out[:, h * HEAD_DIM:(h + 1) * HEAD_DIM] = ro[ATT_HEADS + h].astype(k_out.dtype)
    r = lax.broadcasted_iota(jnp.int32, (IDX_DIM, IDX_DIM), 0)
    c = lax.broadcasted_iota(jnp.int32, (IDX_DIM, IDX_DIM), 1)
    swap = jnp.where(((c < half) & (r == c + half)) | ((c >= half) & (c < 2 * half) & (r == c - half)),
                     1.0, 0.0).astype(jnp.bfloat16)
    sin_signed = slo + shi
    xs = jnp.concatenate([iq_ref[:, h * IDX_DIM:(h + 1) * IDX_DIM] for h in range(IDX_HEADS)]
                         + [ik_ref[...]], axis=0)
    sw = jnp.dot(xs, swap, preferred_element_type=jnp.float32)
    nh = IDX_HEADS + 1
    rot = (xs.astype(jnp.float32).reshape(nh, Q_BLOCK, IDX_DIM) * cosf[None]
           + sw.reshape(nh, Q_BLOCK, IDX_DIM) * sin_signed[None]).reshape(nh * Q_BLOCK, IDX_DIM)
    iq_out[...] = rot[:IDX_HEADS * Q_BLOCK].astype(iq_out.dtype)
    ik_out[...] = rot[IDX_HEADS * Q_BLOCK:].astype(ik_out.dtype)
    w_out[...] = sm_ref[...] * (IDX_HEADS ** -0.5 * IDX_DIM ** -0.5)


def _rope_tables(seq):
    half = ROT_DIM // 2
    inv_freq = ROPE_THETA ** (-jnp.arange(half, dtype=jnp.float32) * 2.0 / ROT_DIM)
    ang = jnp.arange(seq, dtype=jnp.float32)[:, None] * inv_freq[None, :]
    cos, sin = jnp.cos(ang), jnp.sin(ang)
    z = jnp.zeros((seq, LANES - 2 * half), jnp.float32)
    zh = jnp.zeros((seq, half), jnp.float32)
    cosf = jnp.concatenate([cos, cos, jnp.ones_like(z)], axis=1)
    sin_lo = jnp.concatenate([-sin, zh, z], axis=1)
    sin_hi = jnp.concatenate([zh, sin, z], axis=1)
    return cosf, sin_lo, sin_hi


def _dsa_prep(proj, small, att_q_norm, att_k_norm, batch, seq):
    assert ROT_DIM == IDX_ROT
    t = batch * seq
    nq = seq // Q_BLOCK
    qb = Q_BLOCK
    cosf, slo, shi = _rope_tables(seq)
    tab = pl.BlockSpec((qb, LANES), lambda b, i: (i, 0))
    row = lambda w, off: pl.BlockSpec((qb, w), lambda b, i: (b * nq + i, off // w))
    vec = pl.BlockSpec((1, HEAD_DIM), lambda b, i: (0, 0))
    aq_w, ak_w = ATT_HEADS * HEAD_DIM, ATT_KV_HEADS * HEAD_DIM
    return pl.pallas_call(
        _dsa_prep_kernel, grid=(batch, nq),
        in_specs=[row(aq_w, OFF_AQ), row(ak_w, OFF_AK), row(IQ_W, OFF_IQ), row(IDX_DIM, OFF_IK),
                  pl.BlockSpec((qb, small.shape[1]), lambda b, i: (b * nq + i, 0)),
                  tab, tab, tab, vec, vec],
        out_specs=[pl.BlockSpec((ATT_HEADS * qb, HEAD_DIM), lambda b, i: (b * nq + i, 0)),
                   pl.BlockSpec((qb, ak_w), lambda b, i: (b * nq + i, 0)),
                   pl.BlockSpec((IDX_HEADS * qb, IDX_DIM), lambda b, i: (b * nq + i, 0)),
                   pl.BlockSpec((qb, IDX_DIM), lambda b, i: (b * nq + i, 0)),
                   pl.BlockSpec((qb, small.shape[1]), lambda b, i: (b * nq + i, 0))],
        out_shape=[jax.ShapeDtypeStruct((t * ATT_HEADS, HEAD_DIM), jnp.bfloat16),
                   jax.ShapeDtypeStruct((t, ak_w), jnp.bfloat16),
                   jax.ShapeDtypeStruct((t * IDX_HEADS, IDX_DIM), jnp.bfloat16),
                   jax.ShapeDtypeStruct((t, IDX_DIM), jnp.bfloat16),
                   jax.ShapeDtypeStruct((t, small.shape[1]), jnp.float32)],
        compiler_params=_params(("parallel", "parallel")), name="dsa_prep")(
            proj, proj, proj, proj, small, cosf, slo, shi,
            att_q_norm.reshape(1, HEAD_DIM).astype(jnp.float32),
            att_k_norm.reshape(1, HEAD_DIM).astype(jnp.float32))


DSA_KB = 512
DSA_SB = 512
DSA_HEAD_GROUPS = 8
INT_MIN = -2 ** 31


def _dsa_kernel(iq_ref, q_ref, w_ref, ik_ref, k_ref, v_ref, o_ref,
                key_ref, keyt_ref, wb_ref, m_ref, l_ref, acc_ref, *, topk):
    qi = pl.program_id(1)
    QB, KB, SB = Q_BLOCK, DSA_KB, DSA_SB
    n_kb = (qi * QB + QB + KB - 1) // KB
    qpos = qi * QB + lax.broadcasted_iota(jnp.int32, (QB, SB), 0)
    kiota = lax.broadcasted_iota(jnp.int32, (QB, SB), 1)

    w = w_ref[...]
    for h in range(IDX_HEADS):
        wb_ref[h] = jnp.broadcast_to(w[:, SM_IW + h:SM_IW + h + 1], (QB, LANES))

    def score_block(sb, carry):
        k0 = pl.multiple_of(sb * SB, SB)
        ikb = ik_ref[pl.ds(k0, SB), :]
        hg = IDX_HEADS // DSA_HEAD_GROUPS
        lgs = [lax.dot_general(iq_ref[g * hg * QB:(g + 1) * hg * QB, :], ikb, (((1,), (1,)), ((), ())),
                               preferred_element_type=jnp.float32) for g in range(DSA_HEAD_GROUPS)]
        sc = None
        for g, lg in enumerate(lgs):
            for hh in range(hg):
                wb = wb_ref[g * hg + hh]
                term = jnp.maximum(lg[hh * QB:(hh + 1) * QB], 0.0) * jnp.concatenate([wb] * (SB // LANES), axis=1)
                sc = term if sc is None else sc + term
        bits = pltpu.bitcast(sc + 0.0, jnp.int32)
        keys = bits ^ ((bits >> 31) & 0x7FFFFFFF)
        keys = jnp.where(k0 + kiota <= qpos, keys, INT_MIN)
        key_ref[:, pl.ds(k0, SB)] = keys
        keyt_ref[pl.ds(k0, SB), :] = keys.T
        return carry

    n_sb = (qi * QB + QB + SB - 1) // SB
    lax.fori_loop(0, n_sb, score_block, 0)

    @pl.when(n_sb < n_kb * (KB // SB))
    def _():
        k0 = pl.multiple_of(n_sb * SB, SB)
        key_ref[:, pl.ds(k0, SB)] = jnp.full((QB, SB), INT_MIN, jnp.int32)
        keyt_ref[pl.ds(k0, SB), :] = jnp.full((SB, QB), INT_MIN, jnp.int32)

    def count_ge(cand):
        def blk(kb, cnt):
            k0 = pl.multiple_of(kb * KB, KB)
            parts = [jnp.where(keyt_ref[pl.ds(k0 + 8 * j, 8), :] >= cand, 1.0, 0.0) for j in range(KB // 8)]
            while len(parts) > 1:
                parts = [parts[j] + parts[j + 1] for j in range(0, len(parts), 2)]
            return cnt + parts[0]
        cnt = lax.fori_loop(0, n_kb, blk, jnp.zeros((8, LANES), jnp.float32))
        return jnp.sum(cnt, axis=0, keepdims=True)

    def to_cols(x):
        return jnp.concatenate([x] * (QB // 8), axis=0).T

    tile = lambda x: jnp.broadcast_to(x, (8, LANES))
    cnt0 = tile(count_ge(jnp.zeros((8, LANES), jnp.int32)))
    thr0 = jnp.where(cnt0 >= topk, 0, jnp.full((8, LANES), INT_MIN, jnp.int32))

    def bit_step(i, carry):
        thr, cnt_thr = carry
        cand = thr | (jnp.int32(1) << (30 - i))
        cnt = tile(count_ge(cand))
        return jnp.where(cnt >= topk, cand, thr), jnp.where(cnt >= topk, cnt, cnt_thr)

    thr, cnt_thr = lax.fori_loop(0, 31, bit_step, (thr0, cnt0))

    surplus = jnp.where((thr > INT_MIN) & (cnt_thr > topk), 1.0, 0.0)

    @pl.when(jnp.max(surplus) > 0.0)
    def _():
        need = topk - tile(count_ge(thr + 1))
        sub8 = lax.broadcasted_iota(jnp.int32, (8, LANES), 0)

        def count_tied_before(idx):
            def blk(kb, cnt):
                k0 = pl.multiple_of(kb * KB, KB)
                for j in range(KB // 8):
                    kk = keyt_ref[pl.ds(k0 + 8 * j, 8), :]
                    hit = jnp.where(kk == thr, jnp.where(k0 + 8 * j + sub8 < idx, 1.0, 0.0), 0.0)
                    cnt = cnt + hit
                return cnt
            cnt = lax.fori_loop(0, n_kb, blk, jnp.zeros((8, LANES), jnp.float32))
            return tile(jnp.sum(cnt, axis=0, keepdims=True))

        n_bits = (key_ref.shape[1] - 1).bit_length()

        def idx_step(i, last):
            cand = last | (jnp.int32(1) << (n_bits - 1 - i))
            return jnp.where(count_tied_before(cand) < need, cand, last)

        last = lax.fori_loop(0, n_bits, idx_step, jnp.zeros((8, LANES), jnp.int32))
        thr_c = jnp.concatenate([to_cols(thr)] * (KB // LANES), axis=1)
        last_c = jnp.concatenate([to_cols(last)] * (KB // LANES), axis=1)
        flag_c = jnp.concatenate([to_cols(surplus)] * (KB // LANES), axis=1)
        pos = lax.broadcasted_iota(jnp.int32, (QB, KB), 1)

        def demote(kb, carry):
            k0 = pl.multiple_of(kb * KB, KB)
            kk = key_ref[:, pl.ds(k0, KB)]
            drop = jnp.where(kk == thr_c, jnp.where(k0 + pos > last_c, flag_c, 0.0), 0.0)
            key_ref[:, pl.ds(k0, KB)] = jnp.where(drop > 0.0, thr_c - 1, kk)
            return carry

        lax.fori_loop(0, n_kb, demote, 0)

    thr = to_cols(jnp.maximum(thr, INT_MIN + 1))

    G = ATT_GROUP
    NG = KB // LANES
    m_ref[...] = jnp.full(m_ref.shape, NEG_BIG, jnp.float32)
    l_ref[...] = jnp.zeros(l_ref.shape, jnp.float32)
    acc_ref[...] = jnp.zeros(acc_ref.shape, jnp.float32)

    def lane_fold(x, op):
        parts = [x[:, g * LANES:(g + 1) * LANES] for g in range(NG)]
        while len(parts) > 1:
            parts = [op(parts[i], parts[i + 1]) for i in range(0, len(parts), 2)]
        return parts[0]

    def attn_block(kb, carry):
        k0 = pl.multiple_of(kb * KB, KB)
        bias = jnp.where(key_ref[:, pl.ds(k0, KB)] >= jnp.concatenate([thr] * NG, axis=1), 0.0, NEG_BIG)
        bias_g = jnp.concatenate([bias] * G, axis=0)
        grp = range(ATT_KV_HEADS)
        rows = [slice(hk * G * QB, (hk + 1) * G * QB) for hk in grp]
        cols = [slice(hk * HEAD_DIM, (hk + 1) * HEAD_DIM) for hk in grp]
        s = [lax.dot_general(q_ref[rows[i], :], k_ref[pl.ds(k0, KB), cols[i]], (((1,), (1,)), ((), ())),
                             preferred_element_type=jnp.float32) for i in grp]
        p, alpha, m_new = [], [], []
        for i in grp:
            si = s[i] + bias_g
            m_old = m_ref[rows[i], :]
            mi = jnp.maximum(m_old, jnp.max(lane_fold(si, jnp.maximum), axis=1, keepdims=True))
            p.append(jnp.exp2(si - jnp.concatenate([mi] * NG, axis=1)).astype(jnp.bfloat16))
            alpha.append(jnp.exp2(m_old - mi))
            m_new.append(mi)
        ones = jnp.ones((KB, LANES), jnp.bfloat16)
        pv = [jnp.dot(p[i], jnp.concatenate([v_ref[pl.ds(k0, KB), cols[i]], ones], axis=1),
                      preferred_element_type=jnp.float32) for i in grp]
        for i in grp:
            l_ref[rows[i], :] = alpha[i] * l_ref[rows[i], :] + pv[i][:, HEAD_DIM:]
            acc_ref[rows[i], :] = alpha[i] * acc_ref[rows[i], :] + pv[i][:, :HEAD_DIM]
            m_ref[rows[i], :] = m_new[i]
        return carry

    lax.fori_loop(0, n_kb, attn_block, 0)
    for h in range(ATT_HEADS):
        rows = slice(h * QB, (h + 1) * QB)
        o_ref[:, h * HEAD_DIM:(h + 1) * HEAD_DIM] = (acc_ref[rows, :] / l_ref[rows, :]).astype(o_ref.dtype)


def _dsa(iq_hm, q_hm, w_s, ik_rot, k_att, proj, batch, seq, topk):
    nq = seq // Q_BLOCK
    qb = Q_BLOCK
    kvw = ATT_KV_HEADS * HEAD_DIM
    return pl.pallas_call(
        functools.partial(_dsa_kernel, topk=topk), grid=(batch, nq),
        in_specs=[pl.BlockSpec((IDX_HEADS * qb, IDX_DIM), lambda b, i: (b * nq + i, 0)),
                  pl.BlockSpec((ATT_HEADS * qb, HEAD_DIM), lambda b, i: (b * nq + i, 0)),
                  pl.BlockSpec((qb, w_s.shape[1]), lambda b, i: (b * nq + i, 0)),
                  pl.BlockSpec((seq, IDX_DIM), lambda b, i: (b, 0)),
                  pl.BlockSpec((seq, kvw), lambda b, i: (b, 0)),
                  pl.BlockSpec((seq, kvw), lambda b, i: (b, OFF_AV // kvw))],
        out_specs=pl.BlockSpec((qb, ATT_HEADS * HEAD_DIM), lambda b, i: (b * nq + i, 0)),
        out_shape=jax.ShapeDtypeStruct((batch * seq, ATT_HEADS * HEAD_DIM), jnp.bfloat16),
        scratch_shapes=[pltpu.VMEM((qb, seq), jnp.int32),
                        pltpu.VMEM((seq, qb), jnp.int32),
                        pltpu.VMEM((IDX_HEADS, qb, LANES), jnp.float32),
                        pltpu.VMEM((ATT_HEADS * qb, LANES), jnp.float32),
                        pltpu.VMEM((ATT_HEADS * qb, LANES), jnp.float32),
                        pltpu.VMEM((ATT_HEADS * qb, HEAD_DIM), jnp.float32)],
        compiler_params=_params(("parallel", "arbitrary")), name="dsa")(
            iq_hm, q_hm, w_s, ik_rot, k_att, proj)


def _mem_block_kernel(x_ref, gin_ref, wq_ref, kv_ref, gq_ref, gk_ref, wo_ref, gout_ref, y_ref, h_ref):
    def norm(t, gain):
        return t * lax.rsqrt(jnp.mean(t * t, axis=-1, keepdims=True) + EPS) * gain

    x = x_ref[...]
    q_all = jnp.dot(norm(x, gin_ref[...]).astype(jnp.bfloat16), wq_ref[...], preferred_element_type=jnp.float32)
    hs = range(MEM_HEADS)
    cols = [slice(h * MEM_HEAD_DIM, (h + 1) * MEM_HEAD_DIM) for h in hs]
    q = [norm(q_all[:, c], gq_ref[...]) * (MEM_HEAD_DIM ** -0.5) for c in cols]
    k = [norm(kv_ref[:, c], gk_ref[...]) for c in cols]
    s = [_bdot_nt(q[h], k[h]) for h in hs]
    p = [jnp.exp(s[h] - jnp.max(s[h], axis=1, keepdims=True)) for h in hs]
    pv = [_bdot(p[h], kv_ref[:, MEM_WIDTH + h * MEM_HEAD_DIM:MEM_WIDTH + (h + 1) * MEM_HEAD_DIM]) for h in hs]
    o = [pv[h] / jnp.sum(p[h], axis=1, keepdims=True) for h in hs]
    mo = jnp.concatenate(o, axis=1).astype(jnp.bfloat16)
    y = x + jnp.dot(mo, wo_ref[...], preferred_element_type=jnp.float32)
    y_ref[...] = y
    h_ref[...] = norm(y, gout_ref[...]).astype(h_ref.dtype)


def _mem_block(x2d, mkv, norm_in, w_q, gq, gk, w_o, norm_out, batch, seq, n_mem):
    t, d = x2d.shape
    tm = _pick(seq, 256)
    nq = seq // tm
    row = pl.BlockSpec((tm, d), lambda b, i: (b * nq + i, 0))
    full = lambda a: pl.BlockSpec(a.shape, lambda b, i: (0, 0))
    vec = lambda v, n: v.reshape(1, n).astype(jnp.float32)
    args = [x2d, vec(norm_in, d), w_q, mkv, vec(gq, MEM_HEAD_DIM), vec(gk, MEM_HEAD_DIM), w_o, vec(norm_out, d)]
    specs = [row, full(args[1]), full(w_q), pl.BlockSpec((n_mem, 2 * MEM_WIDTH), lambda b, i: (b, 0)),
             full(args[4]), full(args[5]), full(w_o), full(args[7])]
    return pl.pallas_call(
        _mem_block_kernel, grid=(batch, nq), in_specs=specs, out_specs=[row, row],
        out_shape=[jax.ShapeDtypeStruct((t, d), jnp.float32), jax.ShapeDtypeStruct((t, d), jnp.bfloat16)],
        compiler_params=_params(("parallel", "parallel")), name="mem_block")(*args)


def _split_w_in(w):
    sizes = (GDN_QK, GDN_QK, GDN_V, GDN_V, GDN_HEADS, GDN_HEADS, ATT_HEADS * HEAD_DIM,
             ATT_KV_HEADS * HEAD_DIM, ATT_KV_HEADS * HEAD_DIM, IQ_W, IDX_DIM, IDX_HEADS)
    offs = np.concatenate([[0], np.cumsum(sizes)])
    wt = w.T
    seg = lambda i: wt[int(offs[i]):int(offs[i + 1]), :]
    gq, gk, gv, gz, ga, gb, aq, ak, av, iq, ik, iw = (seg(i) for i in range(12))
    d = w.shape[0]
    bf = jnp.bfloat16
    main = jnp.concatenate([s.astype(bf) for s in (iq, gq, gk, gv, gz, aq, ak, av, ik)]
                           + [jnp.zeros((N_MAIN_PAD - N_MAIN, d), bf)], axis=0)
    nsm = 2 * GDN_HEADS + IDX_HEADS
    small = jnp.concatenate([ga, gb, iw, jnp.zeros((LANES - nsm, d), w.dtype)], axis=0)
    small_hi = small.astype(bf)
    small_lo = (small - small_hi.astype(jnp.float32)).astype(bf)
    return main, small_hi, small_lo


def kernel(x, mem, norm_mix, w_in, conv_w, a_log, dt_bias, gdn_norm, att_q_norm, att_k_norm, w_out,
           norm_mem_q, norm_mem_kv, w_mem_q, w_mem_kv, mem_q_norm, mem_k_norm, w_mem_o, norm_ffn,
           w_gate, w_up, w_down):
    B, S, D = x.shape
    M = mem.shape[1]
    T = B * S
    topk = min(TOPK_MAX, S // 4)
    bf = jnp.bfloat16
    x2 = x.reshape(T, D)
    for l in range(w_in.shape[0]):
        w_main, *w_small = _split_w_in(w_in[l])
        h, small = _rmsnorm(x2, norm_mix[l], w_small)
        proj = _matmul([(h, w_main)], out_dtype=bf, tn=IN_TN, w_t=True, name="in_proj")
        qkv = _gdn_prep(proj, conv_w[l], S)
        gates, decay = _gate_prep(small, a_log[l], dt_bias[l])
        o_a = _gdn(qkv, proj, gates, decay, gdn_norm[l], B, S)
        q_hm, k_att, iq_hm, ik_rot, w_s = _dsa_prep(proj, small, att_q_norm[l], att_k_norm[l], B, S)
        o_b = _dsa(iq_hm, q_hm, w_s, ik_rot, k_att, proj, B, S, topk)
        wo = w_out[l].astype(bf)
        x2 = _matmul([(o_a, wo[:GDN_V]), (o_b, wo[GDN_V:])], res=x2, name="out_proj")
        hm = _rmsnorm(mem.reshape(B * M, D), norm_mem_kv[l])
        mkv = _matmul([(hm, w_mem_kv[l].astype(bf))], name="mem_kv")
        x2, hf = _mem_block(x2, mkv, norm_mem_q[l], w_mem_q[l].astype(bf), mem_q_norm[l], mem_k_norm[l],
                            w_mem_o[l].astype(bf), norm_ffn[l], B, S, M)
        mid = _gateup(hf, w_gate[l], w_up[l])
        x2 = _matmul([(mid, w_down[l].astype(bf))], res=x2, tm=512, tn=512, name="ffn_down")
    return x2.reshape(B, S, D)
```

```python
import functools
import math

import jax
import jax.numpy as jnp
import numpy as np
from jax import lax
from jax.experimental import pallas as pl
from jax.experimental.pallas import tpu as pltpu

D_MODEL = 4096
ATT_KV_HEADS = 4
IDX_HEADS = 32
TOPK_MAX = 256

EPS = 1e-6
ROPE_THETA = 500000.0
HEAD_DIM = 128
ROT_DIM = HEAD_DIM // 4
GDN_DK = 128
GDN_DV = 128
GDN_HEADS = (D_MODEL // 2) // GDN_DV
GDN_CONV = 4
GDN_CHUNK = 64
GDN_QK = GDN_HEADS * GDN_DK
GDN_V = GDN_HEADS * GDN_DV
ATT_HEADS = (D_MODEL // 2) // HEAD_DIM
ATT_GROUP = ATT_HEADS // ATT_KV_HEADS
IDX_DIM = 128
IDX_ROT = IDX_DIM // 4
Q_BLOCK = 128
MEM_HEADS = 4
MEM_HEAD_DIM = 128
MEM_WIDTH = MEM_HEADS * MEM_HEAD_DIM

LANES = 128
VMEM_LIMIT = 56 * 1024 * 1024
NEG_BIG = -1e30

IQ_W = IDX_HEADS * IDX_DIM
OFF_IQ = 0
OFF_GQ = OFF_IQ + IQ_W
OFF_GK = OFF_GQ + GDN_QK
OFF_GV = OFF_GK + GDN_QK
OFF_GZ = OFF_GV + GDN_V
OFF_AQ = OFF_GZ + GDN_V
OFF_AK = OFF_AQ + ATT_HEADS * HEAD_DIM
OFF_AV = OFF_AK + ATT_KV_HEADS * HEAD_DIM
OFF_IK = OFF_AV + ATT_KV_HEADS * HEAD_DIM
N_MAIN = OFF_IK + IDX_DIM
IN_TN = 512
N_MAIN_PAD = -(-N_MAIN // IN_TN) * IN_TN
SM_GA = 0
SM_GB = GDN_HEADS
SM_IW = 2 * GDN_HEADS
HI = lax.Precision.HIGHEST


def _params(sem):
    return pltpu.CompilerParams(dimension_semantics=sem, vmem_limit_bytes=VMEM_LIMIT)


def _pick(n, pref):
    t = min(n, pref)
    while n % t:
        t //= 2
    return t


def _sigmoid(x):
    return 1.0 / (1.0 + jnp.exp2(x * (-math.log2(math.e))))


def _rmsnorm_kernel(x_ref, g_ref, o_ref):
    x = x_ref[...]
    ms = jnp.mean(x * x, axis=-1, keepdims=True)
    o_ref[...] = (x * lax.rsqrt(ms + EPS) * g_ref[...]).astype(o_ref.dtype)


def _rmsnorm_side_kernel(x_ref, g_ref, whi_ref, wlo_ref, o_ref, s_ref):
    x = x_ref[...]
    ms = jnp.mean(x * x, axis=-1, keepdims=True)
    xn = x * lax.rsqrt(ms + EPS) * g_ref[...]
    hi = xn.astype(o_ref.dtype)
    o_ref[...] = hi
    lo = (xn - hi.astype(jnp.float32)).astype(hi.dtype)
    dot_t = lambda a, wt: lax.dot_general(a, wt, (((1,), (1,)), ((), ())), preferred_element_type=jnp.float32)
    s_ref[...] = dot_t(hi, whi_ref[...]) + dot_t(lo, whi_ref[...]) + dot_t(hi, wlo_ref[...])


def _rmsnorm(x2d, g, w_side=None, tm=256):
    m, d = x2d.shape
    tm = _pick(m, tm)
    g2 = g.reshape(1, d).astype(jnp.float32)
    xs = pl.BlockSpec((tm, d), lambda i: (i, 0))
    gs = pl.BlockSpec((1, d), lambda i: (0, 0))
    if w_side is None:
        return pl.pallas_call(
            _rmsnorm_kernel, grid=(m // tm,), in_specs=[xs, gs], out_specs=xs,
            out_shape=jax.ShapeDtypeStruct((m, d), jnp.bfloat16),
            compiler_params=_params(("parallel",)), name="rmsnorm")(x2d, g2)
    w_hi, w_lo = w_side
    ns = w_hi.shape[0]
    ws = pl.BlockSpec((ns, d), lambda i: (0, 0))
    return pl.pallas_call(
        _rmsnorm_side_kernel, grid=(m // tm,),
        in_specs=[xs, gs, ws, ws],
        out_specs=[xs, pl.BlockSpec((tm, ns), lambda i: (i, 0))],
        out_shape=[jax.ShapeDtypeStruct((m, d), jnp.bfloat16),
                   jax.ShapeDtypeStruct((m, ns), jnp.float32)],
        compiler_params=_params(("parallel",)), name="rmsnorm_side")(x2d, g2, w_hi, w_lo)


def _matmul_kernel(*refs, n_pairs, has_res, w_t):
    o_ref = refs[-1]
    rhs_dim = 1 if w_t else 0
    acc = None
    for p in range(n_pairs):
        d = lax.dot_general(refs[2 * p][...], refs[2 * p + 1][...], (((1,), (rhs_dim,)), ((), ())),
                            preferred_element_type=jnp.float32)
        acc = d if acc is None else acc + d
    if has_res:
        acc = acc + refs[2 * n_pairs][...]
    o_ref[...] = acc.astype(o_ref.dtype)


def _matmul(pairs, res=None, out_dtype=jnp.float32, tm=1024, tn=512, w_t=False, name="matmul"):
    m = pairs[0][0].shape[0]
    n = pairs[0][1].shape[0 if w_t else 1]
    tm, tn = _pick(m, tm), _pick(n, tn)
    args, specs = [], []
    for a, w in pairs:
        k = a.shape[1]
        args += [a, w]
        wspec = pl.BlockSpec((tn, k), lambda i, j: (j, 0)) if w_t else pl.BlockSpec((k, tn), lambda i, j: (0, j))
        specs += [pl.BlockSpec((tm, k), lambda i, j: (i, 0)), wspec]
    if res is not None:
        args.append(res)
        specs.append(pl.BlockSpec((tm, tn), lambda i, j: (i, j)))
    return pl.pallas_call(
        functools.partial(_matmul_kernel, n_pairs=len(pairs), has_res=res is not None, w_t=w_t),
        grid=(m // tm, n // tn), in_specs=specs,
        out_specs=pl.BlockSpec((tm, tn), lambda i, j: (i, j)),
        out_shape=jax.ShapeDtypeStruct((m, n), out_dtype),
        compiler_params=_params(("parallel", "arbitrary")), name=name)(*args)


def _gateup_kernel(a_ref, wg_ref, wu_ref, o_ref):
    a = a_ref[...]
    g = jnp.dot(a, wg_ref[...].astype(a.dtype), preferred_element_type=jnp.float32)
    u = jnp.dot(a, wu_ref[...].astype(a.dtype), preferred_element_type=jnp.float32)
    o_ref[...] = (g * _sigmoid(g) * u).astype(o_ref.dtype)


def _gateup(a, wg, wu, tm=1024, tn=256):
    m, k = a.shape
    n = wg.shape[1]
    tm, tn = _pick(m, tm), _pick(n, tn)
    ws = pl.BlockSpec((k, tn), lambda i, j: (0, j))
    return pl.pallas_call(
        _gateup_kernel, grid=(m // tm, n // tn),
        in_specs=[pl.BlockSpec((tm, k), lambda i, j: (i, 0)), ws, ws],
        out_specs=pl.BlockSpec((tm, tn), lambda i, j: (i, j)),
        out_shape=jax.ShapeDtypeStruct((m, n), jnp.bfloat16),
        compiler_params=_params(("parallel", "arbitrary")), name="ffn_gateup")(a, wg, wu)


HALO = 16


def _gdn_prep_kernel(x_ref, halo_ref, w_ref, o_ref, *, ts, tc, blocks_per_seq, nqk_blocks):
    i = pl.program_id(0)
    j = pl.program_id(1)
    first = (i % blocks_per_seq) == 0
    x = x_ref[...].astype(jnp.float32)
    halo8 = jnp.where(first, 0.0, halo_ref[HALO - 8:HALO, :].astype(jnp.float32))
    w = w_ref[...]
    row8 = lax.broadcasted_iota(jnp.int32, (8, tc), 0)
    y = x * w[GDN_CONV - 1:GDN_CONV, :]
    for d in range(1, GDN_CONV):
        xs = pltpu.roll(x, d, 0)
        head = jnp.where(row8 < d, pltpu.roll(halo8, d, 0), xs[0:8])
        xs = jnp.concatenate([head, xs[8:]], axis=0)
        y = y + xs * w[GDN_CONV - 1 - d:GDN_CONV - d, :]
    y = y * _sigmoid(y)
    is_q = j < nqk_blocks
    is_v = j >= 2 * nqk_blocks
    for h in range(tc // GDN_DK):
        yh = y[:, h * GDN_DK:(h + 1) * GDN_DK]
        rs = lax.rsqrt(jnp.sum(yh * yh, axis=-1, keepdims=True) + EPS)
        rs = rs * jnp.where(is_q, GDN_DK ** -0.5, 1.0)
        fac = jnp.where(is_v, 1.0, rs)
        o_ref[:, h * GDN_DK:(h + 1) * GDN_DK] = (yh * fac).astype(o_ref.dtype)


def _gdn_prep(proj, conv_w, seq):
    t = proj.shape[0]
    ts = _pick(seq, 512)
    tc = 512
    nconv = (2 * GDN_QK + GDN_V) // tc
    col0 = OFF_GQ // tc
    rb = ts // HALO
    kern = functools.partial(_gdn_prep_kernel, ts=ts, tc=tc, blocks_per_seq=seq // ts,
                             nqk_blocks=GDN_QK // tc)
    return pl.pallas_call(
        kern, grid=(t // ts, nconv),
        in_specs=[pl.BlockSpec((ts, tc), lambda i, j: (i, col0 + j)),
                  pl.BlockSpec((HALO, tc), lambda i, j: (jnp.maximum(i * rb - 1, 0), col0 + j)),
                  pl.BlockSpec((GDN_CONV, tc), lambda i, j: (0, j))],
        out_specs=pl.BlockSpec((ts, tc), lambda i, j: (i, j)),
        out_shape=jax.ShapeDtypeStruct((t, nconv * tc), jnp.bfloat16),
        compiler_params=_params(("parallel", "parallel")), name="gdn_prep")(proj, proj, conv_w)


def _gate_prep_kernel(s_ref, alog_ref, dtb_ref, o_ref, dec_ref, *, ts):
    C = GDN_CHUNK
    s = s_ref[...]
    lane = lax.broadcasted_iota(jnp.int32, s.shape, 1)
    xa = s + dtb_ref[...]
    softplus = jnp.maximum(xa, 0.0) + jnp.log(1.0 + jnp.exp(-jnp.abs(xa)))
    g = -jnp.exp(alog_ref[...]) * softplus
    g = jnp.where(lane < GDN_HEADS, g, 0.0)
    r = lax.broadcasted_iota(jnp.int32, (ts, ts), 0)
    c = lax.broadcasted_iota(jnp.int32, (ts, ts), 1)
    tri = jnp.where((r // C == c // C) & (c <= r), 1.0, 0.0)
    gcum = jnp.dot(tri, g, precision=HI, preferred_element_type=jnp.float32)
    beta = _sigmoid(s)
    is_beta = (lane >= SM_GB) & (lane < SM_GB + GDN_HEADS)
    o_ref[...] = jnp.where(lane < GDN_HEADS, gcum, jnp.where(is_beta, beta, 0.0))
    gcum_t = gcum.T
    incl = lax.broadcasted_iota(jnp.int32, (C, C), 1) <= lax.broadcasted_iota(jnp.int32, (C, C), 0)
    sub = lax.broadcasted_iota(jnp.int32, (8, C), 0)
    hc = [(h, ci) for h in range(GDN_HEADS) for ci in range(ts // C)]
    cols = [jnp.broadcast_to(gcum[ci * C:(ci + 1) * C, h:h + 1], (C, C)) for h, ci in hc]
    tiles = [gcum_t[8 * (h // 8):8 * (h // 8) + 8, ci * C:(ci + 1) * C] for h, ci in hc]
    rows = [jnp.sum(jnp.where(sub == h % 8, t, 0.0), axis=0, keepdims=True) for (h, ci), t in zip(hc, tiles)]
    for (h, ci), col, row in zip(hc, cols, rows):
        dec_ref[ci * C:(ci + 1) * C, h * C:(h + 1) * C] = jnp.exp(jnp.where(incl, col - row, NEG_BIG))


def _gate_prep(small, a_log, dt_bias):
    t, ns = small.shape
    ts = _pick(t, 256)
    pad = lambda v: jnp.zeros((1, ns), jnp.float32).at[0, :GDN_HEADS].set(v.astype(jnp.float32))
    vs = pl.BlockSpec((1, ns), lambda i: (0, 0))
    bs = pl.BlockSpec((ts, ns), lambda i: (i, 0))
    dw = GDN_HEADS * GDN_CHUNK
    return pl.pallas_call(
        functools.partial(_gate_prep_kernel, ts=ts), grid=(t // ts,),
        in_specs=[bs, vs, vs], out_specs=[bs, pl.BlockSpec((ts, dw), lambda i: (i, 0))],
        out_shape=[jax.ShapeDtypeStruct((t, ns), jnp.float32), jax.ShapeDtypeStruct((t, dw), jnp.float32)],
        compiler_params=_params(("parallel",)), name="gate_prep")(small, pad(a_log), pad(dt_bias))


GDN_WY_HEADS = 4
GDN_WY_ROWS = 512
GDN_WY_CG = 8
GDN_SCAN_HEADS = 8
GDN_SCAN_ROWS = 1024


def _bdot(a, b):
    return jnp.dot(a.astype(jnp.bfloat16), b.astype(jnp.bfloat16), preferred_element_type=jnp.float32)


def _bdot_nt(a, b):
    return lax.dot_general(a.astype(jnp.bfloat16), b.astype(jnp.bfloat16), (((1,), (1,)), ((), ())),
                           preferred_element_type=jnp.float32)


def _head_col(gsl, idx):
    lane = lax.broadcasted_iota(jnp.int32, (1, LANES), 1)
    return jnp.sum(gsl * jnp.where(lane == idx, 1.0, 0.0), axis=1, keepdims=True)


def _gdn_wy_kernel(q_ref, k_ref, v_ref, g_ref, dec_ref, u_ref, w_ref, qd_ref, kd_ref, qk_ref, *, hb):
    hg = pl.program_id(1)
    C, CG = GDN_CHUNK, GDN_WY_CG
    r = lax.broadcasted_iota(jnp.int32, (C, C), 0)
    c = lax.broadcasted_iota(jnp.int32, (C, C), 1)
    off_diag = c != r

    def body(it, carry):
        where, q, k, v, gc, beta, decay = [], [], [], [], [], [], []
        for cc in range(CG):
            r0 = pl.multiple_of((it * CG + cc) * C, C)
            gsl = g_ref[pl.ds(r0, C), :]
            for hh in range(hb):
                sl = slice(hh * GDN_DK, (hh + 1) * GDN_DK)
                h = hg * hb + hh
                where.append((r0, hh))
                q.append(q_ref[pl.ds(r0, C), sl].astype(jnp.float32))
                k.append(k_ref[pl.ds(r0, C), sl].astype(jnp.float32))
                v.append(v_ref[pl.ds(r0, C), sl].astype(jnp.float32))
                gc.append(_head_col(gsl, h))
                beta.append(_head_col(gsl, SM_GB + h))
                decay.append(dec_ref[pl.ds(r0, C), hh * C:(hh + 1) * C])
        n = len(where)
        ch = range(n)
        kb = [k[i] * beta[i] for i in ch]
        eg = [jnp.exp(gc[i]) for i in ch]
        kk = [_bdot_nt(kb[i], k[i]) for i in ch]
        qk = [_bdot_nt(q[i], k[i]) for i in ch]
        pw = [jnp.where(off_diag, kk[i] * decay[i], 0.0) for i in ch]
        x = [jnp.concatenate([v[i] * beta[i], kb[i] * eg[i]], axis=1) for i in ch]
        n_lvl = int(math.log2(C))
        for lvl in range(n_lvl):
            px = [_bdot(pw[i], x[i]) for i in ch]
            if lvl + 1 < n_lvl:
                pw = [_bdot(pw[i], pw[i]) for i in ch]
            x = [x[i] - px[i] if lvl == 0 else x[i] + px[i] for i in ch]
        for i, (r0, hh) in enumerate(where):
            sl = slice(hh * GDN_DK, (hh + 1) * GDN_DK)
            u_ref[pl.ds(r0, C), sl] = x[i][:, :GDN_DV]
            w_ref[pl.ds(r0, C), sl] = x[i][:, GDN_DV:].astype(w_ref.dtype)
            qd_ref[pl.ds(r0, C), sl] = (q[i] * eg[i]).astype(qd_ref.dtype)
            kd_ref[pl.ds(r0, C), sl] = (k[i] * jnp.exp(gc[i][C - 1:C, :] - gc[i])).astype(kd_ref.dtype)
            qk_ref[pl.ds(r0, C), hh * C:(hh + 1) * C] = (qk[i] * decay[i]).astype(qk_ref.dtype)
        return carry

    lax.fori_loop(0, q_ref.shape[0] // (CG * C), body, 0)


def _gdn_scan_kernel(u_ref, w_ref, qd_ref, kd_ref, qk_ref, z_ref, g_ref, gain_ref, o_ref, state_ref, *, hb):
    hg = pl.program_id(1)
    C = GDN_CHUNK
    sub8 = lax.broadcasted_iota(jnp.int32, (8, LANES), 0)
    gain = gain_ref[...]

    @pl.when(pl.program_id(2) == 0)
    def _():
        state_ref[...] = jnp.zeros(state_ref.shape, state_ref.dtype)

    def body(ci, states):
        r0 = pl.multiple_of(ci * C, C)
        g_tail = g_ref[pl.ds(r0 + C - 8, 8), :]
        g_last_row = jnp.sum(jnp.where(sub8 == 7, g_tail, 0.0), axis=0, keepdims=True)
        loaded = []
        for hh in range(hb):
            sl = slice(hh * GDN_DK, (hh + 1) * GDN_DK)
            loaded.append((u_ref[pl.ds(r0, C), sl], w_ref[pl.ds(r0, C), sl], qd_ref[pl.ds(r0, C), sl],
                           kd_ref[pl.ds(r0, C), sl], qk_ref[pl.ds(r0, C), hh * C:(hh + 1) * C],
                           z_ref[pl.ds(r0, C), sl].astype(jnp.float32)))
        hs = range(hb)
        f32 = jnp.float32
        u, w, qd, kd, qk, z = zip(*loaded)
        sb = [states[i].astype(jnp.bfloat16) for i in hs]
        ws = [jnp.dot(w[i], sb[i], preferred_element_type=f32) for i in hs]
        qs = [jnp.dot(qd[i], sb[i], preferred_element_type=f32) for i in hs]
        vb = [(u[i] - ws[i]).astype(jnp.bfloat16) for i in hs]
        kv = [lax.dot_general(kd[i], vb[i], (((0,), (0,)), ((), ())), preferred_element_type=f32) for i in hs]
        qv = [jnp.dot(qk[i], vb[i], preferred_element_type=f32) for i in hs]
        a = [jnp.exp(_head_col(g_last_row, hg * hb + i)) for i in hs]
        new_states = tuple(states[i] * a[i] + kv[i] for i in hs)
        for i in hs:
            o = qs[i] + qv[i]
            on = o * lax.rsqrt(jnp.mean(o * o, axis=-1, keepdims=True) + EPS) * gain
            o_ref[pl.ds(r0, C), i * GDN_DV:(i + 1) * GDN_DV] = (on * (z[i] * _sigmoid(z[i]))).astype(o_ref.dtype)
        return new_states

    init = tuple(state_ref[hh] for hh in range(hb))
    final = lax.fori_loop(0, u_ref.shape[0] // C, body, init)
    for hh in range(hb):
        state_ref[hh] = final[hh]


def _gdn(qkv, proj, gates, decay, gdn_norm, batch, seq):
    t = batch * seq
    bf = jnp.bfloat16
    C = GDN_CHUNK
    hb = min(GDN_WY_HEADS, GDN_HEADS)
    ts = _pick(seq, GDN_WY_ROWS)
    assert GDN_HEADS % hb == 0 and ts % (GDN_WY_CG * C) == 0
    hw = hb * GDN_DK
    kq = GDN_QK // hw
    blk = lambda off: pl.BlockSpec((ts, hw), lambda i, g: (i, off + g))
    oblk = pl.BlockSpec((ts, hw), lambda i, g: (i, g))
    cblk = pl.BlockSpec((ts, hb * C), lambda i, g: (i, g))
    u, w, qd, kd, qk = pl.pallas_call(
        functools.partial(_gdn_wy_kernel, hb=hb), grid=(t // ts, GDN_HEADS // hb),
        in_specs=[blk(0), blk(kq), blk(2 * kq),
                  pl.BlockSpec((ts, gates.shape[1]), lambda i, g: (i, 0)), cblk],
        out_specs=[oblk, oblk, oblk, oblk, cblk],
        out_shape=[jax.ShapeDtypeStruct((t, GDN_V), jnp.float32), jax.ShapeDtypeStruct((t, GDN_QK), bf),
                   jax.ShapeDtypeStruct((t, GDN_QK), bf), jax.ShapeDtypeStruct((t, GDN_QK), bf),
                   jax.ShapeDtypeStruct((t, GDN_HEADS * C), bf)],
        compiler_params=_params(("parallel", "parallel")), name="gdn_wy")(qkv, qkv, qkv, gates, decay)
    hb = min(GDN_SCAN_HEADS, GDN_HEADS)
    ts = _pick(seq, GDN_SCAN_ROWS)
    assert GDN_HEADS % hb == 0
    nsb = seq // ts
    hw = hb * GDN_DK
    zoff = OFF_GZ // hw
    blk = pl.BlockSpec((ts, hw), lambda b, g, s: (b * nsb + s, g))
    return pl.pallas_call(
        functools.partial(_gdn_scan_kernel, hb=hb), grid=(batch, GDN_HEADS // hb, nsb),
        in_specs=[blk, blk, blk, blk,
                  pl.BlockSpec((ts, hb * C), lambda b, g, s: (b * nsb + s, g)),
                  pl.BlockSpec((ts, hw), lambda b, g, s: (b * nsb + s, zoff + g)),
                  pl.BlockSpec((ts, gates.shape[1]), lambda b, g, s: (b * nsb + s, 0)),
                  pl.BlockSpec((1, GDN_DV), lambda b, g, s: (0, 0))],
        out_specs=blk,
        out_shape=jax.ShapeDtypeStruct((t, GDN_V), bf),
        scratch_shapes=[pltpu.VMEM((hb, GDN_DK, GDN_DV), jnp.float32)],
        compiler_params=_params(("parallel", "parallel", "arbitrary")), name="gdn_scan")(
            u, w, qd, kd, qk, proj, gates, gdn_norm.reshape(1, GDN_DV).astype(jnp.float32))


def _dsa_prep_kernel(aq_ref, ak_ref, iq_ref, ik_ref, sm_ref, cos_ref, slo_ref, shi_ref, gq_ref, gk_ref,
                     q_out, k_out, iq_out, ik_out, w_out):
    cosf, slo, shi = cos_ref[...], slo_ref[...], shi_ref[...]
    half = ROT_DIM // 2

    xs = ([aq_ref[:, h * HEAD_DIM:(h + 1) * HEAD_DIM].astype(jnp.float32) for h in range(ATT_HEADS)]
          + [ak_ref[:, h * HEAD_DIM:(h + 1) * HEAD_DIM].astype(jnp.float32) for h in range(ATT_KV_HEADS)])
    gains = [gq_ref[...]] * ATT_HEADS + [gk_ref[...]] * ATT_KV_HEADS
    ms = [jnp.mean(x * x, axis=-1, keepdims=True) for x in xs]
    xn = [x * lax.rsqrt(m + EPS) * g for x, m, g in zip(xs, ms, gains)]
    lo = [pltpu.roll(x, LANES - half, 1) for x in xn]
    hi = [pltpu.roll(x, half, 1) for x in xn]
    ro = [x * cosf + a * slo + b * shi for x, a, b in zip(xn, lo, hi)]
    for h in range(ATT_HEADS):
        q_out[h * Q_BLOCK:(h + 1) * Q_BLOCK, :] = (ro[h] * (HEAD_DIM ** -0.5 * math.log2(math.e))).astype(q_out.dtype)
    for h in range(ATT_KV_HEADS):
        k_out[:, h * HEAD_DIM:(h + 1) * HEAD_DIM] = ro[ATT_HEADS + h].astype(k_out.dtype)
    r = lax.broadcasted_iota(jnp.int32, (IDX_DIM, IDX_DIM), 0)
    c = lax.broadcasted_iota(jnp.int32, (IDX_DIM, IDX_DIM), 1)
    swap = jnp.where(((c < half) & (r == c + half)) | ((c >= half) & (c < 2 * half) & (r == c - half)),
                     1.0, 0.0).astype(jnp.bfloat16)
    sin_signed = slo + shi
    xs = jnp.concatenate([iq_ref[:, h * IDX_DIM:(h + 1) * IDX_DIM] for h in range(IDX_HEADS)]
                         + [ik_ref[...]], axis=0)
    sw = jnp.dot(xs, swap, preferred_element_type=jnp.float32)
    nh = IDX_HEADS + 1
    rot = (xs.astype(jnp.float32).reshape(nh, Q_BLOCK, IDX_DIM) * cosf[None]
           + sw.reshape(nh, Q_BLOCK, IDX_DIM) * sin_signed[None]).reshape(nh * Q_BLOCK, IDX_DIM)
    iq_out[...] = rot[:IDX_HEADS * Q_BLOCK].astype(iq_out.dtype)
    ik_out[...] = rot[IDX_HEADS * Q_BLOCK:].astype(ik_out.dtype)
    w_out[...] = sm_ref[...] * (IDX_HEADS ** -0.5 * IDX_DIM ** -0.5)


def _rope_tables(seq):
    half = ROT_DIM // 2
    inv_freq = ROPE_THETA ** (-jnp.arange(half, dtype=jnp.float32) * 2.0 / ROT_DIM)
    ang = jnp.arange(seq, dtype=jnp.float32)[:, None] * inv_freq[None, :]
    cos, sin = jnp.cos(ang), jnp.sin(ang)
    z = jnp.zeros((seq, LANES - 2 * half), jnp.float32)
    zh = jnp.zeros((seq, half), jnp.float32)
    cosf = jnp.concatenate([cos, cos, jnp.ones_like(z)], axis=1)
    sin_lo = jnp.concatenate([-sin, zh, z], axis=1)
    sin_hi = jnp.concatenate([zh, sin, z], axis=1)
    return cosf, sin_lo, sin_hi


def _dsa_prep(proj, small, att_q_norm, att_k_norm, batch, seq):
    assert ROT_DIM == IDX_ROT
    t = batch * seq
    nq = seq // Q_BLOCK
    qb = Q_BLOCK
    cosf, slo, shi = _rope_tables(seq)
    tab = pl.BlockSpec((qb, LANES), lambda b, i: (i, 0))
    row = lambda w, off: pl.BlockSpec((qb, w), lambda b, i: (b * nq + i, off // w))
    vec = pl.BlockSpec((1, HEAD_DIM), lambda b, i: (0, 0))
    aq_w, ak_w = ATT_HEADS * HEAD_DIM, ATT_KV_HEADS * HEAD_DIM
    return pl.pallas_call(
        _dsa_prep_kernel, grid=(batch, nq),
        in_specs=[row(aq_w, OFF_AQ), row(ak_w, OFF_AK), row(IQ_W, OFF_IQ), row(IDX_DIM, OFF_IK),
                  pl.BlockSpec((qb, small.shape[1]), lambda b, i: (b * nq + i, 0)),
                  tab, tab, tab, vec, vec],
        out_specs=[pl.BlockSpec((ATT_HEADS * qb, HEAD_DIM), lambda b, i: (b * nq + i, 0)),
                   pl.BlockSpec((qb, ak_w), lambda b, i: (b * nq + i, 0)),
                   pl.BlockSpec((IDX_HEADS * qb, IDX_DIM), lambda b, i: (b * nq + i, 0)),
                   pl.BlockSpec((qb, IDX_DIM), lambda b, i: (b * nq + i, 0)),
                   pl.BlockSpec((qb, small.shape[1]), lambda b, i: (b * nq + i, 0))],
        out_shape=[jax.ShapeDtypeStruct((t * ATT_HEADS, HEAD_DIM), jnp.bfloat16),
                   jax.ShapeDtypeStruct((t, ak_w), jnp.bfloat16),
                   jax.ShapeDtypeStruct((t * IDX_HEADS, IDX_DIM), jnp.bfloat16),
                   jax.ShapeDtypeStruct((t, IDX_DIM), jnp.bfloat16),
                   jax.ShapeDtypeStruct((t, small.shape[1]), jnp.float32)],
        compiler_params=_params(("parallel", "parallel")), name="dsa_prep")(
            proj, proj, proj, proj, small, cosf, slo, shi,
            att_q_norm.reshape(1, HEAD_DIM).astype(jnp.float32),
            att_k_norm.reshape(1, HEAD_DIM).astype(jnp.float32))


DSA_KB = 512
DSA_SB = 512
DSA_HEAD_GROUPS = 8
INT_MIN = -2 ** 31


def _dsa_kernel(iq_ref, q_ref, w_ref, ik_ref, k_ref, v_ref, o_ref,
                key_ref, keyt_ref, wb_ref, m_ref, l_ref, acc_ref, *, topk):
    qi = pl.program_id(1)
    QB, KB, SB = Q_BLOCK, DSA_KB, DSA_SB
    n_kb = (qi * QB + QB + KB - 1) // KB
    qpos = qi * QB + lax.broadcasted_iota(jnp.int32, (QB, SB), 0)
    kiota = lax.broadcasted_iota(jnp.int32, (QB, SB), 1)

    w = w_ref[...]
    for h in range(IDX_HEADS):
        wb_ref[h] = jnp.broadcast_to(w[:, SM_IW + h:SM_IW + h + 1], (QB, LANES))

    def score_block(sb, carry):
        k0 = pl.multiple_of(sb * SB, SB)
        ikb = ik_ref[pl.ds(k0, SB), :]
        hg = IDX_HEADS // DSA_HEAD_GROUPS
        lgs = [lax.dot_general(iq_ref[g * hg * QB:(g + 1) * hg * QB, :], ikb, (((1,), (1,)), ((), ())),
                               preferred_element_type=jnp.float32) for g in range(DSA_HEAD_GROUPS)]
        sc = None
        for g, lg in enumerate(lgs):
            for hh in range(hg):
                wb = wb_ref[g * hg + hh]
                term = jnp.maximum(lg[hh * QB:(hh + 1) * QB], 0.0) * jnp.concatenate([wb] * (SB // LANES), axis=1)
                sc = term if sc is None else sc + term
        bits = pltpu.bitcast(sc + 0.0, jnp.int32)
        keys = bits ^ ((bits >> 31) & 0x7FFFFFFF)
        keys = jnp.where(k0 + kiota <= qpos, keys, INT_MIN)
        key_ref[:, pl.ds(k0, SB)] = keys
        keyt_ref[pl.ds(k0, SB), :] = keys.T
        return carry

    n_sb = (qi * QB + QB + SB - 1) // SB
    lax.fori_loop(0, n_sb, score_block, 0)

    @pl.when(n_sb < n_kb * (KB // SB))
    def _():
        k0 = pl.multiple_of(n_sb * SB, SB)
        key_ref[:, pl.ds(k0, SB)] = jnp.full((QB, SB), INT_MIN, jnp.int32)
        keyt_ref[pl.ds(k0, SB), :] = jnp.full((SB, QB), INT_MIN, jnp.int32)

    def count_ge(cand):
        def blk(kb, cnt):
            k0 = pl.multiple_of(kb * KB, KB)
            parts = [jnp.where(keyt_ref[pl.ds(k0 + 8 * j, 8), :] >= cand, 1.0, 0.0) for j in range(KB // 8)]
            while len(parts) > 1:
                parts = [parts[j] + parts[j + 1] for j in range(0, len(parts), 2)]
            return cnt + parts[0]
        cnt = lax.fori_loop(0, n_kb, blk, jnp.zeros((8, LANES), jnp.float32))
        return jnp.sum(cnt, axis=0, keepdims=True)

    def to_cols(x):
        return jnp.concatenate([x] * (QB // 8), axis=0).T

    tile = lambda x: jnp.broadcast_to(x, (8, LANES))
    cnt0 = tile(count_ge(jnp.zeros((8, LANES), jnp.int32)))
    thr0 = jnp.where(cnt0 >= topk, 0, jnp.full((8, LANES), INT_MIN, jnp.int32))

    def bit_step(i, carry):
        thr, cnt_thr = carry
        cand = thr | (jnp.int32(1) << (30 - i))
        cnt = tile(count_ge(cand))
        return jnp.where(cnt >= topk, cand, thr), jnp.where(cnt >= topk, cnt, cnt_thr)

    thr, cnt_thr = lax.fori_loop(0, 31, bit_step, (thr0, cnt0))

    surplus = jnp.where((thr > INT_MIN) & (cnt_thr > topk), 1.0, 0.0)

    @pl.when(jnp.max(surplus) > 0.0)
    def _():
        need = topk - tile(count_ge(thr + 1))
        sub8 = lax.broadcasted_iota(jnp.int32, (8, LANES), 0)

        def count_tied_before(idx):
            def blk(kb, cnt):
                k0 = pl.multiple_of(kb * KB, KB)
                for j in range(KB // 8):
                    kk = keyt_ref[pl.ds(k0 + 8 * j, 8), :]
                    hit = jnp.where(kk == thr, jnp.where(k0 + 8 * j + sub8 < idx, 1.0, 0.0), 0.0)
                    cnt = cnt + hit
                return cnt
            cnt = lax.fori_loop(0, n_kb, blk, jnp.zeros((8, LANES), jnp.float32))
            return tile(jnp.sum(cnt, axis=0, keepdims=True))

        n_bits = (key_ref.shape[1] - 1).bit_length()

        def idx_step(i, last):
            cand = last | (jnp.int32(1) << (n_bits - 1 - i))
            return jnp.where(count_tied_before(cand) < need, cand, last)

        last = lax.fori_loop(0, n_bits, idx_step, jnp.zeros((8, LANES), jnp.int32))
        thr_c = jnp.concatenate([to_cols(thr)] * (KB // LANES), axis=1)
        last_c = jnp.concatenate([to_cols(last)] * (KB // LANES), axis=1)
        flag_c = jnp.concatenate([to_cols(surplus)] * (KB // LANES), axis=1)
        pos = lax.broadcasted_iota(jnp.int32, (QB, KB), 1)

        def demote(kb, carry):
            k0 = pl.multiple_of(kb * KB, KB)
            kk = key_ref[:, pl.ds(k0, KB)]
            drop = jnp.where(kk == thr_c, jnp.where(k0 + pos > last_c, flag_c, 0.0), 0.0)
            key_ref[:, pl.ds(k0, KB)] = jnp.where(drop > 0.0, thr_c - 1, kk)
            return carry

        lax.fori_loop(0, n_kb, demote, 0)

    thr = to_cols(jnp.maximum(thr, INT_MIN + 1))

    G = ATT_GROUP
    NG = KB // LANES
    m_ref[...] = jnp.full(m_ref.shape, NEG_BIG, jnp.float32)
    l_ref[...] = jnp.zeros(l_ref.shape, jnp.float32)
    acc_ref[...] = jnp.zeros(acc_ref.shape, jnp.float32)

    def lane_fold(x, op):
        parts = [x[:, g * LANES:(g + 1) * LANES] for g in range(NG)]
        while len(parts) > 1:
            parts = [op(parts[i], parts[i + 1]) for i in range(0, len(parts), 2)]
        return parts[0]

    def attn_block(kb, carry):
        k0 = pl.multiple_of(kb * KB, KB)
        bias = jnp.where(key_ref[:, pl.ds(k0, KB)] >= jnp.concatenate([thr] * NG, axis=1), 0.0, NEG_BIG)
        bias_g = jnp.concatenate([bias] * G, axis=0)
        grp = range(ATT_KV_HEADS)
        rows = [slice(hk * G * QB, (hk + 1) * G * QB) for hk in grp]
        cols = [slice(hk * HEAD_DIM, (hk + 1) * HEAD_DIM) for hk in grp]
        s = [lax.dot_general(q_ref[rows[i], :], k_ref[pl.ds(k0, KB), cols[i]], (((1,), (1,)), ((), ())),
                             preferred_element_type=jnp.float32) for i in grp]
        p, alpha, m_new = [], [], []
        for i in grp:
            si = s[i] + bias_g
            m_old = m_ref[rows[i], :]
            mi = jnp.maximum(m_old, jnp.max(lane_fold(si, jnp.maximum), axis=1, keepdims=True))
            p.append(jnp.exp2(si - jnp.concatenate([mi] * NG, axis=1)).astype(jnp.bfloat16))
            alpha.append(jnp.exp2(m_old - mi))
            m_new.append(mi)
        ones = jnp.ones((KB, LANES), jnp.bfloat16)
        pv = [jnp.dot(p[i], jnp.concatenate([v_ref[pl.ds(k0, KB), cols[i]], ones], axis=1),
                      preferred_element_type=jnp.float32) for i in grp]
        for i in grp:
            l_ref[rows[i], :] = alpha[i] * l_ref[rows[i], :] + pv[i][:, HEAD_DIM:]
            acc_ref[rows[i], :] = alpha[i] * acc_ref[rows[i], :] + pv[i][:, :HEAD_DIM]
            m_ref[rows[i], :] = m_new[i]
        return carry

    lax.fori_loop(0, n_kb, attn_block, 0)
    for h in range(ATT_HEADS):
        rows = slice(h * QB, (h + 1) * QB)
        o_ref[:, h * HEAD_DIM:(h + 1) * HEAD_DIM] = (acc_ref[rows, :] / l_ref[rows, :]).astype(o_ref.dtype)


def _dsa(iq_hm, q_hm, w_s, ik_rot, k_att, proj, batch, seq, topk):
    nq = seq // Q_BLOCK
    qb = Q_BLOCK
    kvw = ATT_KV_HEADS * HEAD_DIM
    return pl.pallas_call(
        functools.partial(_dsa_kernel, topk=topk), grid=(batch, nq),
        in_specs=[pl.BlockSpec((IDX_HEADS * qb, IDX_DIM), lambda b, i: (b * nq + i, 0)),
                  pl.BlockSpec((ATT_HEADS * qb, HEAD_DIM), lambda b, i: (b * nq + i, 0)),
                  pl.BlockSpec((qb, w_s.shape[1]), lambda b, i: (b * nq + i, 0)),
                  pl.BlockSpec((seq, IDX_DIM), lambda b, i: (b, 0)),
                  pl.BlockSpec((seq, kvw), lambda b, i: (b, 0)),
                  pl.BlockSpec((seq, kvw), lambda b, i: (b, OFF_AV // kvw))],
        out_specs=pl.BlockSpec((qb, ATT_HEADS * HEAD_DIM), lambda b, i: (b * nq + i, 0)),
        out_shape=jax.ShapeDtypeStruct((batch * seq, ATT_HEADS * HEAD_DIM), jnp.bfloat16),
        scratch_shapes=[pltpu.VMEM((qb, seq), jnp.int32),
                        pltpu.VMEM((seq, qb), jnp.int32),
                        pltpu.VMEM((IDX_HEADS, qb, LANES), jnp.float32),
                        pltpu.VMEM((ATT_HEADS * qb, LANES), jnp.float32),
                        pltpu.VMEM((ATT_HEADS * qb, LANES), jnp.float32),
                        pltpu.VMEM((ATT_HEADS * qb, HEAD_DIM), jnp.float32)],
        compiler_params=_params(("parallel", "arbitrary")), name="dsa")(
            iq_hm, q_hm, w_s, ik_rot, k_att, proj)


def _mem_block_kernel(x_ref, gin_ref, wq_ref, kv_ref, gq_ref, gk_ref, wo_ref, gout_ref, y_ref, h_ref):
    def norm(t, gain):
        return t * lax.rsqrt(jnp.mean(t * t, axis=-1, keepdims=True) + EPS) * gain

    x = x_ref[...]
    q_all = jnp.dot(norm(x, gin_ref[...]).astype(jnp.bfloat16), wq_ref[...], preferred_element_type=jnp.float32)
    hs = range(MEM_HEADS)
    cols = [slice(h * MEM_HEAD_DIM, (h + 1) * MEM_HEAD_DIM) for h in hs]
    q = [norm(q_all[:, c], gq_ref[...]) * (MEM_HEAD_DIM ** -0.5) for c in cols]
    k = [norm(kv_ref[:, c], gk_ref[...]) for c in cols]
    s = [_bdot_nt(q[h], k[h]) for h in hs]
    p = [jnp.exp(s[h] - jnp.max(s[h], axis=1, keepdims=True)) for h in hs]
    pv = [_bdot(p[h], kv_ref[:, MEM_WIDTH + h * MEM_HEAD_DIM:MEM_WIDTH + (h + 1) * MEM_HEAD_DIM]) for h in hs]
    o = [pv[h] / jnp.sum(p[h], axis=1, keepdims=True) for h in hs]
    mo = jnp.concatenate(o, axis=1).astype(jnp.bfloat16)
    y = x + jnp.dot(mo, wo_ref[...], preferred_element_type=jnp.float32)
    y_ref[...] = y
    h_ref[...] = norm(y, gout_ref[...]).astype(h_ref.dtype)


def _mem_block(x2d, mkv, norm_in, w_q, gq, gk, w_o, norm_out, batch, seq, n_mem):
    t, d = x2d.shape
    tm = _pick(seq, 256)
    nq = seq // tm
    row = pl.BlockSpec((tm, d), lambda b, i: (b * nq + i, 0))
    full = lambda a: pl.BlockSpec(a.shape, lambda b, i: (0, 0))
    vec = lambda v, n: v.reshape(1, n).astype(jnp.float32)
    args = [x2d, vec(norm_in, d), w_q, mkv, vec(gq, MEM_HEAD_DIM), vec(gk, MEM_HEAD_DIM), w_o, vec(norm_out, d)]
    specs = [row, full(args[1]), full(w_q), pl.BlockSpec((n_mem, 2 * MEM_WIDTH), lambda b, i: (b, 0)),
             full(args[4]), full(args[5]), full(w_o), full(args[7])]
    return pl.pallas_call(
        _mem_block_kernel, grid=(batch, nq), in_specs=specs, out_specs=[row, row],
        out_shape=[jax.ShapeDtypeStruct((t, d), jnp.float32), jax.ShapeDtypeStruct((t, d), jnp.bfloat16)],
        compiler_params=_params(("parallel", "parallel")), name="mem_block")(*args)


def _split_w_in(w):
    sizes = (GDN_QK, GDN_QK, GDN_V, GDN_V, GDN_HEADS, GDN_HEADS, ATT_HEADS * HEAD_DIM,
             ATT_KV_HEADS * HEAD_DIM, ATT_KV_HEADS * HEAD_DIM, IQ_W, IDX_DIM, IDX_HEADS)
    offs = np.concatenate([[0], np.cumsum(sizes)])
    wt = w.T
    seg = lambda i: wt[int(offs[i]):int(offs[i + 1]), :]
    gq, gk, gv, gz, ga, gb, aq, ak, av, iq, ik, iw = (seg(i) for i in range(12))
    d = w.shape[0]
    bf = jnp.bfloat16
    main = jnp.concatenate([s.astype(bf) for s in (iq, gq, gk, gv, gz, aq, ak, av, ik)]
                           + [jnp.zeros((N_MAIN_PAD - N_MAIN, d), bf)], axis=0)
    nsm = 2 * GDN_HEADS + IDX_HEADS
    small = jnp.concatenate([ga, gb, iw, jnp.zeros((LANES - nsm, d), w.dtype)], axis=0)
    small_hi = small.astype(bf)
    small_lo = (small - small_hi.astype(jnp.float32)).astype(bf)
    return main, small_hi, small_lo


def kernel(x, mem, norm_mix, w_in, conv_w, a_log, dt_bias, gdn_norm, att_q_norm, att_k_norm, w_out,
           norm_mem_q, norm_mem_kv, w_mem_q, w_mem_kv, mem_q_norm, mem_k_norm, w_mem_o, norm_ffn,
           w_gate, w_up, w_down):
    B, S, D = x.shape
    M = mem.shape[1]
    T = B * S
    topk = min(TOPK_MAX, S // 4)
    bf = jnp.bfloat16
    x2 = x.reshape(T, D)
    for l in range(w_in.shape[0]):
        w_main, *w_small = _split_w_in(w_in[l])
        h, small = _rmsnorm(x2, norm_mix[l], w_small)
        proj = _matmul([(h, w_main)], out_dtype=bf, tm=2048, tn=IN_TN, w_t=True, name="in_proj")
        qkv = _gdn_prep(proj, conv_w[l], S)
        gates, decay = _gate_prep(small, a_log[l], dt_bias[l])
        o_a = _gdn(qkv, proj, gates, decay, gdn_norm[l], B, S)
        q_hm, k_att, iq_hm, ik_rot, w_s = _dsa_prep(proj, small, att_q_norm[l], att_k_norm[l], B, S)
        o_b = _dsa(iq_hm, q_hm, w_s, ik_rot, k_att, proj, B, S, topk)
        wo = w_out[l].astype(bf)
        x2 = _matmul([(o_a, wo[:GDN_V]), (o_b, wo[GDN_V:])], res=x2, name="out_proj")
        hm = _rmsnorm(mem.reshape(B * M, D), norm_mem_kv[l])
        mkv = _matmul([(hm, w_mem_kv[l].astype(bf))], name="mem_kv")
        x2, hf = _mem_block(x2, mkv, norm_mem_q[l], w_mem_q[l].astype(bf), mem_q_norm[l], mem_k_norm[l],
                            w_mem_o[l].astype(bf), norm_ffn[l], B, S, M)
        mid = _gateup(hf, w_gate[l], w_up[l])
        x2 = _matmul([(mid, w_down[l].astype(bf))], res=x2, tm=512, tn=512, name="ffn_down")
    return x2.reshape(B, S, D)
```

```python
import functools
import math

import jax
import jax.numpy as jnp
import numpy as np
from jax import lax
from jax.experimental import pallas as pl
from jax.experimental.pallas import tpu as pltpu

D_MODEL = 4096
ATT_KV_HEADS = 4
IDX_HEADS = 32
TOPK_MAX = 256

EPS = 1e-6
ROPE_THETA = 500000.0
HEAD_DIM = 128
ROT_DIM = HEAD_DIM // 4
GDN_DK = 128
GDN_DV = 128
GDN_HEADS = (D_MODEL // 2) // GDN_DV
GDN_CONV = 4
GDN_CHUNK = 64
GDN_QK = GDN_HEADS * GDN_DK
GDN_V = GDN_HEADS * GDN_DV
ATT_HEADS = (D_MODEL // 2) // HEAD_DIM
ATT_GROUP = ATT_HEADS // ATT_KV_HEADS
IDX_DIM = 128
IDX_ROT = IDX_DIM // 4
Q_BLOCK = 128
MEM_HEADS = 4
MEM_HEAD_DIM = 128
MEM_WIDTH = MEM_HEADS * MEM_HEAD_DIM

LANES = 128
VMEM_LIMIT = 56 * 1024 * 1024
NEG_BIG = -1e30

IQ_W = IDX_HEADS * IDX_DIM
OFF_IQ = 0
OFF_GQ = OFF_IQ + IQ_W
OFF_GK = OFF_GQ + GDN_QK
OFF_GV = OFF_GK + GDN_QK
OFF_GZ = OFF_GV + GDN_V
OFF_AQ = OFF_GZ + GDN_V
OFF_AK = OFF_AQ + ATT_HEADS * HEAD_DIM
OFF_AV = OFF_AK + ATT_KV_HEADS * HEAD_DIM
OFF_IK = OFF_AV + ATT_KV_HEADS * HEAD_DIM
N_MAIN = OFF_IK + IDX_DIM
IN_TN = 512
N_MAIN_PAD = -(-N_MAIN // IN_TN) * IN_TN
SM_GA = 0
SM_GB = GDN_HEADS
SM_IW = 2 * GDN_HEADS
HI = lax.Precision.HIGHEST


def _params(sem):
    return pltpu.CompilerParams(dimension_semantics=sem, vmem_limit_bytes=VMEM_LIMIT)


def _pick(n, pref):
    t = min(n, pref)
    while n % t:
        t //= 2
    return t


def _sigmoid(x):
    return 1.0 / (1.0 + jnp.exp2(x * (-math.log2(math.e))))


def _rmsnorm_kernel(x_ref, g_ref, o_ref):
    x = x_ref[...]
    ms = jnp.mean(x * x, axis=-1, keepdims=True)
    o_ref[...] = (x * lax.rsqrt(ms + EPS) * g_ref[...]).astype(o_ref.dtype)


def _rmsnorm_side_kernel(x_ref, g_ref, whi_ref, wlo_ref, o_ref, s_ref):
    x = x_ref[...]
    ms = jnp.mean(x * x, axis=-1, keepdims=True)
    xn = x * lax.rsqrt(ms + EPS) * g_ref[...]
    hi = xn.astype(o_ref.dtype)
    o_ref[...] = hi
    lo = (xn - hi.astype(jnp.float32)).astype(hi.dtype)
    dot_t = lambda a, wt: lax.dot_general(a, wt, (((1,), (1,)), ((), ())), preferred_element_type=jnp.float32)
    s_ref[...] = dot_t(hi, whi_ref[...]) + dot_t(lo, whi_ref[...]) + dot_t(hi, wlo_ref[...])


def _rmsnorm(x2d, g, w_side=None, tm=256):
    m, d = x2d.shape
    tm = _pick(m, tm)
    g2 = g.reshape(1, d).astype(jnp.float32)
    xs = pl.BlockSpec((tm, d), lambda i: (i, 0))
    gs = pl.BlockSpec((1, d), lambda i: (0, 0))
    if w_side is None:
        return pl.pallas_call(
            _rmsnorm_kernel, grid=(m // tm,), in_specs=[xs, gs], out_specs=xs,
            out_shape=jax.ShapeDtypeStruct((m, d), jnp.bfloat16),
            compiler_params=_params(("parallel",)), name="rmsnorm")(x2d, g2)
    w_hi, w_lo = w_side
    ns = w_hi.shape[0]
    ws = pl.BlockSpec((ns, d), lambda i: (0, 0))
    return pl.pallas_call(
        _rmsnorm_side_kernel, grid=(m // tm,),
        in_specs=[xs, gs, ws, ws],
        out_specs=[xs, pl.BlockSpec((tm, ns), lambda i: (i, 0))],
        out_shape=[jax.ShapeDtypeStruct((m, d), jnp.bfloat16),
                   jax.ShapeDtypeStruct((m, ns), jnp.float32)],
        compiler_params=_params(("parallel",)), name="rmsnorm_side")(x2d, g2, w_hi, w_lo)


def _matmul_kernel(*refs, n_pairs, has_res, w_t):
    o_ref = refs[-1]
    rhs_dim = 1 if w_t else 0
    acc = None
    for p in range(n_pairs):
        d = lax.dot_general(refs[2 * p][...], refs[2 * p + 1][...], (((1,), (rhs_dim,)), ((), ())),
                            preferred_element_type=jnp.float32)
        acc = d if acc is None else acc + d
    if has_res:
        acc = acc + refs[2 * n_pairs][...]
    o_ref[...] = acc.astype(o_ref.dtype)


def _matmul(pairs, res=None, out_dtype=jnp.float32, tm=1024, tn=512, w_t=False, name="matmul"):
    m = pairs[0][0].shape[0]
    n = pairs[0][1].shape[0 if w_t else 1]
    tm, tn = _pick(m, tm), _pick(n, tn)
    args, specs = [], []
    for a, w in pairs:
        k = a.shape[1]
        args += [a, w]
        wspec = pl.BlockSpec((tn, k), lambda i, j: (j, 0)) if w_t else pl.BlockSpec((k, tn), lambda i, j: (0, j))
        specs += [pl.BlockSpec((tm, k), lambda i, j: (i, 0)), wspec]
    if res is not None:
        args.append(res)
        specs.append(pl.BlockSpec((tm, tn), lambda i, j: (i, j)))
    return pl.pallas_call(
        functools.partial(_matmul_kernel, n_pairs=len(pairs), has_res=res is not None, w_t=w_t),
        grid=(m // tm, n // tn), in_specs=specs,
        out_specs=pl.BlockSpec((tm, tn), lambda i, j: (i, j)),
        out_shape=jax.ShapeDtypeStruct((m, n), out_dtype),
        compiler_params=_params(("parallel", "arbitrary")), name=name)(*args)


def _gateup_kernel(a_ref, wg_ref, wu_ref, o_ref):
    a = a_ref[...]
    g = jnp.dot(a, wg_ref[...].astype(a.dtype), preferred_element_type=jnp.float32)
    u = jnp.dot(a, wu_ref[...].astype(a.dtype), preferred_element_type=jnp.float32)
    o_ref[...] = (g * _sigmoid(g) * u).astype(o_ref.dtype)


def _gateup(a, wg, wu, tm=1024, tn=256):
    m, k = a.shape
    n = wg.shape[1]
    tm, tn = _pick(m, tm), _pick(n, tn)
    ws = pl.BlockSpec((k, tn), lambda i, j: (0, j))
    return pl.pallas_call(
        _gateup_kernel, grid=(m // tm, n // tn),
        in_specs=[pl.BlockSpec((tm, k), lambda i, j: (i, 0)), ws, ws],
        out_specs=pl.BlockSpec((tm, tn), lambda i, j: (i, j)),
        out_shape=jax.ShapeDtypeStruct((m, n), jnp.bfloat16),
        compiler_params=_params(("parallel", "arbitrary")), name="ffn_gateup")(a, wg, wu)


HALO = 16


def _gdn_prep_kernel(x_ref, halo_ref, w_ref, o_ref, *, ts, tc, blocks_per_seq, nqk_blocks):
    i = pl.program_id(0)
    j = pl.program_id(1)
    first = (i % blocks_per_seq) == 0
    x = x_ref[...].astype(jnp.float32)
    halo8 = jnp.where(first, 0.0, halo_ref[HALO - 8:HALO, :].astype(jnp.float32))
    w = w_ref[...]
    row8 = lax.broadcasted_iota(jnp.int32, (8, tc), 0)
    y = x * w[GDN_CONV - 1:GDN_CONV, :]
    for d in range(1, GDN_CONV):
        xs = pltpu.roll(x, d, 0)
        head = jnp.where(row8 < d, pltpu.roll(halo8, d, 0), xs[0:8])
        xs = jnp.concatenate([head, xs[8:]], axis=0)
        y = y + xs * w[GDN_CONV - 1 - d:GDN_CONV - d, :]
    y = y * _sigmoid(y)
    is_q = j < nqk_blocks
    is_v = j >= 2 * nqk_blocks
    for h in range(tc // GDN_DK):
        yh = y[:, h * GDN_DK:(h + 1) * GDN_DK]
        rs = lax.rsqrt(jnp.sum(yh * yh, axis=-1, keepdims=True) + EPS)
        rs = rs * jnp.where(is_q, GDN_DK ** -0.5, 1.0)
        fac = jnp.where(is_v, 1.0, rs)
        o_ref[:, h * GDN_DK:(h + 1) * GDN_DK] = (yh * fac).astype(o_ref.dtype)


def _gdn_prep(proj, conv_w, seq):
    t = proj.shape[0]
    ts = _pick(seq, 512)
    tc = 512
    nconv = (2 * GDN_QK + GDN_V) // tc
    col0 = OFF_GQ // tc
    rb = ts // HALO
    kern = functools.partial(_gdn_prep_kernel, ts=ts, tc=tc, blocks_per_seq=seq // ts,
                             nqk_blocks=GDN_QK // tc)
    return pl.pallas_call(
        kern, grid=(t // ts, nconv),
        in_specs=[pl.BlockSpec((ts, tc), lambda i, j: (i, col0 + j)),
                  pl.BlockSpec((HALO, tc), lambda i, j: (jnp.maximum(i * rb - 1, 0), col0 + j)),
                  pl.BlockSpec((GDN_CONV, tc), lambda i, j: (0, j))],
        out_specs=pl.BlockSpec((ts, tc), lambda i, j: (i, j)),
        out_shape=jax.ShapeDtypeStruct((t, nconv * tc), jnp.bfloat16),
        compiler_params=_params(("parallel", "parallel")), name="gdn_prep")(proj, proj, conv_w)


def _gate_prep_kernel(s_ref, alog_ref, dtb_ref, o_ref, dec_ref, *, ts):
    C = GDN_CHUNK
    s = s_ref[...]
    lane = lax.broadcasted_iota(jnp.int32, s.shape, 1)
    xa = s + dtb_ref[...]
    softplus = jnp.maximum(xa, 0.0) + jnp.log(1.0 + jnp.exp(-jnp.abs(xa)))
    g = -jnp.exp(alog_ref[...]) * softplus
    g = jnp.where(lane < GDN_HEADS, g, 0.0)
    r = lax.broadcasted_iota(jnp.int32, (ts, ts), 0)
    c = lax.broadcasted_iota(jnp.int32, (ts, ts), 1)
    tri = jnp.where((r // C == c // C) & (c <= r), 1.0, 0.0)
    gcum = jnp.dot(tri, g, precision=HI, preferred_element_type=jnp.float32)
    beta = _sigmoid(s)
    is_beta = (lane >= SM_GB) & (lane < SM_GB + GDN_HEADS)
    o_ref[...] = jnp.where(lane < GDN_HEADS, gcum, jnp.where(is_beta, beta, 0.0))
    gcum_t = gcum.T
    incl = lax.broadcasted_iota(jnp.int32, (C, C), 1) <= lax.broadcasted_iota(jnp.int32, (C, C), 0)
    sub = lax.broadcasted_iota(jnp.int32, (8, C), 0)
    hc = [(h, ci) for h in range(GDN_HEADS) for ci in range(ts // C)]
    cols = [jnp.broadcast_to(gcum[ci * C:(ci + 1) * C, h:h + 1], (C, C)) for h, ci in hc]
    tiles = [gcum_t[8 * (h // 8):8 * (h // 8) + 8, ci * C:(ci + 1) * C] for h, ci in hc]
    rows = [jnp.sum(jnp.where(sub == h % 8, t, 0.0), axis=0, keepdims=True) for (h, ci), t in zip(hc, tiles)]
    for (h, ci), col, row in zip(hc, cols, rows):
        dec_ref[ci * C:(ci + 1) * C, h * C:(h + 1) * C] = jnp.exp(jnp.where(incl, col - row, NEG_BIG))


def _gate_prep(small, a_log, dt_bias):
    t, ns = small.shape
    ts = _pick(t, 256)
    pad = lambda v: jnp.zeros((1, ns), jnp.float32).at[0, :GDN_HEADS].set(v.astype(jnp.float32))
    vs = pl.BlockSpec((1, ns), lambda i: (0, 0))
    bs = pl.BlockSpec((ts, ns), lambda i: (i, 0))
    dw = GDN_HEADS * GDN_CHUNK
    return pl.pallas_call(
        functools.partial(_gate_prep_kernel, ts=ts), grid=(t // ts,),
        in_specs=[bs, vs, vs], out_specs=[bs, pl.BlockSpec((ts, dw), lambda i: (i, 0))],
        out_shape=[jax.ShapeDtypeStruct((t, ns), jnp.float32), jax.ShapeDtypeStruct((t, dw), jnp.float32)],
        compiler_params=_params(("parallel",)), name="gate_prep")(small, pad(a_log), pad(dt_bias))


GDN_WY_HEADS = 4
GDN_WY_ROWS = 512
GDN_WY_CG = 8
GDN_SCAN_HEADS = 16
GDN_SCAN_ROWS = 512


def _bdot(a, b):
    return jnp.dot(a.astype(jnp.bfloat16), b.astype(jnp.bfloat16), preferred_element_type=jnp.float32)


def _bdot_nt(a, b):
    return lax.dot_general(a.astype(jnp.bfloat16), b.astype(jnp.bfloat16), (((1,), (1,)), ((), ())),
                           preferred_element_type=jnp.float32)


def _head_col(gsl, idx):
    lane = lax.broadcasted_iota(jnp.int32, (1, LANES), 1)
    return jnp.sum(gsl * jnp.where(lane == idx, 1.0, 0.0), axis=1, keepdims=True)


def _gdn_wy_kernel(q_ref, k_ref, v_ref, g_ref, dec_ref, u_ref, w_ref, qd_ref, kd_ref, qk_ref, *, hb):
    hg = pl.program_id(1)
    C, CG = GDN_CHUNK, GDN_WY_CG
    r = lax.broadcasted_iota(jnp.int32, (C, C), 0)
    c = lax.broadcasted_iota(jnp.int32, (C, C), 1)
    off_diag = c != r

    def body(it, carry):
        where, q, k, v, gc, beta, decay = [], [], [], [], [], [], []
        for cc in range(CG):
            r0 = pl.multiple_of((it * CG + cc) * C, C)
            gsl = g_ref[pl.ds(r0, C), :]
            for hh in range(hb):
                sl = slice(hh * GDN_DK, (hh + 1) * GDN_DK)
                h = hg * hb + hh
                where.append((r0, hh))
                q.append(q_ref[pl.ds(r0, C), sl].astype(jnp.float32))
                k.append(k_ref[pl.ds(r0, C), sl].astype(jnp.float32))
                v.append(v_ref[pl.ds(r0, C), sl].astype(jnp.float32))
                gc.append(_head_col(gsl, h))
                beta.append(_head_col(gsl, SM_GB + h))
                decay.append(dec_ref[pl.ds(r0, C), hh * C:(hh + 1) * C])
        n = len(where)
        ch = range(n)
        kb = [k[i] * beta[i] for i in ch]
        eg = [jnp.exp(gc[i]) for i in ch]
        kk = [_bdot_nt(kb[i], k[i]) for i in ch]
        qk = [_bdot_nt(q[i], k[i]) for i in ch]
        pw = [jnp.where(off_diag, kk[i] * decay[i], 0.0) for i in ch]
        x = [jnp.concatenate([v[i] * beta[i], kb[i] * eg[i]], axis=1) for i in ch]
        n_lvl = int(math.log2(C))
        for lvl in range(n_lvl):
            px = [_bdot(pw[i], x[i]) for i in ch]
            if lvl + 1 < n_lvl:
                pw = [_bdot(pw[i], pw[i]) for i in ch]
            x = [x[i] - px[i] if lvl == 0 else x[i] + px[i] for i in ch]
        for i, (r0, hh) in enumerate(where):
            sl = slice(hh * GDN_DK, (hh + 1) * GDN_DK)
            u_ref[pl.ds(r0, C), sl] = x[i][:, :GDN_DV]
            w_ref[pl.ds(r0, C), sl] = x[i][:, GDN_DV:].astype(w_ref.dtype)
            qd_ref[pl.ds(r0, C), sl] = (q[i] * eg[i]).astype(qd_ref.dtype)
            kd_ref[pl.ds(r0, C), sl] = (k[i] * jnp.exp(gc[i][C - 1:C, :] - gc[i])).astype(kd_ref.dtype)
            qk_ref[pl.ds(r0, C), hh * C:(hh + 1) * C] = (qk[i] * decay[i]).astype(qk_ref.dtype)
        return carry

    lax.fori_loop(0, q_ref.shape[0] // (CG * C), body, 0)


def _gdn_scan_kernel(u_ref, w_ref, qd_ref, kd_ref, qk_ref, z_ref, g_ref, gain_ref, o_ref, state_ref, *, hb):
    hg = pl.program_id(1)
    C = GDN_CHUNK
    sub8 = lax.broadcasted_iota(jnp.int32, (8, LANES), 0)
    gain = gain_ref[...]

    @pl.when(pl.program_id(2) == 0)
    def _():
        state_ref[...] = jnp.zeros(state_ref.shape, state_ref.dtype)

    def body(ci, states):
        r0 = pl.multiple_of(ci * C, C)
        g_tail = g_ref[pl.ds(r0 + C - 8, 8), :]
        g_last_row = jnp.sum(jnp.where(sub8 == 7, g_tail, 0.0), axis=0, keepdims=True)
        loaded = []
        for hh in range(hb):
            sl = slice(hh * GDN_DK, (hh + 1) * GDN_DK)
            loaded.append((u_ref[pl.ds(r0, C), sl], w_ref[pl.ds(r0, C), sl], qd_ref[pl.ds(r0, C), sl],
                           kd_ref[pl.ds(r0, C), sl], qk_ref[pl.ds(r0, C), hh * C:(hh + 1) * C],
                           z_ref[pl.ds(r0, C), sl].astype(jnp.float32)))
        hs = range(hb)
        f32 = jnp.float32
        u, w, qd, kd, qk, z = zip(*loaded)
        sb = [states[i].astype(jnp.bfloat16) for i in hs]
        ws = [jnp.dot(w[i], sb[i], preferred_element_type=f32) for i in hs]
        qs = [jnp.dot(qd[i], sb[i], preferred_element_type=f32) for i in hs]
        vb = [(u[i] - ws[i]).astype(jnp.bfloat16) for i in hs]
        kv = [lax.dot_general(kd[i], vb[i], (((0,), (0,)), ((), ())), preferred_element_type=f32) for i in hs]
        qv = [jnp.dot(qk[i], vb[i], preferred_element_type=f32) for i in hs]
        a = [jnp.exp(_head_col(g_last_row, hg * hb + i)) for i in hs]
        new_states = tuple(states[i] * a[i] + kv[i] for i in hs)
        for i in hs:
            o = qs[i] + qv[i]
            on = o * lax.rsqrt(jnp.mean(o * o, axis=-1, keepdims=True) + EPS) * gain
            o_ref[pl.ds(r0, C), i * GDN_DV:(i + 1) * GDN_DV] = (on * (z[i] * _sigmoid(z[i]))).astype(o_ref.dtype)
        return new_states

    init = tuple(state_ref[hh] for hh in range(hb))
    final = lax.fori_loop(0, u_ref.shape[0] // C, body, init)
    for hh in range(hb):
        state_ref[hh] = final[hh]


def _gdn(qkv, proj, gates, decay, gdn_norm, batch, seq):
    t = batch * seq
    bf = jnp.bfloat16
    C = GDN_CHUNK
    hb = min(GDN_WY_HEADS, GDN_HEADS)
    ts = _pick(seq, GDN_WY_ROWS)
    assert GDN_HEADS % hb == 0 and ts % (GDN_WY_CG * C) == 0
    hw = hb * GDN_DK
    kq = GDN_QK // hw
    blk = lambda off: pl.BlockSpec((ts, hw), lambda i, g: (i, off + g))
    oblk = pl.BlockSpec((ts, hw), lambda i, g: (i, g))
    cblk = pl.BlockSpec((ts, hb * C), lambda i, g: (i, g))
    u, w, qd, kd, qk = pl.pallas_call(
        functools.partial(_gdn_wy_kernel, hb=hb), grid=(t // ts, GDN_HEADS // hb),
        in_specs=[blk(0), blk(kq), blk(2 * kq),
                  pl.BlockSpec((ts, gates.shape[1]), lambda i, g: (i, 0)), cblk],
        out_specs=[oblk, oblk, oblk, oblk, cblk],
        out_shape=[jax.ShapeDtypeStruct((t, GDN_V), jnp.float32), jax.ShapeDtypeStruct((t, GDN_QK), bf),
                   jax.ShapeDtypeStruct((t, GDN_QK), bf), jax.ShapeDtypeStruct((t, GDN_QK), bf),
                   jax.ShapeDtypeStruct((t, GDN_HEADS * C), bf)],
        compiler_params=_params(("parallel", "parallel")), name="gdn_wy")(qkv, qkv, qkv, gates, decay)
    hb = min(GDN_SCAN_HEADS, GDN_HEADS)
    ts = _pick(seq, GDN_SCAN_ROWS)
    assert GDN_HEADS % hb == 0
    nsb = seq // ts
    hw = hb * GDN_DK
    zoff = OFF_GZ // hw
    blk = pl.BlockSpec((ts, hw), lambda b, g, s: (b * nsb + s, g))
    return pl.pallas_call(
        functools.partial(_gdn_scan_kernel, hb=hb), grid=(batch, GDN_HEADS // hb, nsb),
        in_specs=[blk, blk, blk, blk,
                  pl.BlockSpec((ts, hb * C), lambda b, g, s: (b * nsb + s, g)),
                  pl.BlockSpec((ts, hw), lambda b, g, s: (b * nsb + s, zoff + g)),
                  pl.BlockSpec((ts, gates.shape[1]), lambda b, g, s: (b * nsb + s, 0)),
                  pl.BlockSpec((1, GDN_DV), lambda b, g, s: (0, 0))],
        out_specs=blk,
        out_shape=jax.ShapeDtypeStruct((t, GDN_V), bf),
        scratch_shapes=[pltpu.VMEM((hb, GDN_DK, GDN_DV), jnp.float32)],
        compiler_params=_params(("parallel", "parallel", "arbitrary")), name="gdn_scan")(
            u, w, qd, kd, qk, proj, gates, gdn_norm.reshape(1, GDN_DV).astype(jnp.float32))


def _dsa_prep_kernel(aq_ref, ak_ref, iq_ref, ik_ref, sm_ref, cos_ref, slo_ref, shi_ref, gq_ref, gk_ref,
                     q_out, k_out, iq_out, ik_out, w_out):
    cosf, slo, shi = cos_ref[...], slo_ref[...], shi_ref[...]
    half = ROT_DIM // 2

    xs = ([aq_ref[:, h * HEAD_DIM:(h + 1) * HEAD_DIM].astype(jnp.float32) for h in range(ATT_HEADS)]
          + [ak_ref[:, h * HEAD_DIM:(h + 1) * HEAD_DIM].astype(jnp.float32) for h in range(ATT_KV_HEADS)])
    gains = [gq_ref[...]] * ATT_HEADS + [gk_ref[...]] * ATT_KV_HEADS
    ms = [jnp.mean(x * x, axis=-1, keepdims=True) for x in xs]
    xn = [x * lax.rsqrt(m + EPS) * g for x, m, g in zip(xs, ms, gains)]
    lo = [pltpu.roll(x, LANES - half, 1) for x in xn]
    hi = [pltpu.roll(x, half, 1) for x in xn]
    ro = [x * cosf + a * slo + b * shi for x, a, b in zip(xn, lo, hi)]
    for h in range(ATT_HEADS):
        q_out[h * Q_BLOCK:(h + 1) * Q_BLOCK, :] = (ro[h] * (HEAD_DIM ** -0.5 * math.log2(math.e))).astype(q_out.dtype)
    for h in range(ATT_KV_HEADS):
        k_out[:, h * HEAD_DIM:(h + 1) * HEAD_DIM] = ro[ATT_HEADS + h].astype(k_out.dtype)
    r = lax.broadcasted_iota(jnp.int32, (IDX_DIM, IDX_DIM), 0)
    c = lax.broadcasted_iota(jnp.int32, (IDX_DIM, IDX_DIM), 1)
    swap = jnp.where(((c < half) & (r == c + half)) | ((c >= half) & (c < 2 * half) & (r == c - half)),
                     1.0, 0.0).astype(jnp.bfloat16)
    sin_signed = slo + shi
    xs = jnp.concatenate([iq_ref[:, h * IDX_DIM:(h + 1) * IDX_DIM] for h in range(IDX_HEADS)]
                         + [ik_ref[...]], axis=0)
    sw = jnp.dot(xs, swap, preferred_element_type=jnp.float32)
    nh = IDX_HEADS + 1
    rot = (xs.astype(jnp.float32).reshape(nh, Q_BLOCK, IDX_DIM) * cosf[None]
           + sw.reshape(nh, Q_BLOCK, IDX_DIM) * sin_signed[None]).reshape(nh * Q_BLOCK, IDX_DIM)
    iq_out[...] = rot[:IDX_HEADS * Q_BLOCK].astype(iq_out.dtype)
    ik_out[...] = rot[IDX_HEADS * Q_BLOCK:].astype(ik_out.dtype)
    w_out[...] = sm_ref[...] * (IDX_HEADS ** -0.5 * IDX_DIM ** -0.5)


def _rope_tables(seq):
    half = ROT_DIM // 2
    inv_freq = ROPE_THETA ** (-jnp.arange(half, dtype=jnp.float32) * 2.0 / ROT_DIM)
    ang = jnp.arange(seq, dtype=jnp.float32)[:, None] * inv_freq[None, :]
    cos, sin = jnp.cos(ang), jnp.sin(ang)
    z = jnp.zeros((seq, LANES - 2 * half), jnp.float32)
    zh = jnp.zeros((seq, half), jnp.float32)
    cosf = jnp.concatenate([cos, cos, jnp.ones_like(z)], axis=1)
    sin_lo = jnp.concatenate([-sin, zh, z], axis=1)
    sin_hi = jnp.concatenate([zh, sin, z], axis=1)
    return cosf, sin_lo, sin_hi


def _dsa_prep(proj, small, att_q_norm, att_k_norm, batch, seq):
    assert ROT_DIM == IDX_ROT
    t = batch * seq
    nq = seq // Q_BLOCK
    qb = Q_BLOCK
    cosf, slo, shi = _rope_tables(seq)
    tab = pl.BlockSpec((qb, LANES), lambda b, i: (i, 0))
    row = lambda w, off: pl.BlockSpec((qb, w), lambda b, i: (b * nq + i, off // w))
    vec = pl.BlockSpec((1, HEAD_DIM), lambda b, i: (0, 0))
    aq_w, ak_w = ATT_HEADS * HEAD_DIM, ATT_KV_HEADS * HEAD_DIM
    return pl.pallas_call(
        _dsa_prep_kernel, grid=(batch, nq),
        in_specs=[row(aq_w, OFF_AQ), row(ak_w, OFF_AK), row(IQ_W, OFF_IQ), row(IDX_DIM, OFF_IK),
                  pl.BlockSpec((qb, small.shape[1]), lambda b, i: (b * nq + i, 0)),
                  tab, tab, tab, vec, vec],
        out_specs=[pl.BlockSpec((ATT_HEADS * qb, HEAD_DIM), lambda b, i: (b * nq + i, 0)),
                   pl.BlockSpec((qb, ak_w), lambda b, i: (b * nq + i, 0)),
                   pl.BlockSpec((IDX_HEADS * qb, IDX_DIM), lambda b, i: (b * nq + i, 0)),
                   pl.BlockSpec((qb, IDX_DIM), lambda b, i: (b * nq + i, 0)),
                   pl.BlockSpec((qb, small.shape[1]), lambda b, i: (b * nq + i, 0))],
        out_shape=[jax.ShapeDtypeStruct((t * ATT_HEADS, HEAD_DIM), jnp.bfloat16),
                   jax.ShapeDtypeStruct((t, ak_w), jnp.bfloat16),
                   jax.ShapeDtypeStruct((t * IDX_HEADS, IDX_DIM), jnp.bfloat16),
                   jax.ShapeDtypeStruct((t, IDX_DIM), jnp.bfloat16),
                   jax.ShapeDtypeStruct((t, small.shape[1]), jnp.float32)],
        compiler_params=_params(("parallel", "parallel")), name="dsa_prep")(
            proj, proj, proj, proj, small, cosf, slo, shi,
            att_q_norm.reshape(1, HEAD_DIM).astype(jnp.float32),
            att_k_norm.reshape(1, HEAD_DIM).astype(jnp.float32))


DSA_KB = 512
DSA_SB = 512
DSA_HEAD_GROUPS = 8
INT_MIN = -2 ** 31


def _dsa_kernel(iq_ref, q_ref, w_ref, ik_ref, k_ref, v_ref, o_ref,
                key_ref, keyt_ref, wb_ref, m_ref, l_ref, acc_ref, *, topk):
    qi = pl.program_id(1)
    QB, KB, SB = Q_BLOCK, DSA_KB, DSA_SB
    n_kb = (qi * QB + QB + KB - 1) // KB
    qpos = qi * QB + lax.broadcasted_iota(jnp.int32, (QB, SB), 0)
    kiota = lax.broadcasted_iota(jnp.int32, (QB, SB), 1)

    w = w_ref[...]
    for h in range(IDX_HEADS):
        wb_ref[h] = jnp.broadcast_to(w[:, SM_IW + h:SM_IW + h + 1], (QB, LANES))

    def score_block(sb, carry):
        k0 = pl.multiple_of(sb * SB, SB)
        ikb = ik_ref[pl.ds(k0, SB), :]
        hg = IDX_HEADS // DSA_HEAD_GROUPS
        lgs = [lax.dot_general(iq_ref[g * hg * QB:(g + 1) * hg * QB, :], ikb, (((1,), (1,)), ((), ())),
                               preferred_element_type=jnp.float32) for g in range(DSA_HEAD_GROUPS)]
        sc = None
        for g, lg in enumerate(lgs):
            for hh in range(hg):
                wb = wb_ref[g * hg + hh]
                term = jnp.maximum(lg[hh * QB:(hh + 1) * QB], 0.0) * jnp.concatenate([wb] * (SB // LANES), axis=1)
                sc = term if sc is None else sc + term
        bits = pltpu.bitcast(sc + 0.0, jnp.int32)
        keys = bits ^ ((bits >> 31) & 0x7FFFFFFF)
        keys = jnp.where(k0 + kiota <= qpos, keys, INT_MIN)
        key_ref[:, pl.ds(k0, SB)] = keys
        keyt_ref[pl.ds(k0, SB), :] = keys.T
        return carry

    n_sb = (qi * QB + QB + SB - 1) // SB
    lax.fori_loop(0, n_sb, score_block, 0)

    @pl.when(n_sb < n_kb * (KB // SB))
    def _():
        k0 = pl.multiple_of(n_sb * SB, SB)
        key_ref[:, pl.ds(k0, SB)] = jnp.full((QB, SB), INT_MIN, jnp.int32)
        keyt_ref[pl.ds(k0, SB), :] = jnp.full((SB, QB), INT_MIN, jnp.int32)

    def count_ge(cand):
        def blk(kb, cnt):
            k0 = pl.multiple_of(kb * KB, KB)
            parts = [jnp.where(keyt_ref[pl.ds(k0 + 8 * j, 8), :] >= cand, 1.0, 0.0) for j in range(KB // 8)]
            while len(parts) > 1:
                parts = [parts[j] + parts[j + 1] for j in range(0, len(parts), 2)]
            return cnt + parts[0]
        cnt = lax.fori_loop(0, n_kb, blk, jnp.zeros((8, LANES), jnp.float32))
        return jnp.sum(cnt, axis=0, keepdims=True)

    def to_cols(x):
        return jnp.concatenate([x] * (QB // 8), axis=0).T

    tile = lambda x: jnp.broadcast_to(x, (8, LANES))
    cnt0 = tile(count_ge(jnp.zeros((8, LANES), jnp.int32)))
    thr0 = jnp.where(cnt0 >= topk, 0, jnp.full((8, LANES), INT_MIN, jnp.int32))

    def bit_step(i, carry):
        thr, cnt_thr = carry
        cand = thr | (jnp.int32(1) << (30 - i))
        cnt = tile(count_ge(cand))
        return jnp.where(cnt >= topk, cand, thr), jnp.where(cnt >= topk, cnt, cnt_thr)

    thr, cnt_thr = lax.fori_loop(0, 31, bit_step, (thr0, cnt0))

    surplus = jnp.where((thr > INT_MIN) & (cnt_thr > topk), 1.0, 0.0)

    @pl.when(jnp.max(surplus) > 0.0)
    def _():
        need = topk - tile(count_ge(thr + 1))
        sub8 = lax.broadcasted_iota(jnp.int32, (8, LANES), 0)

        def count_tied_before(idx):
            def blk(kb, cnt):
                k0 = pl.multiple_of(kb * KB, KB)
                for j in range(KB // 8):
                    kk = keyt_ref[pl.ds(k0 + 8 * j, 8), :]
                    hit = jnp.where(kk == thr, jnp.where(k0 + 8 * j + sub8 < idx, 1.0, 0.0), 0.0)
                    cnt = cnt + hit
                return cnt
            cnt = lax.fori_loop(0, n_kb, blk, jnp.zeros((8, LANES), jnp.float32))
            return tile(jnp.sum(cnt, axis=0, keepdims=True))

        n_bits = (key_ref.shape[1] - 1).bit_length()

        def idx_step(i, last):
            cand = last | (jnp.int32(1) << (n_bits - 1 - i))
            return jnp.where(count_tied_before(cand) < need, cand, last)

        last = lax.fori_loop(0, n_bits, idx_step, jnp.zeros((8, LANES), jnp.int32))
        thr_c = jnp.concatenate([to_cols(thr)] * (KB // LANES), axis=1)
        last_c = jnp.concatenate([to_cols(last)] * (KB // LANES), axis=1)
        flag_c = jnp.concatenate([to_cols(surplus)] * (KB // LANES), axis=1)
        pos = lax.broadcasted_iota(jnp.int32, (QB, KB), 1)

        def demote(kb, carry):
            k0 = pl.multiple_of(kb * KB, KB)
            kk = key_ref[:, pl.ds(k0, KB)]
            drop = jnp.where(kk == thr_c, jnp.where(k0 + pos > last_c, flag_c, 0.0), 0.0)
            key_ref[:, pl.ds(k0, KB)] = jnp.where(drop > 0.0, thr_c - 1, kk)
            return carry

        lax.fori_loop(0, n_kb, demote, 0)

    thr = to_cols(jnp.maximum(thr, INT_MIN + 1))

    G = ATT_GROUP
    NG = KB // LANES
    m_ref[...] = jnp.full(m_ref.shape, NEG_BIG, jnp.float32)
    l_ref[...] = jnp.zeros(l_ref.shape, jnp.float32)
    acc_ref[...] = jnp.zeros(acc_ref.shape, jnp.float32)

    def lane_fold(x, op):
        parts = [x[:, g * LANES:(g + 1) * LANES] for g in range(NG)]
        while len(parts) > 1:
            parts = [op(parts[i], parts[i + 1]) for i in range(0, len(parts), 2)]
        return parts[0]

    def attn_block(kb, carry):
        k0 = pl.multiple_of(kb * KB, KB)
        bias = jnp.where(key_ref[:, pl.ds(k0, KB)] >= jnp.concatenate([thr] * NG, axis=1), 0.0, NEG_BIG)
        bias_g = jnp.concatenate([bias] * G, axis=0)
        grp = range(ATT_KV_HEADS)
        rows = [slice(hk * G * QB, (hk + 1) * G * QB) for hk in grp]
        cols = [slice(hk * HEAD_DIM, (hk + 1) * HEAD_DIM) for hk in grp]
        s = [lax.dot_general(q_ref[rows[i], :], k_ref[pl.ds(k0, KB), cols[i]], (((1,), (1,)), ((), ())),
                             preferred_element_type=jnp.float32) for i in grp]
        p, alpha, m_new = [], [], []
        for i in grp:
            si = s[i] + bias_g
            m_old = m_ref[rows[i], :]
            mi = jnp.maximum(m_old, jnp.max(lane_fold(si, jnp.maximum), axis=1, keepdims=True))
            p.append(jnp.exp2(si - jnp.concatenate([mi] * NG, axis=1)).astype(jnp.bfloat16))
            alpha.append(jnp.exp2(m_old - mi))
            m_new.append(mi)
        ones = jnp.ones((KB, LANES), jnp.bfloat16)
        pv = [jnp.dot(p[i], jnp.concatenate([v_ref[pl.ds(k0, KB), cols[i]], ones], axis=1),
                      preferred_element_type=jnp.float32) for i in grp]
        for i in grp:
            l_ref[rows[i], :] = alpha[i] * l_ref[rows[i], :] + pv[i][:, HEAD_DIM:]
            acc_ref[rows[i], :] = alpha[i] * acc_ref[rows[i], :] + pv[i][:, :HEAD_DIM]
            m_ref[rows[i], :] = m_new[i]
        return carry

    lax.fori_loop(0, n_kb, attn_block, 0)
    for h in range(ATT_HEADS):
        rows = slice(h * QB, (h + 1) * QB)
        o_ref[:, h * HEAD_DIM:(h + 1) * HEAD_DIM] = (acc_ref[rows, :] / l_ref[rows, :]).astype(o_ref.dtype)


def _dsa(iq_hm, q_hm, w_s, ik_rot, k_att, proj, batch, seq, topk):
    nq = seq // Q_BLOCK
    qb = Q_BLOCK
    kvw = ATT_KV_HEADS * HEAD_DIM
    return pl.pallas_call(
        functools.partial(_dsa_kernel, topk=topk), grid=(batch, nq),
        in_specs=[pl.BlockSpec((IDX_HEADS * qb, IDX_DIM), lambda b, i: (b * nq + i, 0)),
                  pl.BlockSpec((ATT_HEADS * qb, HEAD_DIM), lambda b, i: (b * nq + i, 0)),
                  pl.BlockSpec((qb, w_s.shape[1]), lambda b, i: (b * nq + i, 0)),
                  pl.BlockSpec((seq, IDX_DIM), lambda b, i: (b, 0)),
                  pl.BlockSpec((seq, kvw), lambda b, i: (b, 0)),
                  pl.BlockSpec((seq, kvw), lambda b, i: (b, OFF_AV // kvw))],
        out_specs=pl.BlockSpec((qb, ATT_HEADS * HEAD_DIM), lambda b, i: (b * nq + i, 0)),
        out_shape=jax.ShapeDtypeStruct((batch * seq, ATT_HEADS * HEAD_DIM), jnp.bfloat16),
        scratch_shapes=[pltpu.VMEM((qb, seq), jnp.int32),
                        pltpu.VMEM((seq, qb), jnp.int32),
                        pltpu.VMEM((IDX_HEADS, qb, LANES), jnp.float32),
                        pltpu.VMEM((ATT_HEADS * qb, LANES), jnp.float32),
                        pltpu.VMEM((ATT_HEADS * qb, LANES), jnp.float32),
                        pltpu.VMEM((ATT_HEADS * qb, HEAD_DIM), jnp.float32)],
        compiler_params=_params(("parallel", "arbitrary")), name="dsa")(
            iq_hm, q_hm, w_s, ik_rot, k_att, proj)


def _mem_block_kernel(x_ref, gin_ref, wq_ref, kv_ref, gq_ref, gk_ref, wo_ref, gout_ref, y_ref, h_ref):
    def norm(t, gain):
        return t * lax.rsqrt(jnp.mean(t * t, axis=-1, keepdims=True) + EPS) * gain

    x = x_ref[...]
    q_all = jnp.dot(norm(x, gin_ref[...]).astype(jnp.bfloat16), wq_ref[...], preferred_element_type=jnp.float32)
    hs = range(MEM_HEADS)
    cols = [slice(h * MEM_HEAD_DIM, (h + 1) * MEM_HEAD_DIM) for h in hs]
    q = [norm(q_all[:, c], gq_ref[...]) * (MEM_HEAD_DIM ** -0.5) for c in cols]
    k = [norm(kv_ref[:, c], gk_ref[...]) for c in cols]
    s = [_bdot_nt(q[h], k[h]) for h in hs]
    p = [jnp.exp(s[h] - jnp.max(s[h], axis=1, keepdims=True)) for h in hs]
    pv = [_bdot(p[h], kv_ref[:, MEM_WIDTH + h * MEM_HEAD_DIM:MEM_WIDTH + (h + 1) * MEM_HEAD_DIM]) for h in hs]
    o = [pv[h] / jnp.sum(p[h], axis=1, keepdims=True) for h in hs]
    mo = jnp.concatenate(o, axis=1).astype(jnp.bfloat16)
    y = x + jnp.dot(mo, wo_ref[...], preferred_element_type=jnp.float32)
    y_ref[...] = y
    h_ref[...] = norm(y, gout_ref[...]).astype(h_ref.dtype)


def _mem_block(x2d, mkv, norm_in, w_q, gq, gk, w_o, norm_out, batch, seq, n_mem):
    t, d = x2d.shape
    tm = _pick(seq, 256)
    nq = seq // tm
    row = pl.BlockSpec((tm, d), lambda b, i: (b * nq + i, 0))
    full = lambda a: pl.BlockSpec(a.shape, lambda b, i: (0, 0))
    vec = lambda v, n: v.reshape(1, n).astype(jnp.float32)
    args = [x2d, vec(norm_in, d), w_q, mkv, vec(gq, MEM_HEAD_DIM), vec(gk, MEM_HEAD_DIM), w_o, vec(norm_out, d)]
    specs = [row, full(args[1]), full(w_q), pl.BlockSpec((n_mem, 2 * MEM_WIDTH), lambda b, i: (b, 0)),
             full(args[4]), full(args[5]), full(w_o), full(args[7])]
    return pl.pallas_call(
        _mem_block_kernel, grid=(batch, nq), in_specs=specs, out_specs=[row, row],
        out_shape=[jax.ShapeDtypeStruct((t, d), jnp.float32), jax.ShapeDtypeStruct((t, d), jnp.bfloat16)],
        compiler_params=_params(("parallel", "parallel")), name="mem_block")(*args)


def _split_w_in(w):
    sizes = (GDN_QK, GDN_QK, GDN_V, GDN_V, GDN_HEADS, GDN_HEADS, ATT_HEADS * HEAD_DIM,
             ATT_KV_HEADS * HEAD_DIM, ATT_KV_HEADS * HEAD_DIM, IQ_W, IDX_DIM, IDX_HEADS)
    offs = np.concatenate([[0], np.cumsum(sizes)])
    wt = w.T
    seg = lambda i: wt[int(offs[i]):int(offs[i + 1]), :]
    gq, gk, gv, gz, ga, gb, aq, ak, av, iq, ik, iw = (seg(i) for i in range(12))
    d = w.shape[0]
    bf = jnp.bfloat16
    main = jnp.concatenate([s.astype(bf) for s in (iq, gq, gk, gv, gz, aq, ak, av, ik)]
                           + [jnp.zeros((N_MAIN_PAD - N_MAIN, d), bf)], axis=0)
    nsm = 2 * GDN_HEADS + IDX_HEADS
    small = jnp.concatenate([ga, gb, iw, jnp.zeros((LANES - nsm, d), w.dtype)], axis=0)
    small_hi = small.astype(bf)
    small_lo = (small - small_hi.astype(jnp.float32)).astype(bf)
    return main, small_hi, small_lo


def kernel(x, mem, norm_mix, w_in, conv_w, a_log, dt_bias, gdn_norm, att_q_norm, att_k_norm, w_out,
           norm_mem_q, norm_mem_kv, w_mem_q, w_mem_kv, mem_q_norm, mem_k_norm, w_mem_o, norm_ffn,
           w_gate, w_up, w_down):
    B, S, D = x.shape
    M = mem.shape[1]
    T = B * S
    topk = min(TOPK_MAX, S // 4)
    bf = jnp.bfloat16
    x2 = x.reshape(T, D)
    for l in range(w_in.shape[0]):
        w_main, *w_small = _split_w_in(w_in[l])
        h, small = _rmsnorm(x2, norm_mix[l], w_small)
        proj = _matmul([(h, w_main)], out_dtype=bf, tm=2048, tn=IN_TN, w_t=True, name="in_proj")
        qkv = _gdn_prep(proj, conv_w[l], S)
        gates, decay = _gate_prep(small, a_log[l], dt_bias[l])
        o_a = _gdn(qkv, proj, gates, decay, gdn_norm[l], B, S)
        q_hm, k_att, iq_hm, ik_rot, w_s = _dsa_prep(proj, small, att_q_norm[l], att_k_norm[l], B, S)
        o_b = _dsa(iq_hm, q_hm, w_s, ik_rot, k_att, proj, B, S, topk)
        wo = w_out[l].astype(bf)
        x2 = _matmul([(o_a, wo[:GDN_V]), (o_b, wo[GDN_V:])], res=x2, name="out_proj")
        hm = _rmsnorm(mem.reshape(B * M, D), norm_mem_kv[l])
        mkv = _matmul([(hm, w_mem_kv[l].astype(bf))], name="mem_kv")
        x2, hf = _mem_block(x2, mkv, norm_mem_q[l], w_mem_q[l].astype(bf), mem_q_norm[l], mem_k_norm[l],
                            w_mem_o[l].astype(bf), norm_ffn[l], B, S, M)
        mid = _gateup(hf, w_gate[l], w_up[l])
        x2 = _matmul([(mid, w_down[l].astype(bf))], res=x2, tm=512, tn=512, name="ffn_down")
    return x2.reshape(B, S, D)
```

```python
import functools
import math

import jax
import jax.numpy as jnp
import numpy as np
from jax import lax
from jax.experimental import pallas as pl
from jax.experimental.pallas import tpu as pltpu

D_MODEL = 4096
ATT_KV_HEADS = 4
IDX_HEADS = 32
TOPK_MAX = 256

EPS = 1e-6
ROPE_THETA = 500000.0
HEAD_DIM = 128
ROT_DIM = HEAD_DIM // 4
GDN_DK = 128
GDN_DV = 128
GDN_HEADS = (D_MODEL // 2) // GDN_DV
GDN_CONV = 4
GDN_CHUNK = 64
GDN_QK = GDN_HEADS * GDN_DK
GDN_V = GDN_HEADS * GDN_DV
ATT_HEADS = (D_MODEL // 2) // HEAD_DIM
ATT_GROUP = ATT_HEADS // ATT_KV_HEADS
IDX_DIM = 128
IDX_ROT = IDX_DIM // 4
Q_BLOCK = 128
MEM_HEADS = 4
MEM_HEAD_DIM = 128
MEM_WIDTH = MEM_HEADS * MEM_HEAD_DIM

LANES = 128
VMEM_LIMIT = 56 * 1024 * 1024
NEG_BIG = -1e30

IQ_W = IDX_HEADS * IDX_DIM
OFF_IQ = 0
OFF_GQ = OFF_IQ + IQ_W
OFF_GK = OFF_GQ + GDN_QK
OFF_GV = OFF_GK + GDN_QK
OFF_GZ = OFF_GV + GDN_V
OFF_AQ = OFF_GZ + GDN_V
OFF_AK = OFF_AQ + ATT_HEADS * HEAD_DIM
OFF_AV = OFF_AK + ATT_KV_HEADS * HEAD_DIM
OFF_IK = OFF_AV + ATT_KV_HEADS * HEAD_DIM
N_MAIN = OFF_IK + IDX_DIM
IN_TN = 512
N_MAIN_PAD = -(-N_MAIN // IN_TN) * IN_TN
SM_GA = 0
SM_GB = GDN_HEADS
SM_IW = 2 * GDN_HEADS
HI = lax.Precision.HIGHEST


def _params(sem):
    return pltpu.CompilerParams(dimension_semantics=sem, vmem_limit_bytes=VMEM_LIMIT)


def _pick(n, pref):
    t = min(n, pref)
    while n % t:
        t //= 2
    return t


def _sigmoid(x):
    return 1.0 / (1.0 + jnp.exp2(x * (-math.log2(math.e))))


def _rmsnorm_kernel(x_ref, g_ref, o_ref):
    x = x_ref[...]
    ms = jnp.mean(x * x, axis=-1, keepdims=True)
    o_ref[...] = (x * lax.rsqrt(ms + EPS) * g_ref[...]).astype(o_ref.dtype)


def _rmsnorm_side_kernel(x_ref, g_ref, whi_ref, wlo_ref, o_ref, s_ref):
    x = x_ref[...]
    ms = jnp.mean(x * x, axis=-1, keepdims=True)
    xn = x * lax.rsqrt(ms + EPS) * g_ref[...]
    hi = xn.astype(o_ref.dtype)
    o_ref[...] = hi
    lo = (xn - hi.astype(jnp.float32)).astype(hi.dtype)
    dot_t = lambda a, wt: lax.dot_general(a, wt, (((1,), (1,)), ((), ())), preferred_element_type=jnp.float32)
    s_ref[...] = dot_t(hi, whi_ref[...]) + dot_t(lo, whi_ref[...]) + dot_t(hi, wlo_ref[...])


def _rmsnorm(x2d, g, w_side=None, tm=256):
    m, d = x2d.shape
    tm = _pick(m, tm)
    g2 = g.reshape(1, d).astype(jnp.float32)
    xs = pl.BlockSpec((tm, d), lambda i: (i, 0))
    gs = pl.BlockSpec((1, d), lambda i: (0, 0))
    if w_side is None:
        return pl.pallas_call(
            _rmsnorm_kernel, grid=(m // tm,), in_specs=[xs, gs], out_specs=xs,
            out_shape=jax.ShapeDtypeStruct((m, d), jnp.bfloat16),
            compiler_params=_params(("parallel",)), name="rmsnorm")(x2d, g2)
    w_hi, w_lo = w_side
    ns = w_hi.shape[0]
    ws = pl.BlockSpec((ns, d), lambda i: (0, 0))
    return pl.pallas_call(
        _rmsnorm_side_kernel, grid=(m // tm,),
        in_specs=[xs, gs, ws, ws],
        out_specs=[xs, pl.BlockSpec((tm, ns), lambda i: (i, 0))],
        out_shape=[jax.ShapeDtypeStruct((m, d), jnp.bfloat16),
                   jax.ShapeDtypeStruct((m, ns), jnp.float32)],
        compiler_params=_params(("parallel",)), name="rmsnorm_side")(x2d, g2, w_hi, w_lo)


def _matmul_kernel(*refs, n_pairs, has_res, w_t):
    o_ref = refs[-1]
    rhs_dim = 1 if w_t else 0
    acc = None
    for p in range(n_pairs):
        d = lax.dot_general(refs[2 * p][...], refs[2 * p + 1][...], (((1,), (rhs_dim,)), ((), ())),
                            preferred_element_type=jnp.float32)
        acc = d if acc is None else acc + d
    if has_res:
        acc = acc + refs[2 * n_pairs][...]
    o_ref[...] = acc.astype(o_ref.dtype)


def _matmul(pairs, res=None, out_dtype=jnp.float32, tm=1024, tn=512, w_t=False, name="matmul"):
    m = pairs[0][0].shape[0]
    n = pairs[0][1].shape[0 if w_t else 1]
    tm, tn = _pick(m, tm), _pick(n, tn)
    args, specs = [], []
    for a, w in pairs:
        k = a.shape[1]
        args += [a, w]
        wspec = pl.BlockSpec((tn, k), lambda i, j: (j, 0)) if w_t else pl.BlockSpec((k, tn), lambda i, j: (0, j))
        specs += [pl.BlockSpec((tm, k), lambda i, j: (i, 0)), wspec]
    if res is not None:
        args.append(res)
        specs.append(pl.BlockSpec((tm, tn), lambda i, j: (i, j)))
    return pl.pallas_call(
        functools.partial(_matmul_kernel, n_pairs=len(pairs), has_res=res is not None, w_t=w_t),
        grid=(m // tm, n // tn), in_specs=specs,
        out_specs=pl.BlockSpec((tm, tn), lambda i, j: (i, j)),
        out_shape=jax.ShapeDtypeStruct((m, n), out_dtype),
        compiler_params=_params(("parallel", "arbitrary")), name=name)(*args)


def _gateup_kernel(a_ref, wg_ref, wu_ref, o_ref):
    a = a_ref[...]
    g = jnp.dot(a, wg_ref[...].astype(a.dtype), preferred_element_type=jnp.float32)
    u = jnp.dot(a, wu_ref[...].astype(a.dtype), preferred_element_type=jnp.float32)
    o_ref[...] = (g * _sigmoid(g) * u).astype(o_ref.dtype)


def _gateup(a, wg, wu, tm=1024, tn=256):
    m, k = a.shape
    n = wg.shape[1]
    tm, tn = _pick(m, tm), _pick(n, tn)
    ws = pl.BlockSpec((k, tn), lambda i, j: (0, j))
    return pl.pallas_call(
        _gateup_kernel, grid=(m // tm, n // tn),
        in_specs=[pl.BlockSpec((tm, k), lambda i, j: (i, 0)), ws, ws],
        out_specs=pl.BlockSpec((tm, tn), lambda i, j: (i, j)),
        out_shape=jax.ShapeDtypeStruct((m, n), jnp.bfloat16),
        compiler_params=_params(("parallel", "arbitrary")), name="ffn_gateup")(a, wg, wu)


HALO = 16
CONV_TILE = 128


def _gdn_prep_kernel(x_ref, halo_ref, w_ref, o_ref, *, ts, tc, blocks_per_seq, nqk_blocks):
    i = pl.program_id(0)
    j = pl.program_id(1)
    first = (i % blocks_per_seq) == 0
    TR = CONV_TILE
    xb = x_ref[...]
    halo = jnp.where(first, jnp.zeros_like(halo_ref[...]), halo_ref[...])
    xe = jnp.concatenate([jnp.zeros((TR - HALO, tc), xb.dtype), halo, xb], axis=0)
    nd = GDN_CONV - 1
    r = lax.broadcasted_iota(jnp.int32, (nd * TR, 2 * TR), 0)
    c = lax.broadcasted_iota(jnp.int32, (nd * TR, 2 * TR), 1)
    sel = jnp.where(c == TR + r % TR - (r // TR + 1), 1.0, 0.0).astype(xb.dtype)
    w = w_ref[...]
    tiles = []
    for t in range(ts // TR):
        sh = jnp.dot(sel, xe[t * TR:(t + 2) * TR], preferred_element_type=jnp.float32)
        yt = xb[t * TR:(t + 1) * TR].astype(jnp.float32) * w[nd:nd + 1, :]
        for d in range(1, GDN_CONV):
            yt = yt + sh[(d - 1) * TR:d * TR] * w[nd - d:nd - d + 1, :]
        tiles.append(yt)
    y = jnp.concatenate(tiles, axis=0)
    y = y * _sigmoid(y)
    is_q = j < nqk_blocks
    is_v = j >= 2 * nqk_blocks

    for h in range(tc // GDN_DK):
        yh = y[:, h * GDN_DK:(h + 1) * GDN_DK]
        rs = lax.rsqrt(jnp.sum(yh * yh, axis=-1, keepdims=True) + EPS)
        rs = rs * jnp.where(is_q, GDN_DK ** -0.5, 1.0)
        fac = jnp.where(is_v, 1.0, rs)
        o_ref[:, h * GDN_DK:(h + 1) * GDN_DK] = (yh * fac).astype(o_ref.dtype)


def _gdn_prep(proj, conv_w, seq):
    t = proj.shape[0]
    ts = _pick(seq, 512)
    tc = 512
    nconv = (2 * GDN_QK + GDN_V) // tc
    col0 = OFF_GQ // tc
    rb = ts // HALO
    kern = functools.partial(_gdn_prep_kernel, ts=ts, tc=tc, blocks_per_seq=seq // ts,
                             nqk_blocks=GDN_QK // tc)
    return pl.pallas_call(
        kern, grid=(t // ts, nconv),
        in_specs=[pl.BlockSpec((ts, tc), lambda i, j: (i, col0 + j)),
                  pl.BlockSpec((HALO, tc), lambda i, j: (jnp.maximum(i * rb - 1, 0), col0 + j)),
                  pl.BlockSpec((GDN_CONV, tc), lambda i, j: (0, j))],
        out_specs=pl.BlockSpec((ts, tc), lambda i, j: (i, j)),
        out_shape=jax.ShapeDtypeStruct((t, nconv * tc), jnp.bfloat16),
        compiler_params=_params(("parallel", "parallel")), name="gdn_prep")(proj, proj, conv_w)


def _gate_prep_kernel(s_ref, alog_ref, dtb_ref, o_ref, dec_ref, *, ts):
    C = GDN_CHUNK
    s = s_ref[...]
    lane = lax.broadcasted_iota(jnp.int32, s.shape, 1)
    xa = s + dtb_ref[...]
    softplus = jnp.maximum(xa, 0.0) + jnp.log(1.0 + jnp.exp(-jnp.abs(xa)))
    g = -jnp.exp(alog_ref[...]) * softplus
    g = jnp.where(lane < GDN_HEADS, g, 0.0)
    r = lax.broadcasted_iota(jnp.int32, (ts, ts), 0)
    c = lax.broadcasted_iota(jnp.int32, (ts, ts), 1)
    tri = jnp.where((r // C == c // C) & (c <= r), 1.0, 0.0)
    gcum = jnp.dot(tri, g, precision=HI, preferred_element_type=jnp.float32)
    beta = _sigmoid(s)
    is_beta = (lane >= SM_GB) & (lane < SM_GB + GDN_HEADS)
    o_ref[...] = jnp.where(lane < GDN_HEADS, gcum, jnp.where(is_beta, beta, 0.0))
    gcum_t = gcum.T
    incl = lax.broadcasted_iota(jnp.int32, (C, C), 1) <= lax.broadcasted_iota(jnp.int32, (C, C), 0)
    sub = lax.broadcasted_iota(jnp.int32, (8, C), 0)
    hc = [(h, ci) for h in range(GDN_HEADS) for ci in range(ts // C)]
    cols = [jnp.broadcast_to(gcum[ci * C:(ci + 1) * C, h:h + 1], (C, C)) for h, ci in hc]
    tiles = [gcum_t[8 * (h // 8):8 * (h // 8) + 8, ci * C:(ci + 1) * C] for h, ci in hc]
    rows = [jnp.sum(jnp.where(sub == h % 8, t, 0.0), axis=0, keepdims=True) for (h, ci), t in zip(hc, tiles)]
    for (h, ci), col, row in zip(hc, cols, rows):
        dec_ref[ci * C:(ci + 1) * C, h * C:(h + 1) * C] = jnp.exp(jnp.where(incl, col - row, NEG_BIG))


def _gate_prep(small, a_log, dt_bias):
    t, ns = small.shape
    ts = _pick(t, 256)
    pad = lambda v: jnp.zeros((1, ns), jnp.float32).at[0, :GDN_HEADS].set(v.astype(jnp.float32))
    vs = pl.BlockSpec((1, ns), lambda i: (0, 0))
    bs = pl.BlockSpec((ts, ns), lambda i: (i, 0))
    dw = GDN_HEADS * GDN_CHUNK
    return pl.pallas_call(
        functools.partial(_gate_prep_kernel, ts=ts), grid=(t // ts,),
        in_specs=[bs, vs, vs], out_specs=[bs, pl.BlockSpec((ts, dw), lambda i: (i, 0))],
        out_shape=[jax.ShapeDtypeStruct((t, ns), jnp.float32), jax.ShapeDtypeStruct((t, dw), jnp.float32)],
        compiler_params=_params(("parallel",)), name="gate_prep")(small, pad(a_log), pad(dt_bias))


GDN_WY_HEADS = 4
GDN_WY_ROWS = 512
GDN_WY_CG = 8
GDN_SCAN_HEADS = 16
GDN_SCAN_ROWS = 512


def _bdot(a, b):
    return jnp.dot(a.astype(jnp.bfloat16), b.astype(jnp.bfloat16), preferred_element_type=jnp.float32)


def _bdot_nt(a, b):
    return lax.dot_general(a.astype(jnp.bfloat16), b.astype(jnp.bfloat16), (((1,), (1,)), ((), ())),
                           preferred_element_type=jnp.float32)


def _head_col(gsl, idx):
    lane = lax.broadcasted_iota(jnp.int32, (1, LANES), 1)
    return jnp.sum(gsl * jnp.where(lane == idx, 1.0, 0.0), axis=1, keepdims=True)


def _gdn_wy_kernel(q_ref, k_ref, v_ref, g_ref, dec_ref, u_ref, w_ref, qd_ref, kd_ref, qk_ref, *, hb):
    hg = pl.program_id(1)
    C, CG = GDN_CHUNK, GDN_WY_CG
    r = lax.broadcasted_iota(jnp.int32, (C, C), 0)
    c = lax.broadcasted_iota(jnp.int32, (C, C), 1)
    off_diag = c != r

    def body(it, carry):
        where, q, k, v, gc, beta, decay = [], [], [], [], [], [], []
        for cc in range(CG):
            r0 = pl.multiple_of((it * CG + cc) * C, C)
            gsl = g_ref[pl.ds(r0, C), :]
            for hh in range(hb):
                sl = slice(hh * GDN_DK, (hh + 1) * GDN_DK)
                h = hg * hb + hh
                where.append((r0, hh))
                q.append(q_ref[pl.ds(r0, C), sl].astype(jnp.float32))
                k.append(k_ref[pl.ds(r0, C), sl].astype(jnp.float32))
                v.append(v_ref[pl.ds(r0, C), sl].astype(jnp.float32))
                gc.append(_head_col(gsl, h))
                beta.append(_head_col(gsl, SM_GB + h))
                decay.append(dec_ref[pl.ds(r0, C), hh * C:(hh + 1) * C])
        n = len(where)
        ch = range(n)
        kb = [k[i] * beta[i] for i in ch]
        eg = [jnp.exp(gc[i]) for i in ch]
        kk = [_bdot_nt(kb[i], k[i]) for i in ch]
        qk = [_bdot_nt(q[i], k[i]) for i in ch]
        pw = [jnp.where(off_diag, kk[i] * decay[i], 0.0) for i in ch]
        x = [jnp.concatenate([v[i] * beta[i], kb[i] * eg[i]], axis=1) for i in ch]
        n_lvl = int(math.log2(C))
        for lvl in range(n_lvl):
            px = [_bdot(pw[i], x[i]) for i in ch]
            if lvl + 1 < n_lvl:
                pw = [_bdot(pw[i], pw[i]) for i in ch]
            x = [x[i] - px[i] if lvl == 0 else x[i] + px[i] for i in ch]
        for i, (r0, hh) in enumerate(where):
            sl = slice(hh * GDN_DK, (hh + 1) * GDN_DK)
            u_ref[pl.ds(r0, C), sl] = x[i][:, :GDN_DV]
            w_ref[pl.ds(r0, C), sl] = x[i][:, GDN_DV:].astype(w_ref.dtype)
            qd_ref[pl.ds(r0, C), sl] = (q[i] * eg[i]).astype(qd_ref.dtype)
            kd_ref[pl.ds(r0, C), sl] = (k[i] * jnp.exp(gc[i][C - 1:C, :] - gc[i])).astype(kd_ref.dtype)
            qk_ref[pl.ds(r0, C), hh * C:(hh + 1) * C] = (qk[i] * decay[i]).astype(qk_ref.dtype)
        return carry

    lax.fori_loop(0, q_ref.shape[0] // (CG * C), body, 0)


def _gdn_scan_kernel(u_ref, w_ref, qd_ref, kd_ref, qk_ref, z_ref, g_ref, gain_ref, o_ref, state_ref, *, hb):
    hg = pl.program_id(1)
    C = GDN_CHUNK
    sub8 = lax.broadcasted_iota(jnp.int32, (8, LANES), 0)
    gain = gain_ref[...]

    @pl.when(pl.program_id(2) == 0)
    def _():
        state_ref[...] = jnp.zeros(state_ref.shape, state_ref.dtype)

    def body(ci, states):
        r0 = pl.multiple_of(ci * C, C)
        g_tail = g_ref[pl.ds(r0 + C - 8, 8), :]
        g_last_row = jnp.sum(jnp.where(sub8 == 7, g_tail, 0.0), axis=0, keepdims=True)
        loaded = []
        for hh in range(hb):
            sl = slice(hh * GDN_DK, (hh + 1) * GDN_DK)
            loaded.append((u_ref[pl.ds(r0, C), sl], w_ref[pl.ds(r0, C), sl], qd_ref[pl.ds(r0, C), sl],
                           kd_ref[pl.ds(r0, C), sl], qk_ref[pl.ds(r0, C), hh * C:(hh + 1) * C],
                           z_ref[pl.ds(r0, C), sl].astype(jnp.float32)))
        hs = range(hb)
        f32 = jnp.float32
        u, w, qd, kd, qk, z = zip(*loaded)
        sb = [states[i].astype(jnp.bfloat16) for i in hs]
        ws = [jnp.dot(w[i], sb[i], preferred_element_type=f32) for i in hs]
        qs = [jnp.dot(qd[i], sb[i], preferred_element_type=f32) for i in hs]
        vb = [(u[i] - ws[i]).astype(jnp.bfloat16) for i in hs]
        kv = [lax.dot_general(kd[i], vb[i], (((0,), (0,)), ((), ())), preferred_element_type=f32) for i in hs]
        qv = [jnp.dot(qk[i], vb[i], preferred_element_type=f32) for i in hs]
        a = [jnp.exp(_head_col(g_last_row, hg * hb + i)) for i in hs]
        new_states = tuple(states[i] * a[i] + kv[i] for i in hs)
        for i in hs:
            o = qs[i] + qv[i]
            on = o * lax.rsqrt(jnp.mean(o * o, axis=-1, keepdims=True) + EPS) * gain
            o_ref[pl.ds(r0, C), i * GDN_DV:(i + 1) * GDN_DV] = (on * (z[i] * _sigmoid(z[i]))).astype(o_ref.dtype)
        return new_states

    init = tuple(state_ref[hh] for hh in range(hb))
    final = lax.fori_loop(0, u_ref.shape[0] // C, body, init)
    for hh in range(hb):
        state_ref[hh] = final[hh]


def _gdn(qkv, proj, gates, decay, gdn_norm, batch, seq):
    t = batch * seq
    bf = jnp.bfloat16
    C = GDN_CHUNK
    hb = min(GDN_WY_HEADS, GDN_HEADS)
    ts = _pick(seq, GDN_WY_ROWS)
    assert GDN_HEADS % hb == 0 and ts % (GDN_WY_CG * C) == 0
    hw = hb * GDN_DK
    kq = GDN_QK // hw
    blk = lambda off: pl.BlockSpec((ts, hw), lambda i, g: (i, off + g))
    oblk = pl.BlockSpec((ts, hw), lambda i, g: (i, g))
    cblk = pl.BlockSpec((ts, hb * C), lambda i, g: (i, g))
    u, w, qd, kd, qk = pl.pallas_call(
        functools.partial(_gdn_wy_kernel, hb=hb), grid=(t // ts, GDN_HEADS // hb),
        in_specs=[blk(0), blk(kq), blk(2 * kq),
                  pl.BlockSpec((ts, gates.shape[1]), lambda i, g: (i, 0)), cblk],
        out_specs=[oblk, oblk, oblk, oblk, cblk],
        out_shape=[jax.ShapeDtypeStruct((t, GDN_V), jnp.float32), jax.ShapeDtypeStruct((t, GDN_QK), bf),
                   jax.ShapeDtypeStruct((t, GDN_QK), bf), jax.ShapeDtypeStruct((t, GDN_QK), bf),
                   jax.ShapeDtypeStruct((t, GDN_HEADS * C), bf)],
        compiler_params=_params(("parallel", "parallel")), name="gdn_wy")(qkv, qkv, qkv, gates, decay)
    hb = min(GDN_SCAN_HEADS, GDN_HEADS)
    ts = _pick(seq, GDN_SCAN_ROWS)
    assert GDN_HEADS % hb == 0
    nsb = seq // ts
    hw = hb * GDN_DK
    zoff = OFF_GZ // hw
    blk = pl.BlockSpec((ts, hw), lambda b, g, s: (b * nsb + s, g))
    return pl.pallas_call(
        functools.partial(_gdn_scan_kernel, hb=hb), grid=(batch, GDN_HEADS // hb, nsb),
        in_specs=[blk, blk, blk, blk,
                  pl.BlockSpec((ts, hb * C), lambda b, g, s: (b * nsb + s, g)),
                  pl.BlockSpec((ts, hw), lambda b, g, s: (b * nsb + s, zoff + g)),
                  pl.BlockSpec((ts, gates.shape[1]), lambda b, g, s: (b * nsb + s, 0)),
                  pl.BlockSpec((1, GDN_DV), lambda b, g, s: (0, 0))],
        out_specs=blk,
        out_shape=jax.ShapeDtypeStruct((t, GDN_V), bf),
        scratch_shapes=[pltpu.VMEM((hb, GDN_DK, GDN_DV), jnp.float32)],
        compiler_params=_params(("parallel", "parallel", "arbitrary")), name="gdn_scan")(
            u, w, qd, kd, qk, proj, gates, gdn_norm.reshape(1, GDN_DV).astype(jnp.float32))


def _dsa_prep_kernel(aq_ref, ak_ref, iq_ref, ik_ref, sm_ref, cos_ref, slo_ref, shi_ref, gq_ref, gk_ref,
                     q_out, k_out, iq_out, ik_out, w_out):
    cosf, slo, shi = cos_ref[...], slo_ref[...], shi_ref[...]
    half = ROT_DIM // 2

    xs = ([aq_ref[:, h * HEAD_DIM:(h + 1) * HEAD_DIM].astype(jnp.float32) for h in range(ATT_HEADS)]
          + [ak_ref[:, h * HEAD_DIM:(h + 1) * HEAD_DIM].astype(jnp.float32) for h in range(ATT_KV_HEADS)])
    gains = [gq_ref[...]] * ATT_HEADS + [gk_ref[...]] * ATT_KV_HEADS
    ms = [jnp.mean(x * x, axis=-1, keepdims=True) for x in xs]
    xn = [x * lax.rsqrt(m + EPS) * g for x, m, g in zip(xs, ms, gains)]
    lo = [pltpu.roll(x, LANES - half, 1) for x in xn]
    hi = [pltpu.roll(x, half, 1) for x in xn]
    ro = [x * cosf + a * slo + b * shi for x, a, b in zip(xn, lo, hi)]
    for h in range(ATT_HEADS):
        q_out[h * Q_BLOCK:(h + 1) * Q_BLOCK, :] = (ro[h] * (HEAD_DIM ** -0.5 * math.log2(math.e))).astype(q_out.dtype)
    for h in range(ATT_KV_HEADS):
        k_out[:, h * HEAD_DIM:(h + 1) * HEAD_DIM] = ro[ATT_HEADS + h].astype(k_out.dtype)
    r = lax.broadcasted_iota(jnp.int32, (IDX_DIM, IDX_DIM), 0)
    c = lax.broadcasted_iota(jnp.int32, (IDX_DIM, IDX_DIM), 1)
    swap = jnp.where(((c < half) & (r == c + half)) | ((c >= half) & (c < 2 * half) & (r == c - half)),
                     1.0, 0.0).astype(jnp.bfloat16)
    sin_signed = slo + shi
    xs = jnp.concatenate([iq_ref[:, h * IDX_DIM:(h + 1) * IDX_DIM] for h in range(IDX_HEADS)]
                         + [ik_ref[...]], axis=0)
    sw = jnp.dot(xs, swap, preferred_element_type=jnp.float32)
    nh = IDX_HEADS + 1
    rot = (xs.astype(jnp.float32).reshape(nh, Q_BLOCK, IDX_DIM) * cosf[None]
           + sw.reshape(nh, Q_BLOCK, IDX_DIM) * sin_signed[None]).reshape(nh * Q_BLOCK, IDX_DIM)
    iq_out[...] = rot[:IDX_HEADS * Q_BLOCK].astype(iq_out.dtype)
    ik_out[...] = rot[IDX_HEADS * Q_BLOCK:].astype(ik_out.dtype)
    w_out[...] = sm_ref[...] * (IDX_HEADS ** -0.5 * IDX_DIM ** -0.5)


def _rope_tables(seq):
    half = ROT_DIM // 2
    inv_freq = ROPE_THETA ** (-jnp.arange(half, dtype=jnp.float32) * 2.0 / ROT_DIM)
    ang = jnp.arange(seq, dtype=jnp.float32)[:, None] * inv_freq[None, :]
    cos, sin = jnp.cos(ang), jnp.sin(ang)
    z = jnp.zeros((seq, LANES - 2 * half), jnp.float32)
    zh = jnp.zeros((seq, half), jnp.float32)
    cosf = jnp.concatenate([cos, cos, jnp.ones_like(z)], axis=1)
    sin_lo = jnp.concatenate([-sin, zh, z], axis=1)
    sin_hi = jnp.concatenate([zh, sin, z], axis=1)
    return cosf, sin_lo, sin_hi


def _dsa_prep(proj, small, att_q_norm, att_k_norm, batch, seq):
    assert ROT_DIM == IDX_ROT
    t = batch * seq
    nq = seq // Q_BLOCK
    qb = Q_BLOCK
    cosf, slo, shi = _rope_tables(seq)
    tab = pl.BlockSpec((qb, LANES), lambda b, i: (i, 0))
    row = lambda w, off: pl.BlockSpec((qb, w), lambda b, i: (b * nq + i, off // w))
    vec = pl.BlockSpec((1, HEAD_DIM), lambda b, i: (0, 0))
    aq_w, ak_w = ATT_HEADS * HEAD_DIM, ATT_KV_HEADS * HEAD_DIM
    return pl.pallas_call(
        _dsa_prep_kernel, grid=(batch, nq),
        in_specs=[row(aq_w, OFF_AQ), row(ak_w, OFF_AK), row(IQ_W, OFF_IQ), row(IDX_DIM, OFF_IK),
                  pl.BlockSpec((qb, small.shape[1]), lambda b, i: (b * nq + i, 0)),
                  tab, tab, tab, vec, vec],
        out_specs=[pl.BlockSpec((ATT_HEADS * qb, HEAD_DIM), lambda b, i: (b * nq + i, 0)),
                   pl.BlockSpec((qb, ak_w), lambda b, i: (b * nq + i, 0)),
                   pl.BlockSpec((IDX_HEADS * qb, IDX_DIM), lambda b, i: (b * nq + i, 0)),
                   pl.BlockSpec((qb, IDX_DIM), lambda b, i: (b * nq + i, 0)),
                   pl.BlockSpec((qb, small.shape[1]), lambda b, i: (b * nq + i, 0))],
        out_shape=[jax.ShapeDtypeStruct((t * ATT_HEADS, HEAD_DIM), jnp.bfloat16),
                   jax.ShapeDtypeStruct((t, ak_w), jnp.bfloat16),
                   jax.ShapeDtypeStruct((t * IDX_HEADS, IDX_DIM), jnp.bfloat16),
                   jax.ShapeDtypeStruct((t, IDX_DIM), jnp.bfloat16),
                   jax.ShapeDtypeStruct((t, small.shape[1]), jnp.float32)],
        compiler_params=_params(("parallel", "parallel")), name="dsa_prep")(
            proj, proj, proj, proj, small, cosf, slo, shi,
            att_q_norm.reshape(1, HEAD_DIM).astype(jnp.float32),
            att_k_norm.reshape(1, HEAD_DIM).astype(jnp.float32))


DSA_KB = 512
DSA_SB = 512
DSA_HEAD_GROUPS = 8
INT_MIN = -2 ** 31


def _dsa_kernel(iq_ref, q_ref, w_ref, ik_ref, k_ref, v_ref, o_ref,
                key_ref, keyt_ref, wb_ref, m_ref, l_ref, acc_ref, *, topk):
    qi = pl.program_id(1)
    QB, KB, SB = Q_BLOCK, DSA_KB, DSA_SB
    n_kb = (qi * QB + QB + KB - 1) // KB
    qpos = qi * QB + lax.broadcasted_iota(jnp.int32, (QB, SB), 0)
    kiota = lax.broadcasted_iota(jnp.int32, (QB, SB), 1)

    w = w_ref[...]
    for h in range(IDX_HEADS):
        wb_ref[h] = jnp.broadcast_to(w[:, SM_IW + h:SM_IW + h + 1], (QB, LANES))

    def score_block(sb, carry):
        k0 = pl.multiple_of(sb * SB, SB)
        ikb = ik_ref[pl.ds(k0, SB), :]
        hg = IDX_HEADS // DSA_HEAD_GROUPS
        lgs = [lax.dot_general(iq_ref[g * hg * QB:(g + 1) * hg * QB, :], ikb, (((1,), (1,)), ((), ())),
                               preferred_element_type=jnp.float32) for g in range(DSA_HEAD_GROUPS)]
        sc = None
        for g, lg in enumerate(lgs):
            for hh in range(hg):
                wb = wb_ref[g * hg + hh]
                term = jnp.maximum(lg[hh * QB:(hh + 1) * QB], 0.0) * jnp.concatenate([wb] * (SB // LANES), axis=1)
                sc = term if sc is None else sc + term
        bits = pltpu.bitcast(sc + 0.0, jnp.int32)
        keys = bits ^ ((bits >> 31) & 0x7FFFFFFF)
        keys = jnp.where(k0 + kiota <= qpos, keys, INT_MIN)
        key_ref[:, pl.ds(k0, SB)] = keys
        keyt_ref[pl.ds(k0, SB), :] = keys.T
        return carry

    n_sb = (qi * QB + QB + SB - 1) // SB
    lax.fori_loop(0, n_sb, score_block, 0)

    @pl.when(n_sb < n_kb * (KB // SB))
    def _():
        k0 = pl.multiple_of(n_sb * SB, SB)
        key_ref[:, pl.ds(k0, SB)] = jnp.full((QB, SB), INT_MIN, jnp.int32)
        keyt_ref[pl.ds(k0, SB), :] = jnp.full((SB, QB), INT_MIN, jnp.int32)

    def count_ge(cand):
        def blk(kb, cnt):
            k0 = pl.multiple_of(kb * KB, KB)
            parts = [jnp.where(keyt_ref[pl.ds(k0 + 8 * j, 8), :] >= cand, 1.0, 0.0) for j in range(KB // 8)]
            while len(parts) > 1:
                parts = [parts[j] + parts[j + 1] for j in range(0, len(parts), 2)]
            return cnt + parts[0]
        cnt = lax.fori_loop(0, n_kb, blk, jnp.zeros((8, LANES), jnp.float32))
        return jnp.sum(cnt, axis=0, keepdims=True)

    def to_cols(x):
        return jnp.concatenate([x] * (QB // 8), axis=0).T

    tile = lambda x: jnp.broadcast_to(x, (8, LANES))
    cnt0 = tile(count_ge(jnp.zeros((8, LANES), jnp.int32)))
    thr0 = jnp.where(cnt0 >= topk, 0, jnp.full((8, LANES), INT_MIN, jnp.int32))

    def bit_step(i, carry):
        thr, cnt_thr = carry
        cand = thr | (jnp.int32(1) << (30 - i))
        cnt = tile(count_ge(cand))
        return jnp.where(cnt >= topk, cand, thr), jnp.where(cnt >= topk, cnt, cnt_thr)

    thr, cnt_thr = lax.fori_loop(0, 31, bit_step, (thr0, cnt0))

    surplus = jnp.where((thr > INT_MIN) & (cnt_thr > topk), 1.0, 0.0)

    @pl.when(jnp.max(surplus) > 0.0)
    def _():
        need = topk - tile(count_ge(thr + 1))
        sub8 = lax.broadcasted_iota(jnp.int32, (8, LANES), 0)

        def count_tied_before(idx):
            def blk(kb, cnt):
                k0 = pl.multiple_of(kb * KB, KB)
                for j in range(KB // 8):
                    kk = keyt_ref[pl.ds(k0 + 8 * j, 8), :]
                    hit = jnp.where(kk == thr, jnp.where(k0 + 8 * j + sub8 < idx, 1.0, 0.0), 0.0)
                    cnt = cnt + hit
                return cnt
            cnt = lax.fori_loop(0, n_kb, blk, jnp.zeros((8, LANES), jnp.float32))
            return tile(jnp.sum(cnt, axis=0, keepdims=True))

        n_bits = (key_ref.shape[1] - 1).bit_length()

        def idx_step(i, last):
            cand = last | (jnp.int32(1) << (n_bits - 1 - i))
            return jnp.where(count_tied_before(cand) < need, cand, last)

        last = lax.fori_loop(0, n_bits, idx_step, jnp.zeros((8, LANES), jnp.int32))
        thr_c = jnp.concatenate([to_cols(thr)] * (KB // LANES), axis=1)
        last_c = jnp.concatenate([to_cols(last)] * (KB // LANES), axis=1)
        flag_c = jnp.concatenate([to_cols(surplus)] * (KB // LANES), axis=1)
        pos = lax.broadcasted_iota(jnp.int32, (QB, KB), 1)

        def demote(kb, carry):
            k0 = pl.multiple_of(kb * KB, KB)
            kk = key_ref[:, pl.ds(k0, KB)]
            drop = jnp.where(kk == thr_c, jnp.where(k0 + pos > last_c, flag_c, 0.0), 0.0)
            key_ref[:, pl.ds(k0, KB)] = jnp.where(drop > 0.0, thr_c - 1, kk)
            return carry

        lax.fori_loop(0, n_kb, demote, 0)

    thr = to_cols(jnp.maximum(thr, INT_MIN + 1))

    G = ATT_GROUP
    NG = KB // LANES
    m_ref[...] = jnp.full(m_ref.shape, NEG_BIG, jnp.float32)
    l_ref[...] = jnp.zeros(l_ref.shape, jnp.float32)
    acc_ref[...] = jnp.zeros(acc_ref.shape, jnp.float32)

    def lane_fold(x, op):
        parts = [x[:, g * LANES:(g + 1) * LANES] for g in range(NG)]
        while len(parts) > 1:
            parts = [op(parts[i], parts[i + 1]) for i in range(0, len(parts), 2)]
        return parts[0]

    def attn_block(kb, carry):
        k0 = pl.multiple_of(kb * KB, KB)
        bias = jnp.where(key_ref[:, pl.ds(k0, KB)] >= jnp.concatenate([thr] * NG, axis=1), 0.0, NEG_BIG)
        bias_g = jnp.concatenate([bias] * G, axis=0)
        grp = range(ATT_KV_HEADS)
        rows = [slice(hk * G * QB, (hk + 1) * G * QB) for hk in grp]
        cols = [slice(hk * HEAD_DIM, (hk + 1) * HEAD_DIM) for hk in grp]
        s = [lax.dot_general(q_ref[rows[i], :], k_ref[pl.ds(k0, KB), cols[i]], (((1,), (1,)), ((), ())),
                             preferred_element_type=jnp.float32) for i in grp]
        p, alpha, m_new = [], [], []
        for i in grp:
            si = s[i] + bias_g
            m_old = m_ref[rows[i], :]
            mi = jnp.maximum(m_old, jnp.max(lane_fold(si, jnp.maximum), axis=1, keepdims=True))
            p.append(jnp.exp2(si - jnp.concatenate([mi] * NG, axis=1)).astype(jnp.bfloat16))
            alpha.append(jnp.exp2(m_old - mi))
            m_new.append(mi)
        ones = jnp.ones((KB, LANES), jnp.bfloat16)
        pv = [jnp.dot(p[i], jnp.concatenate([v_ref[pl.ds(k0, KB), cols[i]], ones], axis=1),
                      preferred_element_type=jnp.float32) for i in grp]
        for i in grp:
            l_ref[rows[i], :] = alpha[i] * l_ref[rows[i], :] + pv[i][:, HEAD_DIM:]
            acc_ref[rows[i], :] = alpha[i] * acc_ref[rows[i], :] + pv[i][:, :HEAD_DIM]
            m_ref[rows[i], :] = m_new[i]
        return carry

    lax.fori_loop(0, n_kb, attn_block, 0)
    for h in range(ATT_HEADS):
        rows = slice(h * QB, (h + 1) * QB)
        o_ref[:, h * HEAD_DIM:(h + 1) * HEAD_DIM] = (acc_ref[rows, :] / l_ref[rows, :]).astype(o_ref.dtype)


def _dsa(iq_hm, q_hm, w_s, ik_rot, k_att, proj, batch, seq, topk):
    nq = seq // Q_BLOCK
    qb = Q_BLOCK
    kvw = ATT_KV_HEADS * HEAD_DIM
    return pl.pallas_call(
        functools.partial(_dsa_kernel, topk=topk), grid=(batch, nq),
        in_specs=[pl.BlockSpec((IDX_HEADS * qb, IDX_DIM), lambda b, i: (b * nq + i, 0)),
                  pl.BlockSpec((ATT_HEADS * qb, HEAD_DIM), lambda b, i: (b * nq + i, 0)),
                  pl.BlockSpec((qb, w_s.shape[1]), lambda b, i: (b * nq + i, 0)),
                  pl.BlockSpec((seq, IDX_DIM), lambda b, i: (b, 0)),
                  pl.BlockSpec((seq, kvw), lambda b, i: (b, 0)),
                  pl.BlockSpec((seq, kvw), lambda b, i: (b, OFF_AV // kvw))],
        out_specs=pl.BlockSpec((qb, ATT_HEADS * HEAD_DIM), lambda b, i: (b * nq + i, 0)),
        out_shape=jax.ShapeDtypeStruct((batch * seq, ATT_HEADS * HEAD_DIM), jnp.bfloat16),
        scratch_shapes=[pltpu.VMEM((qb, seq), jnp.int32),
                        pltpu.VMEM((seq, qb), jnp.int32),
                        pltpu.VMEM((IDX_HEADS, qb, LANES), jnp.float32),
                        pltpu.VMEM((ATT_HEADS * qb, LANES), jnp.float32),
                        pltpu.VMEM((ATT_HEADS * qb, LANES), jnp.float32),
                        pltpu.VMEM((ATT_HEADS * qb, HEAD_DIM), jnp.float32)],
        compiler_params=_params(("parallel", "arbitrary")), name="dsa")(
            iq_hm, q_hm, w_s, ik_rot, k_att, proj)


def _mem_block_kernel(x_ref, gin_ref, wq_ref, kv_ref, gq_ref, gk_ref, wo_ref, gout_ref, y_ref, h_ref):
    def norm(t, gain):
        return t * lax.rsqrt(jnp.mean(t * t, axis=-1, keepdims=True) + EPS) * gain

    x = x_ref[...]
    q_all = jnp.dot(norm(x, gin_ref[...]).astype(jnp.bfloat16), wq_ref[...], preferred_element_type=jnp.float32)
    hs = range(MEM_HEADS)
    cols = [slice(h * MEM_HEAD_DIM, (h + 1) * MEM_HEAD_DIM) for h in hs]
    q = [norm(q_all[:, c], gq_ref[...]) * (MEM_HEAD_DIM ** -0.5) for c in cols]
    k = [norm(kv_ref[:, c], gk_ref[...]) for c in cols]
    s = [_bdot_nt(q[h], k[h]) for h in hs]
    p = [jnp.exp(s[h] - jnp.max(s[h], axis=1, keepdims=True)) for h in hs]
    pv = [_bdot(p[h], kv_ref[:, MEM_WIDTH + h * MEM_HEAD_DIM:MEM_WIDTH + (h + 1) * MEM_HEAD_DIM]) for h in hs]
    o = [pv[h] / jnp.sum(p[h], axis=1, keepdims=True) for h in hs]
    mo = jnp.concatenate(o, axis=1).astype(jnp.bfloat16)
    y = x + jnp.dot(mo, wo_ref[...], preferred_element_type=jnp.float32)
    y_ref[...] = y
    h_ref[...] = norm(y, gout_ref[...]).astype(h_ref.dtype)


def _mem_block(x2d, mkv, norm_in, w_q, gq, gk, w_o, norm_out, batch, seq, n_mem):
    t, d = x2d.shape
    tm = _pick(seq, 256)
    nq = seq // tm
    row = pl.BlockSpec((tm, d), lambda b, i: (b * nq + i, 0))
    full = lambda a: pl.BlockSpec(a.shape, lambda b, i: (0, 0))
    vec = lambda v, n: v.reshape(1, n).astype(jnp.float32)
    args = [x2d, vec(norm_in, d), w_q, mkv, vec(gq, MEM_HEAD_DIM), vec(gk, MEM_HEAD_DIM), w_o, vec(norm_out, d)]
    specs = [row, full(args[1]), full(w_q), pl.BlockSpec((n_mem, 2 * MEM_WIDTH), lambda b, i: (b, 0)),
             full(args[4]), full(args[5]), full(w_o), full(args[7])]
    return pl.pallas_call(
        _mem_block_kernel, grid=(batch, nq), in_specs=specs, out_specs=[row, row],
        out_shape=[jax.ShapeDtypeStruct((t, d), jnp.float32), jax.ShapeDtypeStruct((t, d), jnp.bfloat16)],
        compiler_params=_params(("parallel", "parallel")), name="mem_block")(*args)


def _split_w_in(w):
    sizes = (GDN_QK, GDN_QK, GDN_V, GDN_V, GDN_HEADS, GDN_HEADS, ATT_HEADS * HEAD_DIM,
             ATT_KV_HEADS * HEAD_DIM, ATT_KV_HEADS * HEAD_DIM, IQ_W, IDX_DIM, IDX_HEADS)
    offs = np.concatenate([[0], np.cumsum(sizes)])
    wt = w.T
    seg = lambda i: wt[int(offs[i]):int(offs[i + 1]), :]
    gq, gk, gv, gz, ga, gb, aq, ak, av, iq, ik, iw = (seg(i) for i in range(12))
    d = w.shape[0]
    bf = jnp.bfloat16
    main = jnp.concatenate([s.astype(bf) for s in (iq, gq, gk, gv, gz, aq, ak, av, ik)]
                           + [jnp.zeros((N_MAIN_PAD - N_MAIN, d), bf)], axis=0)
    nsm = 2 * GDN_HEADS + IDX_HEADS
    small = jnp.concatenate([ga, gb, iw, jnp.zeros((LANES - nsm, d), w.dtype)], axis=0)
    small_hi = small.astype(bf)
    small_lo = (small - small_hi.astype(jnp.float32)).astype(bf)
    return main, small_hi, small_lo


def kernel(x, mem, norm_mix, w_in, conv_w, a_log, dt_bias, gdn_norm, att_q_norm, att_k_norm, w_out,
           norm_mem_q, norm_mem_kv, w_mem_q, w_mem_kv, mem_q_norm, mem_k_norm, w_mem_o, norm_ffn,
           w_gate, w_up, w_down):
    B, S, D = x.shape
    M = mem.shape[1]
    T = B * S
    topk = min(TOPK_MAX, S // 4)
    bf = jnp.bfloat16
    x2 = x.reshape(T, D)
    for l in range(w_in.shape[0]):
        w_main, *w_small = _split_w_in(w_in[l])
        h, small = _rmsnorm(x2, norm_mix[l], w_small)
        proj = _matmul([(h, w_main)], out_dtype=bf, tm=2048, tn=IN_TN, w_t=True, name="in_proj")
        qkv = _gdn_prep(proj, conv_w[l], S)
        gates, decay = _gate_prep(small, a_log[l], dt_bias[l])
        o_a = _gdn(qkv, proj, gates, decay, gdn_norm[l], B, S)
        q_hm, k_att, iq_hm, ik_rot, w_s = _dsa_prep(proj, small, att_q_norm[l], att_k_norm[l], B, S)
        o_b = _dsa(iq_hm, q_hm, w_s, ik_rot, k_att, proj, B, S, topk)
        wo = w_out[l].astype(bf)
        x2 = _matmul([(o_a, wo[:GDN_V]), (o_b, wo[GDN_V:])], res=x2, tn=1024, name="out_proj")
        hm = _rmsnorm(mem.reshape(B * M, D), norm_mem_kv[l])
        mkv = _matmul([(hm, w_mem_kv[l].astype(bf))], name="mem_kv")
        x2, hf = _mem_block(x2, mkv, norm_mem_q[l], w_mem_q[l].astype(bf), mem_q_norm[l], mem_k_norm[l],
                            w_mem_o[l].astype(bf), norm_ffn[l], B, S, M)
        mid = _gateup(hf, w_gate[l], w_up[l])
        x2 = _matmul([(mid, w_down[l].astype(bf))], res=x2, tm=512, tn=512, name="ffn_down")
    return x2.reshape(B, S, D)
```

```python
import functools
import math

import jax
import jax.numpy as jnp
import numpy as np
from jax import lax
from jax.experimental import pallas as pl
from jax.experimental.pallas import tpu as pltpu

D_MODEL = 4096
ATT_KV_HEADS = 4
IDX_HEADS = 32
TOPK_MAX = 256

EPS = 1e-6
ROPE_THETA = 500000.0
HEAD_DIM = 128
ROT_DIM = HEAD_DIM // 4
GDN_DK = 128
GDN_DV = 128
GDN_HEADS = (D_MODEL // 2) // GDN_DV
GDN_CONV = 4
GDN_CHUNK = 64
GDN_QK = GDN_HEADS * GDN_DK
GDN_V = GDN_HEADS * GDN_DV
ATT_HEADS = (D_MODEL // 2) // HEAD_DIM
ATT_GROUP = ATT_HEADS // ATT_KV_HEADS
IDX_DIM = 128
IDX_ROT = IDX_DIM // 4
Q_BLOCK = 128
MEM_HEADS = 4
MEM_HEAD_DIM = 128
MEM_WIDTH = MEM_HEADS * MEM_HEAD_DIM

LANES = 128
VMEM_LIMIT = 56 * 1024 * 1024
NEG_BIG = -1e30

IQ_W = IDX_HEADS * IDX_DIM
OFF_IQ = 0
OFF_GQ = OFF_IQ + IQ_W
OFF_GK = OFF_GQ + GDN_QK
OFF_GV = OFF_GK + GDN_QK
OFF_GZ = OFF_GV + GDN_V
OFF_AQ = OFF_GZ + GDN_V
OFF_AK = OFF_AQ + ATT_HEADS * HEAD_DIM
OFF_AV = OFF_AK + ATT_KV_HEADS * HEAD_DIM
OFF_IK = OFF_AV + ATT_KV_HEADS * HEAD_DIM
N_MAIN = OFF_IK + IDX_DIM
IN_TN = 512
N_MAIN_PAD = -(-N_MAIN // IN_TN) * IN_TN
SM_GB = GDN_HEADS
SM_IW = 2 * GDN_HEADS
HI = lax.Precision.HIGHEST


def _params(sem):
    return pltpu.CompilerParams(dimension_semantics=sem, vmem_limit_bytes=VMEM_LIMIT)


def _pick(n, pref):
    t = min(n, pref)
    while n % t:
        t //= 2
    return t


def _sigmoid(x):
    return 1.0 / (1.0 + jnp.exp2(x * (-math.log2(math.e))))


def _rmsnorm_kernel(x_ref, g_ref, o_ref):
    x = x_ref[...]
    ms = jnp.mean(x * x, axis=-1, keepdims=True)
    o_ref[...] = (x * lax.rsqrt(ms + EPS) * g_ref[...]).astype(o_ref.dtype)


def _rmsnorm_side_kernel(x_ref, g_ref, whi_ref, wlo_ref, o_ref, s_ref):
    x = x_ref[...]
    ms = jnp.mean(x * x, axis=-1, keepdims=True)
    xn = x * lax.rsqrt(ms + EPS) * g_ref[...]
    hi = xn.astype(o_ref.dtype)
    o_ref[...] = hi
    lo = (xn - hi.astype(jnp.float32)).astype(hi.dtype)
    dot_t = lambda a, wt: lax.dot_general(a, wt, (((1,), (1,)), ((), ())), preferred_element_type=jnp.float32)
    s_ref[...] = dot_t(hi, whi_ref[...]) + dot_t(lo, whi_ref[...]) + dot_t(hi, wlo_ref[...])


def _rmsnorm(x2d, g, w_side=None, tm=256):
    m, d = x2d.shape
    tm = _pick(m, tm)
    g2 = g.reshape(1, d).astype(jnp.float32)
    xs = pl.BlockSpec((tm, d), lambda i: (i, 0))
    gs = pl.BlockSpec((1, d), lambda i: (0, 0))
    if w_side is None:
        return pl.pallas_call(
            _rmsnorm_kernel, grid=(m // tm,), in_specs=[xs, gs], out_specs=xs,
            out_shape=jax.ShapeDtypeStruct((m, d), jnp.bfloat16),
            compiler_params=_params(("parallel",)), name="rmsnorm")(x2d, g2)
    w_hi, w_lo = w_side
    ns = w_hi.shape[0]
    ws = pl.BlockSpec((ns, d), lambda i: (0, 0))
    return pl.pallas_call(
        _rmsnorm_side_kernel, grid=(m // tm,),
        in_specs=[xs, gs, ws, ws],
        out_specs=[xs, pl.BlockSpec((tm, ns), lambda i: (i, 0))],
        out_shape=[jax.ShapeDtypeStruct((m, d), jnp.bfloat16),
                   jax.ShapeDtypeStruct((m, ns), jnp.float32)],
        compiler_params=_params(("parallel",)), name="rmsnorm_side")(x2d, g2, w_hi, w_lo)


def _matmul_kernel(*refs, n_pairs, has_res, w_t):
    o_ref = refs[-1]
    rhs_dim = 1 if w_t else 0
    acc = None
    for p in range(n_pairs):
        d = lax.dot_general(refs[2 * p][...], refs[2 * p + 1][...], (((1,), (rhs_dim,)), ((), ())),
                            preferred_element_type=jnp.float32)
        acc = d if acc is None else acc + d
    if has_res:
        acc = acc + refs[2 * n_pairs][...]
    o_ref[...] = acc.astype(o_ref.dtype)


def _matmul(pairs, res=None, out_dtype=jnp.float32, tm=1024, tn=512, w_t=False, name="matmul"):
    m = pairs[0][0].shape[0]
    n = pairs[0][1].shape[0 if w_t else 1]
    tm, tn = _pick(m, tm), _pick(n, tn)
    args, specs = [], []
    for a, w in pairs:
        k = a.shape[1]
        args += [a, w]
        wspec = pl.BlockSpec((tn, k), lambda i, j: (j, 0)) if w_t else pl.BlockSpec((k, tn), lambda i, j: (0, j))
        specs += [pl.BlockSpec((tm, k), lambda i, j: (i, 0)), wspec]
    if res is not None:
        args.append(res)
        specs.append(pl.BlockSpec((tm, tn), lambda i, j: (i, j)))
    return pl.pallas_call(
        functools.partial(_matmul_kernel, n_pairs=len(pairs), has_res=res is not None, w_t=w_t),
        grid=(m // tm, n // tn), in_specs=specs,
        out_specs=pl.BlockSpec((tm, tn), lambda i, j: (i, j)),
        out_shape=jax.ShapeDtypeStruct((m, n), out_dtype),
        compiler_params=_params(("parallel", "arbitrary")), name=name)(*args)


def _gateup_kernel(a_ref, wg_ref, wu_ref, o_ref):
    a = a_ref[...]
    g = jnp.dot(a, wg_ref[...].astype(a.dtype), preferred_element_type=jnp.float32)
    u = jnp.dot(a, wu_ref[...].astype(a.dtype), preferred_element_type=jnp.float32)
    o_ref[...] = (g * _sigmoid(g) * u).astype(o_ref.dtype)


def _gateup(a, wg, wu, tm=1024, tn=256):
    m, k = a.shape
    n = wg.shape[1]
    tm, tn = _pick(m, tm), _pick(n, tn)
    ws = pl.BlockSpec((k, tn), lambda i, j: (0, j))
    return pl.pallas_call(
        _gateup_kernel, grid=(m // tm, n // tn),
        in_specs=[pl.BlockSpec((tm, k), lambda i, j: (i, 0)), ws, ws],
        out_specs=pl.BlockSpec((tm, tn), lambda i, j: (i, j)),
        out_shape=jax.ShapeDtypeStruct((m, n), jnp.bfloat16),
        compiler_params=_params(("parallel", "arbitrary")), name="ffn_gateup")(a, wg, wu)


HALO = 16


def _gdn_prep_kernel(x_ref, halo_ref, w_ref, o_ref, *, ts, tc, blocks_per_seq, nqk_blocks):
    i = pl.program_id(0)
    j = pl.program_id(1)
    first = (i % blocks_per_seq) == 0
    x = x_ref[...].astype(jnp.float32)
    halo8 = jnp.where(first, 0.0, halo_ref[HALO - 8:HALO, :].astype(jnp.float32))
    w = w_ref[...]
    row8 = lax.broadcasted_iota(jnp.int32, (8, tc), 0)
    y = x * w[GDN_CONV - 1:GDN_CONV, :]
    for d in range(1, GDN_CONV):
        xs = pltpu.roll(x, d, 0)
        head = jnp.where(row8 < d, pltpu.roll(halo8, d, 0), xs[0:8])
        xs = jnp.concatenate([head, xs[8:]], axis=0)
        y = y + xs * w[GDN_CONV - 1 - d:GDN_CONV - d, :]
    y = y * _sigmoid(y)
    is_q = j < nqk_blocks
    is_v = j >= 2 * nqk_blocks
    for h in range(tc // GDN_DK):
        yh = y[:, h * GDN_DK:(h + 1) * GDN_DK]
        rs = lax.rsqrt(jnp.sum(yh * yh, axis=-1, keepdims=True) + EPS)
        rs = rs * jnp.where(is_q, GDN_DK ** -0.5, 1.0)
        fac = jnp.where(is_v, 1.0, rs)
        o_ref[:, h * GDN_DK:(h + 1) * GDN_DK] = (yh * fac).astype(o_ref.dtype)


def _gdn_prep(proj, conv_w, seq):
    t = proj.shape[0]
    ts = _pick(seq, 512)
    tc = 512
    nconv = (2 * GDN_QK + GDN_V) // tc
    col0 = OFF_GQ // tc
    rb = ts // HALO
    kern = functools.partial(_gdn_prep_kernel, ts=ts, tc=tc, blocks_per_seq=seq // ts,
                             nqk_blocks=GDN_QK // tc)
    return pl.pallas_call(
        kern, grid=(t // ts, nconv),
        in_specs=[pl.BlockSpec((ts, tc), lambda i, j: (i, col0 + j)),
                  pl.BlockSpec((HALO, tc), lambda i, j: (jnp.maximum(i * rb - 1, 0), col0 + j)),
                  pl.BlockSpec((GDN_CONV, tc), lambda i, j: (0, j))],
        out_specs=pl.BlockSpec((ts, tc), lambda i, j: (i, j)),
        out_shape=jax.ShapeDtypeStruct((t, nconv * tc), jnp.bfloat16),
        compiler_params=_params(("parallel", "parallel")), name="gdn_prep")(proj, proj, conv_w)


def _gate_prep_kernel(s_ref, alog_ref, dtb_ref, o_ref, dec_ref, *, ts):
    C = GDN_CHUNK
    s = s_ref[...]
    lane = lax.broadcasted_iota(jnp.int32, s.shape, 1)
    xa = s + dtb_ref[...]
    softplus = jnp.maximum(xa, 0.0) + jnp.log(1.0 + jnp.exp(-jnp.abs(xa)))
    g = -jnp.exp(alog_ref[...]) * softplus
    g = jnp.where(lane < GDN_HEADS, g, 0.0)
    r = lax.broadcasted_iota(jnp.int32, (ts, ts), 0)
    c = lax.broadcasted_iota(jnp.int32, (ts, ts), 1)
    tri = jnp.where((r // C == c // C) & (c <= r), 1.0, 0.0)
    gcum = jnp.dot(tri, g, precision=HI, preferred_element_type=jnp.float32)
    beta = _sigmoid(s)
    is_beta = (lane >= SM_GB) & (lane < SM_GB + GDN_HEADS)
    o_ref[...] = jnp.where(lane < GDN_HEADS, gcum, jnp.where(is_beta, beta, 0.0))
    gcum_t = gcum.T
    incl = lax.broadcasted_iota(jnp.int32, (C, C), 1) <= lax.broadcasted_iota(jnp.int32, (C, C), 0)
    sub = lax.broadcasted_iota(jnp.int32, (8, C), 0)
    hc = [(h, ci) for h in range(GDN_HEADS) for ci in range(ts // C)]
    cols = [jnp.broadcast_to(gcum[ci * C:(ci + 1) * C, h:h + 1], (C, C)) for h, ci in hc]
    tiles = [gcum_t[8 * (h // 8):8 * (h // 8) + 8, ci * C:(ci + 1) * C] for h, ci in hc]
    rows = [jnp.sum(jnp.where(sub == h % 8, t, 0.0), axis=0, keepdims=True) for (h, ci), t in zip(hc, tiles)]
    for (h, ci), col, row in zip(hc, cols, rows):
        dec_ref[ci * C:(ci + 1) * C, h * C:(h + 1) * C] = jnp.exp(jnp.where(incl, col - row, NEG_BIG))


def _gate_prep(small, a_log, dt_bias):
    t, ns = small.shape
    ts = _pick(t, 256)
    pad = lambda v: jnp.zeros((1, ns), jnp.float32).at[0, :GDN_HEADS].set(v.astype(jnp.float32))
    vs = pl.BlockSpec((1, ns), lambda i: (0, 0))
    bs = pl.BlockSpec((ts, ns), lambda i: (i, 0))
    dw = GDN_HEADS * GDN_CHUNK
    return pl.pallas_call(
        functools.partial(_gate_prep_kernel, ts=ts), grid=(t // ts,),
        in_specs=[bs, vs, vs], out_specs=[bs, pl.BlockSpec((ts, dw), lambda i: (i, 0))],
        out_shape=[jax.ShapeDtypeStruct((t, ns), jnp.float32), jax.ShapeDtypeStruct((t, dw), jnp.float32)],
        compiler_params=_params(("parallel",)), name="gate_prep")(small, pad(a_log), pad(dt_bias))


GDN_WY_HEADS = 4
GDN_WY_ROWS = 512
GDN_WY_CG = 8
GDN_SCAN_HEADS = 16
GDN_SCAN_ROWS = 512


def _bdot(a, b):
    return jnp.dot(a.astype(jnp.bfloat16), b.astype(jnp.bfloat16), preferred_element_type=jnp.float32)


def _bdot_nt(a, b):
    return lax.dot_general(a.astype(jnp.bfloat16), b.astype(jnp.bfloat16), (((1,), (1,)), ((), ())),
                           preferred_element_type=jnp.float32)


def _head_col(gsl, idx):
    lane = lax.broadcasted_iota(jnp.int32, (1, LANES), 1)
    return jnp.sum(gsl * jnp.where(lane == idx, 1.0, 0.0), axis=1, keepdims=True)


def _gdn_wy_kernel(q_ref, k_ref, v_ref, g_ref, dec_ref, u_ref, w_ref, qd_ref, kd_ref, qk_ref, *, hb):
    hg = pl.program_id(1)
    C, CG = GDN_CHUNK, GDN_WY_CG
    r = lax.broadcasted_iota(jnp.int32, (C, C), 0)
    c = lax.broadcasted_iota(jnp.int32, (C, C), 1)
    off_diag = c != r

    def body(it, carry):
        where, q, k, v, gc, beta, decay = [], [], [], [], [], [], []
        for cc in range(CG):
            r0 = pl.multiple_of((it * CG + cc) * C, C)
            gsl = g_ref[pl.ds(r0, C), :]
            for hh in range(hb):
                sl = slice(hh * GDN_DK, (hh + 1) * GDN_DK)
                h = hg * hb + hh
                where.append((r0, hh))
                q.append(q_ref[pl.ds(r0, C), sl].astype(jnp.float32))
                k.append(k_ref[pl.ds(r0, C), sl].astype(jnp.float32))
                v.append(v_ref[pl.ds(r0, C), sl].astype(jnp.float32))
                gc.append(_head_col(gsl, h))
                beta.append(_head_col(gsl, SM_GB + h))
                decay.append(dec_ref[pl.ds(r0, C), hh * C:(hh + 1) * C])
        n = len(where)
        ch = range(n)
        kb = [k[i] * beta[i] for i in ch]
        eg = [jnp.exp(gc[i]) for i in ch]
        kk = [_bdot_nt(kb[i], k[i]) for i in ch]
        qk = [_bdot_nt(q[i], k[i]) for i in ch]
        pw = [jnp.where(off_diag, kk[i] * decay[i], 0.0) for i in ch]
        x = [jnp.concatenate([v[i] * beta[i], kb[i] * eg[i]], axis=1) for i in ch]
        n_lvl = int(math.log2(C))
        for lvl in range(n_lvl):
            px = [_bdot(pw[i], x[i]) for i in ch]
            if lvl + 1 < n_lvl:
                pw = [_bdot(pw[i], pw[i]) for i in ch]
            x = [x[i] - px[i] if lvl == 0 else x[i] + px[i] for i in ch]
        for i, (r0, hh) in enumerate(where):
            sl = slice(hh * GDN_DK, (hh + 1) * GDN_DK)
            u_ref[pl.ds(r0, C), sl] = x[i][:, :GDN_DV]
            w_ref[pl.ds(r0, C), sl] = x[i][:, GDN_DV:].astype(w_ref.dtype)
            qd_ref[pl.ds(r0, C), sl] = (q[i] * eg[i]).astype(qd_ref.dtype)
            kd_ref[pl.ds(r0, C), sl] = (k[i] * jnp.exp(gc[i][C - 1:C, :] - gc[i])).astype(kd_ref.dtype)
            qk_ref[pl.ds(r0, C), hh * C:(hh + 1) * C] = (qk[i] * decay[i]).astype(qk_ref.dtype)
        return carry

    lax.fori_loop(0, q_ref.shape[0] // (CG * C), body, 0)


def _gdn_scan_kernel(u_ref, w_ref, qd_ref, kd_ref, qk_ref, z_ref, g_ref, gain_ref, o_ref, state_ref, *, hb):
    hg = pl.program_id(1)
    C = GDN_CHUNK
    sub8 = lax.broadcasted_iota(jnp.int32, (8, LANES), 0)
    gain = gain_ref[...]

    @pl.when(pl.program_id(2) == 0)
    def _():
        state_ref[...] = jnp.zeros(state_ref.shape, state_ref.dtype)

    def body(ci, states):
        r0 = pl.multiple_of(ci * C, C)
        g_tail = g_ref[pl.ds(r0 + C - 8, 8), :]
        g_last_row = jnp.sum(jnp.where(sub8 == 7, g_tail, 0.0), axis=0, keepdims=True)
        loaded = []
        for hh in range(hb):
            sl = slice(hh * GDN_DK, (hh + 1) * GDN_DK)
            loaded.append((u_ref[pl.ds(r0, C), sl], w_ref[pl.ds(r0, C), sl], qd_ref[pl.ds(r0, C), sl],
                           kd_ref[pl.ds(r0, C), sl], qk_ref[pl.ds(r0, C), hh * C:(hh + 1) * C],
                           z_ref[pl.ds(r0, C), sl].astype(jnp.float32)))
        hs = range(hb)
        f32 = jnp.float32
        u, w, qd, kd, qk, z = zip(*loaded)
        sb = [states[i].astype(jnp.bfloat16) for i in hs]
        ws = [jnp.dot(w[i], sb[i], preferred_element_type=f32) for i in hs]
        qs = [jnp.dot(qd[i], sb[i], preferred_element_type=f32) for i in hs]
        vb = [(u[i] - ws[i]).astype(jnp.bfloat16) for i in hs]
        kv = [lax.dot_general(kd[i], vb[i], (((0,), (0,)), ((), ())), preferred_element_type=f32) for i in hs]
        qv = [jnp.dot(qk[i], vb[i], preferred_element_type=f32) for i in hs]
        a = [jnp.exp(_head_col(g_last_row, hg * hb + i)) for i in hs]
        new_states = tuple(states[i] * a[i] + kv[i] for i in hs)
        for i in hs:
            o = qs[i] + qv[i]
            on = o * lax.rsqrt(jnp.mean(o * o, axis=-1, keepdims=True) + EPS) * gain
            o_ref[pl.ds(r0, C), i * GDN_DV:(i + 1) * GDN_DV] = (on * (z[i] * _sigmoid(z[i]))).astype(o_ref.dtype)
        return new_states

    init = tuple(state_ref[hh] for hh in range(hb))
    final = lax.fori_loop(0, u_ref.shape[0] // C, body, init)
    for hh in range(hb):
        state_ref[hh] = final[hh]


def _gdn(qkv, proj, gates, decay, gdn_norm, batch, seq):
    t = batch * seq
    bf = jnp.bfloat16
    C = GDN_CHUNK
    hb = min(GDN_WY_HEADS, GDN_HEADS)
    ts = _pick(seq, GDN_WY_ROWS)
    assert GDN_HEADS % hb == 0 and ts % (GDN_WY_CG * C) == 0
    hw = hb * GDN_DK
    kq = GDN_QK // hw
    blk = lambda off: pl.BlockSpec((ts, hw), lambda i, g: (i, off + g))
    oblk = pl.BlockSpec((ts, hw), lambda i, g: (i, g))
    cblk = pl.BlockSpec((ts, hb * C), lambda i, g: (i, g))
    u, w, qd, kd, qk = pl.pallas_call(
        functools.partial(_gdn_wy_kernel, hb=hb), grid=(t // ts, GDN_HEADS // hb),
        in_specs=[blk(0), blk(kq), blk(2 * kq),
                  pl.BlockSpec((ts, gates.shape[1]), lambda i, g: (i, 0)), cblk],
        out_specs=[oblk, oblk, oblk, oblk, cblk],
        out_shape=[jax.ShapeDtypeStruct((t, GDN_V), jnp.float32), jax.ShapeDtypeStruct((t, GDN_QK), bf),
                   jax.ShapeDtypeStruct((t, GDN_QK), bf), jax.ShapeDtypeStruct((t, GDN_QK), bf),
                   jax.ShapeDtypeStruct((t, GDN_HEADS * C), bf)],
        compiler_params=_params(("parallel", "parallel")), name="gdn_wy")(qkv, qkv, qkv, gates, decay)
    hb = min(GDN_SCAN_HEADS, GDN_HEADS)
    ts = _pick(seq, GDN_SCAN_ROWS)
    assert GDN_HEADS % hb == 0
    nsb = seq // ts
    hw = hb * GDN_DK
    zoff = OFF_GZ // hw
    blk = pl.BlockSpec((ts, hw), lambda b, g, s: (b * nsb + s, g))
    return pl.pallas_call(
        functools.partial(_gdn_scan_kernel, hb=hb), grid=(batch, GDN_HEADS // hb, nsb),
        in_specs=[blk, blk, blk, blk,
                  pl.BlockSpec((ts, hb * C), lambda b, g, s: (b * nsb + s, g)),
                  pl.BlockSpec((ts, hw), lambda b, g, s: (b * nsb + s, zoff + g)),
                  pl.BlockSpec((ts, gates.shape[1]), lambda b, g, s: (b * nsb + s, 0)),
                  pl.BlockSpec((1, GDN_DV), lambda b, g, s: (0, 0))],
        out_specs=blk,
        out_shape=jax.ShapeDtypeStruct((t, GDN_V), bf),
        scratch_shapes=[pltpu.VMEM((hb, GDN_DK, GDN_DV), jnp.float32)],
        compiler_params=_params(("parallel", "parallel", "arbitrary")), name="gdn_scan")(
            u, w, qd, kd, qk, proj, gates, gdn_norm.reshape(1, GDN_DV).astype(jnp.float32))


def _dsa_prep_kernel(aq_ref, ak_ref, iq_ref, ik_ref, sm_ref, cos_ref, slo_ref, shi_ref, gq_ref, gk_ref,
                     q_out, k_out, iq_out, ik_out, w_out):
    cosf, slo, shi = cos_ref[...], slo_ref[...], shi_ref[...]
    half = ROT_DIM // 2

    xs = ([aq_ref[:, h * HEAD_DIM:(h + 1) * HEAD_DIM].astype(jnp.float32) for h in range(ATT_HEADS)]
          + [ak_ref[:, h * HEAD_DIM:(h + 1) * HEAD_DIM].astype(jnp.float32) for h in range(ATT_KV_HEADS)])
    gains = [gq_ref[...]] * ATT_HEADS + [gk_ref[...]] * ATT_KV_HEADS
    ms = [jnp.mean(x * x, axis=-1, keepdims=True) for x in xs]
    xn = [x * lax.rsqrt(m + EPS) * g for x, m, g in zip(xs, ms, gains)]
    lo = [pltpu.roll(x, LANES - half, 1) for x in xn]
    hi = [pltpu.roll(x, half, 1) for x in xn]
    ro = [x * cosf + a * slo + b * shi for x, a, b in zip(xn, lo, hi)]
    for h in range(ATT_HEADS):
        q_out[h * Q_BLOCK:(h + 1) * Q_BLOCK, :] = (ro[h] * (HEAD_DIM ** -0.5 * math.log2(math.e))).astype(q_out.dtype)
    for h in range(ATT_KV_HEADS):
        k_out[:, h * HEAD_DIM:(h + 1) * HEAD_DIM] = ro[ATT_HEADS + h].astype(k_out.dtype)
    r = lax.broadcasted_iota(jnp.int32, (IDX_DIM, IDX_DIM), 0)
    c = lax.broadcasted_iota(jnp.int32, (IDX_DIM, IDX_DIM), 1)
    swap = jnp.where(((c < half) & (r == c + half)) | ((c >= half) & (c < 2 * half) & (r == c - half)),
                     1.0, 0.0).astype(jnp.bfloat16)
    sin_signed = slo + shi
    xs = jnp.concatenate([iq_ref[:, h * IDX_DIM:(h + 1) * IDX_DIM] for h in range(IDX_HEADS)]
                         + [ik_ref[...]], axis=0)
    sw = jnp.dot(xs, swap, preferred_element_type=jnp.float32)
    nh = IDX_HEADS + 1
    rot = (xs.astype(jnp.float32).reshape(nh, Q_BLOCK, IDX_DIM) * cosf[None]
           + sw.reshape(nh, Q_BLOCK, IDX_DIM) * sin_signed[None]).reshape(nh * Q_BLOCK, IDX_DIM)
    iq_out[...] = rot[:IDX_HEADS * Q_BLOCK].astype(iq_out.dtype)
    ik_out[...] = rot[IDX_HEADS * Q_BLOCK:].astype(ik_out.dtype)
    w_out[...] = sm_ref[...] * (IDX_HEADS ** -0.5 * IDX_DIM ** -0.5)


def _rope_tables(seq):
    half = ROT_DIM // 2
    inv_freq = ROPE_THETA ** (-jnp.arange(half, dtype=jnp.float32) * 2.0 / ROT_DIM)
    ang = jnp.arange(seq, dtype=jnp.float32)[:, None] * inv_freq[None, :]
    cos, sin = jnp.cos(ang), jnp.sin(ang)
    z = jnp.zeros((seq, LANES - 2 * half), jnp.float32)
    zh = jnp.zeros((seq, half), jnp.float32)
    cosf = jnp.concatenate([cos, cos, jnp.ones_like(z)], axis=1)
    sin_lo = jnp.concatenate([-sin, zh, z], axis=1)
    sin_hi = jnp.concatenate([zh, sin, z], axis=1)
    return cosf, sin_lo, sin_hi


def _dsa_prep(proj, small, att_q_norm, att_k_norm, batch, seq):
    assert ROT_DIM == IDX_ROT
    t = batch * seq
    nq = seq // Q_BLOCK
    qb = Q_BLOCK
    cosf, slo, shi = _rope_tables(seq)
    tab = pl.BlockSpec((qb, LANES), lambda b, i: (i, 0))
    row = lambda w, off: pl.BlockSpec((qb, w), lambda b, i: (b * nq + i, off // w))
    vec = pl.BlockSpec((1, HEAD_DIM), lambda b, i: (0, 0))
    aq_w, ak_w = ATT_HEADS * HEAD_DIM, ATT_KV_HEADS * HEAD_DIM
    return pl.pallas_call(
        _dsa_prep_kernel, grid=(batch, nq),
        in_specs=[row(aq_w, OFF_AQ), row(ak_w, OFF_AK), row(IQ_W, OFF_IQ), row(IDX_DIM, OFF_IK),
                  pl.BlockSpec((qb, small.shape[1]), lambda b, i: (b * nq + i, 0)),
                  tab, tab, tab, vec, vec],
        out_specs=[pl.BlockSpec((ATT_HEADS * qb, HEAD_DIM), lambda b, i: (b * nq + i, 0)),
                   pl.BlockSpec((qb, ak_w), lambda b, i: (b * nq + i, 0)),
                   pl.BlockSpec((IDX_HEADS * qb, IDX_DIM), lambda b, i: (b * nq + i, 0)),
                   pl.BlockSpec((qb, IDX_DIM), lambda b, i: (b * nq + i, 0)),
                   pl.BlockSpec((qb, small.shape[1]), lambda b, i: (b * nq + i, 0))],
        out_shape=[jax.ShapeDtypeStruct((t * ATT_HEADS, HEAD_DIM), jnp.bfloat16),
                   jax.ShapeDtypeStruct((t, ak_w), jnp.bfloat16),
                   jax.ShapeDtypeStruct((t * IDX_HEADS, IDX_DIM), jnp.bfloat16),
                   jax.ShapeDtypeStruct((t, IDX_DIM), jnp.bfloat16),
                   jax.ShapeDtypeStruct((t, small.shape[1]), jnp.float32)],
        compiler_params=_params(("parallel", "parallel")), name="dsa_prep")(
            proj, proj, proj, proj, small, cosf, slo, shi,
            att_q_norm.reshape(1, HEAD_DIM).astype(jnp.float32),
            att_k_norm.reshape(1, HEAD_DIM).astype(jnp.float32))


DSA_KB = 512
DSA_SB = 512
DSA_HEAD_GROUPS = 8
INT_MIN = -2 ** 31


def _dsa_kernel(iq_ref, q_ref, w_ref, ik_ref, k_ref, v_ref, o_ref,
                key_ref, keyt_ref, wb_ref, m_ref, l_ref, acc_ref, *, topk):
    qi = pl.program_id(1)
    QB, KB, SB = Q_BLOCK, DSA_KB, DSA_SB
    n_kb = (qi * QB + QB + KB - 1) // KB
    qpos = qi * QB + lax.broadcasted_iota(jnp.int32, (QB, SB), 0)
    kiota = lax.broadcasted_iota(jnp.int32, (QB, SB), 1)

    w = w_ref[...]
    for h in range(IDX_HEADS):
        wb_ref[h] = jnp.broadcast_to(w[:, SM_IW + h:SM_IW + h + 1], (QB, LANES))

    def score_block(sb, carry):
        k0 = pl.multiple_of(sb * SB, SB)
        ikb = ik_ref[pl.ds(k0, SB), :]
        hg = IDX_HEADS // DSA_HEAD_GROUPS
        lgs = [lax.dot_general(iq_ref[g * hg * QB:(g + 1) * hg * QB, :], ikb, (((1,), (1,)), ((), ())),
                               preferred_element_type=jnp.float32) for g in range(DSA_HEAD_GROUPS)]
        sc = None
        for g, lg in enumerate(lgs):
            for hh in range(hg):
                wb = wb_ref[g * hg + hh]
                term = jnp.maximum(lg[hh * QB:(hh + 1) * QB], 0.0) * jnp.concatenate([wb] * (SB // LANES), axis=1)
                sc = term if sc is None else sc + term
        bits = pltpu.bitcast(sc + 0.0, jnp.int32)
        keys = bits ^ ((bits >> 31) & 0x7FFFFFFF)
        keys = jnp.where(k0 + kiota <= qpos, keys, INT_MIN)
        key_ref[:, pl.ds(k0, SB)] = keys
        keyt_ref[pl.ds(k0, SB), :] = keys.T
        return carry

    n_sb = (qi * QB + QB + SB - 1) // SB
    lax.fori_loop(0, n_sb, score_block, 0)

    @pl.when(n_sb < n_kb * (KB // SB))
    def _():
        k0 = pl.multiple_of(n_sb * SB, SB)
        key_ref[:, pl.ds(k0, SB)] = jnp.full((QB, SB), INT_MIN, jnp.int32)
        keyt_ref[pl.ds(k0, SB), :] = jnp.full((SB, QB), INT_MIN, jnp.int32)

    def count_ge(cand):
        def blk(kb, cnt):
            k0 = pl.multiple_of(kb * KB, KB)
            parts = [jnp.where(keyt_ref[pl.ds(k0 + 8 * j, 8), :] >= cand, 1.0, 0.0) for j in range(KB // 8)]
            while len(parts) > 1:
                parts = [parts[j] + parts[j + 1] for j in range(0, len(parts), 2)]
            return cnt + parts[0]
        cnt = lax.fori_loop(0, n_kb, blk, jnp.zeros((8, LANES), jnp.float32))
        return jnp.sum(cnt, axis=0, keepdims=True)

    def to_cols(x):
        return jnp.concatenate([x] * (QB // 8), axis=0).T

    tile = lambda x: jnp.broadcast_to(x, (8, LANES))
    cnt0 = tile(count_ge(jnp.zeros((8, LANES), jnp.int32)))
    thr0 = jnp.where(cnt0 >= topk, 0, jnp.full((8, LANES), INT_MIN, jnp.int32))

    def bit_step(i, carry):
        thr, cnt_thr = carry
        cand = thr | (jnp.int32(1) << (30 - i))
        cnt = tile(count_ge(cand))
        return jnp.where(cnt >= topk, cand, thr), jnp.where(cnt >= topk, cnt, cnt_thr)

    thr, cnt_thr = lax.fori_loop(0, 31, bit_step, (thr0, cnt0))

    surplus = jnp.where((thr > INT_MIN) & (cnt_thr > topk), 1.0, 0.0)

    @pl.when(jnp.max(surplus) > 0.0)
    def _():
        need = topk - tile(count_ge(thr + 1))
        sub8 = lax.broadcasted_iota(jnp.int32, (8, LANES), 0)

        def count_tied_before(idx):
            def blk(kb, cnt):
                k0 = pl.multiple_of(kb * KB, KB)
                for j in range(KB // 8):
                    kk = keyt_ref[pl.ds(k0 + 8 * j, 8), :]
                    hit = jnp.where(kk == thr, jnp.where(k0 + 8 * j + sub8 < idx, 1.0, 0.0), 0.0)
                    cnt = cnt + hit
                return cnt
            cnt = lax.fori_loop(0, n_kb, blk, jnp.zeros((8, LANES), jnp.float32))
            return tile(jnp.sum(cnt, axis=0, keepdims=True))

        n_bits = (key_ref.shape[1] - 1).bit_length()

        def idx_step(i, last):
            cand = last | (jnp.int32(1) << (n_bits - 1 - i))
            return jnp.where(count_tied_before(cand) < need, cand, last)

        last = lax.fori_loop(0, n_bits, idx_step, jnp.zeros((8, LANES), jnp.int32))
        thr_c = jnp.concatenate([to_cols(thr)] * (KB // LANES), axis=1)
        last_c = jnp.concatenate([to_cols(last)] * (KB // LANES), axis=1)
        flag_c = jnp.concatenate([to_cols(surplus)] * (KB // LANES), axis=1)
        pos = lax.broadcasted_iota(jnp.int32, (QB, KB), 1)

        def demote(kb, carry):
            k0 = pl.multiple_of(kb * KB, KB)
            kk = key_ref[:, pl.ds(k0, KB)]
            drop = jnp.where(kk == thr_c, jnp.where(k0 + pos > last_c, flag_c, 0.0), 0.0)
            key_ref[:, pl.ds(k0, KB)] = jnp.where(drop > 0.0, thr_c - 1, kk)
            return carry

        lax.fori_loop(0, n_kb, demote, 0)

    thr = to_cols(jnp.maximum(thr, INT_MIN + 1))

    G = ATT_GROUP
    NG = KB // LANES
    m_ref[...] = jnp.full(m_ref.shape, NEG_BIG, jnp.float32)
    l_ref[...] = jnp.zeros(l_ref.shape, jnp.float32)
    acc_ref[...] = jnp.zeros(acc_ref.shape, jnp.float32)

    def lane_fold(x, op):
        parts = [x[:, g * LANES:(g + 1) * LANES] for g in range(NG)]
        while len(parts) > 1:
            parts = [op(parts[i], parts[i + 1]) for i in range(0, len(parts), 2)]
        return parts[0]

    def attn_block(kb, carry):
        k0 = pl.multiple_of(kb * KB, KB)
        bias = jnp.where(key_ref[:, pl.ds(k0, KB)] >= jnp.concatenate([thr] * NG, axis=1), 0.0, NEG_BIG)
        bias_g = jnp.concatenate([bias] * G, axis=0)
        grp = range(ATT_KV_HEADS)
        rows = [slice(hk * G * QB, (hk + 1) * G * QB) for hk in grp]
        cols = [slice(hk * HEAD_DIM, (hk + 1) * HEAD_DIM) for hk in grp]
        s = [lax.dot_general(q_ref[rows[i], :], k_ref[pl.ds(k0, KB), cols[i]], (((1,), (1,)), ((), ())),
                             preferred_element_type=jnp.float32) for i in grp]
        p, alpha, m_new = [], [], []
        for i in grp:
            si = s[i] + bias_g
            m_old = m_ref[rows[i], :]
            mi = jnp.maximum(m_old, jnp.max(lane_fold(si, jnp.maximum), axis=1, keepdims=True))
            p.append(jnp.exp2(si - jnp.concatenate([mi] * NG, axis=1)).astype(jnp.bfloat16))
            alpha.append(jnp.exp2(m_old - mi))
            m_new.append(mi)
        ones = jnp.ones((KB, LANES), jnp.bfloat16)
        pv = [jnp.dot(p[i], jnp.concatenate([v_ref[pl.ds(k0, KB), cols[i]], ones], axis=1),
                      preferred_element_type=jnp.float32) for i in grp]
        for i in grp:
            l_ref[rows[i], :] = alpha[i] * l_ref[rows[i], :] + pv[i][:, HEAD_DIM:]
            acc_ref[rows[i], :] = alpha[i] * acc_ref[rows[i], :] + pv[i][:, :HEAD_DIM]
            m_ref[rows[i], :] = m_new[i]
        return carry

    lax.fori_loop(0, n_kb, attn_block, 0)
    for h in range(ATT_HEADS):
        rows = slice(h * QB, (h + 1) * QB)
        o_ref[:, h * HEAD_DIM:(h + 1) * HEAD_DIM] = (acc_ref[rows, :] / l_ref[rows, :]).astype(o_ref.dtype)


def _dsa(iq_hm, q_hm, w_s, ik_rot, k_att, proj, batch, seq, topk):
    nq = seq // Q_BLOCK
    qb = Q_BLOCK
    kvw = ATT_KV_HEADS * HEAD_DIM
    return pl.pallas_call(
        functools.partial(_dsa_kernel, topk=topk), grid=(batch, nq),
        in_specs=[pl.BlockSpec((IDX_HEADS * qb, IDX_DIM), lambda b, i: (b * nq + i, 0)),
                  pl.BlockSpec((ATT_HEADS * qb, HEAD_DIM), lambda b, i: (b * nq + i, 0)),
                  pl.BlockSpec((qb, w_s.shape[1]), lambda b, i: (b * nq + i, 0)),
                  pl.BlockSpec((seq, IDX_DIM), lambda b, i: (b, 0)),
                  pl.BlockSpec((seq, kvw), lambda b, i: (b, 0)),
                  pl.BlockSpec((seq, kvw), lambda b, i: (b, OFF_AV // kvw))],
        out_specs=pl.BlockSpec((qb, ATT_HEADS * HEAD_DIM), lambda b, i: (b * nq + i, 0)),
        out_shape=jax.ShapeDtypeStruct((batch * seq, ATT_HEADS * HEAD_DIM), jnp.bfloat16),
        scratch_shapes=[pltpu.VMEM((qb, seq), jnp.int32),
                        pltpu.VMEM((seq, qb), jnp.int32),
                        pltpu.VMEM((IDX_HEADS, qb, LANES), jnp.float32),
                        pltpu.VMEM((ATT_HEADS * qb, LANES), jnp.float32),
                        pltpu.VMEM((ATT_HEADS * qb, LANES), jnp.float32),
                        pltpu.VMEM((ATT_HEADS * qb, HEAD_DIM), jnp.float32)],
        compiler_params=_params(("parallel", "arbitrary")), name="dsa")(
            iq_hm, q_hm, w_s, ik_rot, k_att, proj)


def _mem_block_kernel(x_ref, gin_ref, wq_ref, kv_ref, gq_ref, gk_ref, wo_ref, gout_ref, y_ref, h_ref):
    def norm(t, gain):
        return t * lax.rsqrt(jnp.mean(t * t, axis=-1, keepdims=True) + EPS) * gain

    x = x_ref[...]
    q_all = jnp.dot(norm(x, gin_ref[...]).astype(jnp.bfloat16), wq_ref[...], preferred_element_type=jnp.float32)
    hs = range(MEM_HEADS)
    cols = [slice(h * MEM_HEAD_DIM, (h + 1) * MEM_HEAD_DIM) for h in hs]
    q = [norm(q_all[:, c], gq_ref[...]) * (MEM_HEAD_DIM ** -0.5) for c in cols]
    k = [norm(kv_ref[:, c], gk_ref[...]) for c in cols]
    s = [_bdot_nt(q[h], k[h]) for h in hs]
    p = [jnp.exp(s[h] - jnp.max(s[h], axis=1, keepdims=True)) for h in hs]
    pv = [_bdot(p[h], kv_ref[:, MEM_WIDTH + h * MEM_HEAD_DIM:MEM_WIDTH + (h + 1) * MEM_HEAD_DIM]) for h in hs]
    o = [pv[h] / jnp.sum(p[h], axis=1, keepdims=True) for h in hs]
    mo = jnp.concatenate(o, axis=1).astype(jnp.bfloat16)
    y = x + jnp.dot(mo, wo_ref[...], preferred_element_type=jnp.float32)
    y_ref[...] = y
    h_ref[...] = norm(y, gout_ref[...]).astype(h_ref.dtype)


def _mem_block(x2d, mkv, norm_in, w_q, gq, gk, w_o, norm_out, batch, seq, n_mem):
    t, d = x2d.shape
    tm = _pick(seq, 256)
    nq = seq // tm
    row = pl.BlockSpec((tm, d), lambda b, i: (b * nq + i, 0))
    full = lambda a: pl.BlockSpec(a.shape, lambda b, i: (0, 0))
    vec = lambda v, n: v.reshape(1, n).astype(jnp.float32)
    args = [x2d, vec(norm_in, d), w_q, mkv, vec(gq, MEM_HEAD_DIM), vec(gk, MEM_HEAD_DIM), w_o, vec(norm_out, d)]
    specs = [row, full(args[1]), full(w_q), pl.BlockSpec((n_mem, 2 * MEM_WIDTH), lambda b, i: (b, 0)),
             full(args[4]), full(args[5]), full(w_o), full(args[7])]
    return pl.pallas_call(
        _mem_block_kernel, grid=(batch, nq), in_specs=specs, out_specs=[row, row],
        out_shape=[jax.ShapeDtypeStruct((t, d), jnp.float32), jax.ShapeDtypeStruct((t, d), jnp.bfloat16)],
        compiler_params=_params(("parallel", "parallel")), name="mem_block")(*args)


def _split_w_in(w):
    sizes = (GDN_QK, GDN_QK, GDN_V, GDN_V, GDN_HEADS, GDN_HEADS, ATT_HEADS * HEAD_DIM,
             ATT_KV_HEADS * HEAD_DIM, ATT_KV_HEADS * HEAD_DIM, IQ_W, IDX_DIM, IDX_HEADS)
    offs = np.concatenate([[0], np.cumsum(sizes)])
    wt = w.T
    seg = lambda i: wt[int(offs[i]):int(offs[i + 1]), :]
    gq, gk, gv, gz, ga, gb, aq, ak, av, iq, ik, iw = (seg(i) for i in range(12))
    d = w.shape[0]
    bf = jnp.bfloat16
    main = jnp.concatenate([s.astype(bf) for s in (iq, gq, gk, gv, gz, aq, ak, av, ik)]
                           + [jnp.zeros((N_MAIN_PAD - N_MAIN, d), bf)], axis=0)
    nsm = 2 * GDN_HEADS + IDX_HEADS
    small = jnp.concatenate([ga, gb, iw, jnp.zeros((LANES - nsm, d), w.dtype)], axis=0)
    small_hi = small.astype(bf)
    small_lo = (small - small_hi.astype(jnp.float32)).astype(bf)
    return main, small_hi, small_lo


def kernel(x, mem, norm_mix, w_in, conv_w, a_log, dt_bias, gdn_norm, att_q_norm, att_k_norm, w_out,
           norm_mem_q, norm_mem_kv, w_mem_q, w_mem_kv, mem_q_norm, mem_k_norm, w_mem_o, norm_ffn,
           w_gate, w_up, w_down):
    B, S, D = x.shape
    M = mem.shape[1]
    T = B * S
    topk = min(TOPK_MAX, S // 4)
    bf = jnp.bfloat16
    x2 = x.reshape(T, D)
    for l in range(w_in.shape[0]):
        w_main, *w_small = _split_w_in(w_in[l])
        h, small = _rmsnorm(x2, norm_mix[l], w_small)
        proj = _matmul([(h, w_main)], out_dtype=bf, tm=2048, tn=IN_TN, w_t=True, name="in_proj")
        qkv = _gdn_prep(proj, conv_w[l], S)
        gates, decay = _gate_prep(small, a_log[l], dt_bias[l])
        o_a = _gdn(qkv, proj, gates, decay, gdn_norm[l], B, S)
        q_hm, k_att, iq_hm, ik_rot, w_s = _dsa_prep(proj, small, att_q_norm[l], att_k_norm[l], B, S)
        o_b = _dsa(iq_hm, q_hm, w_s, ik_rot, k_att, proj, B, S, topk)
        wo = w_out[l].astype(bf)
        x2 = _matmul([(o_a, wo[:GDN_V]), (o_b, wo[GDN_V:])], res=x2, tn=1024, name="out_proj")
        hm = _rmsnorm(mem.reshape(B * M, D), norm_mem_kv[l])
        mkv = _matmul([(hm, w_mem_kv[l].astype(bf))], name="mem_kv")
        x2, hf = _mem_block(x2, mkv, norm_mem_q[l], w_mem_q[l].astype(bf), mem_q_norm[l], mem_k_norm[l],
                            w_mem_o[l].astype(bf), norm_ffn[l], B, S, M)
        mid = _gateup(hf, w_gate[l], w_up[l])
        x2 = _matmul([(mid, w_down[l].astype(bf))], res=x2, tm=512, tn=512, name="ffn_down")
    return x2.reshape(B, S, D)
```

```python
import functools
import math

import jax
import jax.numpy as jnp
import numpy as np
from jax import lax
from jax.experimental import pallas as pl
from jax.experimental.pallas import tpu as pltpu

D_MODEL = 4096
ATT_KV_HEADS = 4
IDX_HEADS = 32
TOPK_MAX = 256

EPS = 1e-6
ROPE_THETA = 500000.0
HEAD_DIM = 128
ROT_DIM = HEAD_DIM // 4
GDN_DK = 128
GDN_DV = 128
GDN_HEADS = (D_MODEL // 2) // GDN_DV
GDN_CONV = 4
GDN_CHUNK = 64
GDN_QK = GDN_HEADS * GDN_DK
GDN_V = GDN_HEADS * GDN_DV
ATT_HEADS = (D_MODEL // 2) // HEAD_DIM
ATT_GROUP = ATT_HEADS // ATT_KV_HEADS
IDX_DIM = 128
IDX_ROT = IDX_DIM // 4
Q_BLOCK = 128
MEM_HEADS = 4
MEM_HEAD_DIM = 128
MEM_WIDTH = MEM_HEADS * MEM_HEAD_DIM

LANES = 128
VMEM_LIMIT = 56 * 1024 * 1024
NEG_BIG = -1e30

IQ_W = IDX_HEADS * IDX_DIM
OFF_IQ = 0
OFF_GQ = OFF_IQ + IQ_W
OFF_GK = OFF_GQ + GDN_QK
OFF_GV = OFF_GK + GDN_QK
OFF_GZ = OFF_GV + GDN_V
OFF_AQ = OFF_GZ + GDN_V
OFF_AK = OFF_AQ + ATT_HEADS * HEAD_DIM
OFF_AV = OFF_AK + ATT_KV_HEADS * HEAD_DIM
OFF_IK = OFF_AV + ATT_KV_HEADS * HEAD_DIM
N_MAIN = OFF_IK + IDX_DIM
IN_TN = 512
N_MAIN_PAD = -(-N_MAIN // IN_TN) * IN_TN
SM_GB = GDN_HEADS
SM_IW = 2 * GDN_HEADS
HI = lax.Precision.HIGHEST


def _params(sem):
    return pltpu.CompilerParams(dimension_semantics=sem, vmem_limit_bytes=VMEM_LIMIT)


def _pick(n, pref):
    t = min(n, pref)
    while n % t:
        t //= 2
    return t


def _sigmoid(x):
    return 1.0 / (1.0 + jnp.exp2(x * (-math.log2(math.e))))


def _rmsnorm_kernel(x_ref, g_ref, o_ref):
    x = x_ref[...]
    ms = jnp.mean(x * x, axis=-1, keepdims=True)
    o_ref[...] = (x * lax.rsqrt(ms + EPS) * g_ref[...]).astype(o_ref.dtype)


def _rmsnorm_side_kernel(x_ref, g_ref, whi_ref, wlo_ref, o_ref, s_ref):
    x = x_ref[...]
    ms = jnp.mean(x * x, axis=-1, keepdims=True)
    xn = x * lax.rsqrt(ms + EPS) * g_ref[...]
    hi = xn.astype(o_ref.dtype)
    o_ref[...] = hi
    lo = (xn - hi.astype(jnp.float32)).astype(hi.dtype)
    dot_t = lambda a, wt: lax.dot_general(a, wt, (((1,), (1,)), ((), ())), preferred_element_type=jnp.float32)
    s_ref[...] = dot_t(hi, whi_ref[...]) + dot_t(lo, whi_ref[...]) + dot_t(hi, wlo_ref[...])


def _rmsnorm(x2d, g, w_side=None, tm=256):
    m, d = x2d.shape
    tm = _pick(m, tm)
    g2 = g.reshape(1, d).astype(jnp.float32)
    xs = pl.BlockSpec((tm, d), lambda i: (i, 0))
    gs = pl.BlockSpec((1, d), lambda i: (0, 0))
    if w_side is None:
        return pl.pallas_call(
            _rmsnorm_kernel, grid=(m // tm,), in_specs=[xs, gs], out_specs=xs,
            out_shape=jax.ShapeDtypeStruct((m, d), jnp.bfloat16),
            compiler_params=_params(("parallel",)), name="rmsnorm")(x2d, g2)
    w_hi, w_lo = w_side
    ns = w_hi.shape[0]
    ws = pl.BlockSpec((ns, d), lambda i: (0, 0))
    return pl.pallas_call(
        _rmsnorm_side_kernel, grid=(m // tm,),
        in_specs=[xs, gs, ws, ws],
        out_specs=[xs, pl.BlockSpec((tm, ns), lambda i: (i, 0))],
        out_shape=[jax.ShapeDtypeStruct((m, d), jnp.bfloat16),
                   jax.ShapeDtypeStruct((m, ns), jnp.float32)],
        compiler_params=_params(("parallel",)), name="rmsnorm_side")(x2d, g2, w_hi, w_lo)


def _matmul_kernel(*refs, n_pairs, has_res, w_t):
    o_ref = refs[-1]
    rhs_dim = 1 if w_t else 0
    acc = None
    for p in range(n_pairs):
        d = lax.dot_general(refs[2 * p][...], refs[2 * p + 1][...], (((1,), (rhs_dim,)), ((), ())),
                            preferred_element_type=jnp.float32)
        acc = d if acc is None else acc + d
    if has_res:
        acc = acc + refs[2 * n_pairs][...]
    o_ref[...] = acc.astype(o_ref.dtype)


def _matmul(pairs, res=None, out_dtype=jnp.float32, tm=1024, tn=512, w_t=False, single_buffer_rows=False,
            name="matmul"):
    m = pairs[0][0].shape[0]
    n = pairs[0][1].shape[0 if w_t else 1]
    tm, tn = _pick(m, tm), _pick(n, tn)
    args, specs = [], []
    for a, w in pairs:
        k = a.shape[1]
        args += [a, w]
        wspec = pl.BlockSpec((tn, k), lambda i, j: (j, 0)) if w_t else pl.BlockSpec((k, tn), lambda i, j: (0, j))
        aspec = (pl.BlockSpec((tm, k), lambda i, j: (i, 0), pipeline_mode=pl.Buffered(1)) if single_buffer_rows
                 else pl.BlockSpec((tm, k), lambda i, j: (i, 0)))
        specs += [aspec, wspec]
    if res is not None:
        args.append(res)
        specs.append(pl.BlockSpec((tm, tn), lambda i, j: (i, j)))
    return pl.pallas_call(
        functools.partial(_matmul_kernel, n_pairs=len(pairs), has_res=res is not None, w_t=w_t),
        grid=(m // tm, n // tn), in_specs=specs,
        out_specs=pl.BlockSpec((tm, tn), lambda i, j: (i, j)),
        out_shape=jax.ShapeDtypeStruct((m, n), out_dtype),
        compiler_params=_params(("parallel", "arbitrary")), name=name)(*args)


def _gateup_kernel(a_ref, wg_ref, wu_ref, o_ref):
    a = a_ref[...]
    g = jnp.dot(a, wg_ref[...].astype(a.dtype), preferred_element_type=jnp.float32)
    u = jnp.dot(a, wu_ref[...].astype(a.dtype), preferred_element_type=jnp.float32)
    o_ref[...] = (g * _sigmoid(g) * u).astype(o_ref.dtype)


def _gateup(a, wg, wu, tm=1024, tn=256):
    m, k = a.shape
    n = wg.shape[1]
    tm, tn = _pick(m, tm), _pick(n, tn)
    ws = pl.BlockSpec((k, tn), lambda i, j: (0, j))
    return pl.pallas_call(
        _gateup_kernel, grid=(m // tm, n // tn),
        in_specs=[pl.BlockSpec((tm, k), lambda i, j: (i, 0)), ws, ws],
        out_specs=pl.BlockSpec((tm, tn), lambda i, j: (i, j)),
        out_shape=jax.ShapeDtypeStruct((m, n), jnp.bfloat16),
        compiler_params=_params(("parallel", "arbitrary")), name="ffn_gateup")(a, wg, wu)


HALO = 16


def _gdn_prep_kernel(x_ref, halo_ref, w_ref, o_ref, *, ts, tc, blocks_per_seq, nqk_blocks):
    i = pl.program_id(0)
    j = pl.program_id(1)
    first = (i % blocks_per_seq) == 0
    x = x_ref[...].astype(jnp.float32)
    halo8 = jnp.where(first, 0.0, halo_ref[HALO - 8:HALO, :].astype(jnp.float32))
    w = w_ref[...]
    row8 = lax.broadcasted_iota(jnp.int32, (8, tc), 0)
    y = x * w[GDN_CONV - 1:GDN_CONV, :]
    for d in range(1, GDN_CONV):
        xs = pltpu.roll(x, d, 0)
        head = jnp.where(row8 < d, pltpu.roll(halo8, d, 0), xs[0:8])
        xs = jnp.concatenate([head, xs[8:]], axis=0)
        y = y + xs * w[GDN_CONV - 1 - d:GDN_CONV - d, :]
    y = y * _sigmoid(y)
    is_q = j < nqk_blocks
    is_v = j >= 2 * nqk_blocks
    for h in range(tc // GDN_DK):
        yh = y[:, h * GDN_DK:(h + 1) * GDN_DK]
        rs = lax.rsqrt(jnp.sum(yh * yh, axis=-1, keepdims=True) + EPS)
        rs = rs * jnp.where(is_q, GDN_DK ** -0.5, 1.0)
        fac = jnp.where(is_v, 1.0, rs)
        o_ref[:, h * GDN_DK:(h + 1) * GDN_DK] = (yh * fac).astype(o_ref.dtype)


def _gdn_prep(proj, conv_w, seq):
    t = proj.shape[0]
    ts = _pick(seq, 512)
    tc = 512
    nconv = (2 * GDN_QK + GDN_V) // tc
    col0 = OFF_GQ // tc
    rb = ts // HALO
    kern = functools.partial(_gdn_prep_kernel, ts=ts, tc=tc, blocks_per_seq=seq // ts,
                             nqk_blocks=GDN_QK // tc)
    return pl.pallas_call(
        kern, grid=(t // ts, nconv),
        in_specs=[pl.BlockSpec((ts, tc), lambda i, j: (i, col0 + j)),
                  pl.BlockSpec((HALO, tc), lambda i, j: (jnp.maximum(i * rb - 1, 0), col0 + j)),
                  pl.BlockSpec((GDN_CONV, tc), lambda i, j: (0, j))],
        out_specs=pl.BlockSpec((ts, tc), lambda i, j: (i, j)),
        out_shape=jax.ShapeDtypeStruct((t, nconv * tc), jnp.bfloat16),
        compiler_params=_params(("parallel", "parallel")), name="gdn_prep")(proj, proj, conv_w)


def _gate_prep_kernel(s_ref, alog_ref, dtb_ref, o_ref, dec_ref, *, ts):
    C = GDN_CHUNK
    s = s_ref[...]
    lane = lax.broadcasted_iota(jnp.int32, s.shape, 1)
    xa = s + dtb_ref[...]
    softplus = jnp.maximum(xa, 0.0) + jnp.log(1.0 + jnp.exp(-jnp.abs(xa)))
    g = -jnp.exp(alog_ref[...]) * softplus
    g = jnp.where(lane < GDN_HEADS, g, 0.0)
    r = lax.broadcasted_iota(jnp.int32, (ts, ts), 0)
    c = lax.broadcasted_iota(jnp.int32, (ts, ts), 1)
    tri = jnp.where((r // C == c // C) & (c <= r), 1.0, 0.0)
    gcum = jnp.dot(tri, g, precision=HI, preferred_element_type=jnp.float32)
    beta = _sigmoid(s)
    is_beta = (lane >= SM_GB) & (lane < SM_GB + GDN_HEADS)
    o_ref[...] = jnp.where(lane < GDN_HEADS, gcum, jnp.where(is_beta, beta, 0.0))
    gcum_t = gcum.T
    incl = lax.broadcasted_iota(jnp.int32, (C, C), 1) <= lax.broadcasted_iota(jnp.int32, (C, C), 0)
    sub = lax.broadcasted_iota(jnp.int32, (8, C), 0)
    hc = [(h, ci) for h in range(GDN_HEADS) for ci in range(ts // C)]
    cols = [jnp.broadcast_to(gcum[ci * C:(ci + 1) * C, h:h + 1], (C, C)) for h, ci in hc]
    tiles = [gcum_t[8 * (h // 8):8 * (h // 8) + 8, ci * C:(ci + 1) * C] for h, ci in hc]
    rows = [jnp.sum(jnp.where(sub == h % 8, t, 0.0), axis=0, keepdims=True) for (h, ci), t in zip(hc, tiles)]
    for (h, ci), col, row in zip(hc, cols, rows):
        dec_ref[ci * C:(ci + 1) * C, h * C:(h + 1) * C] = jnp.exp(jnp.where(incl, col - row, NEG_BIG))


def _gate_prep(small, a_log, dt_bias):
    t, ns = small.shape
    ts = _pick(t, 256)
    pad = lambda v: jnp.zeros((1, ns), jnp.float32).at[0, :GDN_HEADS].set(v.astype(jnp.float32))
    vs = pl.BlockSpec((1, ns), lambda i: (0, 0))
    bs = pl.BlockSpec((ts, ns), lambda i: (i, 0))
    dw = GDN_HEADS * GDN_CHUNK
    return pl.pallas_call(
        functools.partial(_gate_prep_kernel, ts=ts), grid=(t // ts,),
        in_specs=[bs, vs, vs], out_specs=[bs, pl.BlockSpec((ts, dw), lambda i: (i, 0))],
        out_shape=[jax.ShapeDtypeStruct((t, ns), jnp.float32), jax.ShapeDtypeStruct((t, dw), jnp.float32)],
        compiler_params=_params(("parallel",)), name="gate_prep")(small, pad(a_log), pad(dt_bias))


GDN_WY_HEADS = 4
GDN_WY_ROWS = 512
GDN_WY_CG = 8
GDN_SCAN_HEADS = 16
GDN_SCAN_ROWS = 512


def _bdot(a, b):
    return jnp.dot(a.astype(jnp.bfloat16), b.astype(jnp.bfloat16), preferred_element_type=jnp.float32)


def _bdot_nt(a, b):
    return lax.dot_general(a.astype(jnp.bfloat16), b.astype(jnp.bfloat16), (((1,), (1,)), ((), ())),
                           preferred_element_type=jnp.float32)


def _head_col(gsl, idx):
    lane = lax.broadcasted_iota(jnp.int32, (1, LANES), 1)
    return jnp.sum(gsl * jnp.where(lane == idx, 1.0, 0.0), axis=1, keepdims=True)


def _gdn_wy_kernel(q_ref, k_ref, v_ref, g_ref, dec_ref, u_ref, w_ref, qd_ref, kd_ref, qk_ref, *, hb):
    hg = pl.program_id(1)
    C, CG = GDN_CHUNK, GDN_WY_CG
    r = lax.broadcasted_iota(jnp.int32, (C, C), 0)
    c = lax.broadcasted_iota(jnp.int32, (C, C), 1)
    off_diag = c != r

    def body(it, carry):
        where, q, k, v, gc, beta, decay = [], [], [], [], [], [], []
        for cc in range(CG):
            r0 = pl.multiple_of((it * CG + cc) * C, C)
            gsl = g_ref[pl.ds(r0, C), :]
            for hh in range(hb):
                sl = slice(hh * GDN_DK, (hh + 1) * GDN_DK)
                h = hg * hb + hh
                where.append((r0, hh))
                q.append(q_ref[pl.ds(r0, C), sl].astype(jnp.float32))
                k.append(k_ref[pl.ds(r0, C), sl].astype(jnp.float32))
                v.append(v_ref[pl.ds(r0, C), sl].astype(jnp.float32))
                gc.append(_head_col(gsl, h))
                beta.append(_head_col(gsl, SM_GB + h))
                decay.append(dec_ref[pl.ds(r0, C), hh * C:(hh + 1) * C])
        n = len(where)
        ch = range(n)
        kb = [k[i] * beta[i] for i in ch]
        eg = [jnp.exp(gc[i]) for i in ch]
        kk = [_bdot_nt(kb[i], k[i]) for i in ch]
        qk = [_bdot_nt(q[i], k[i]) for i in ch]
        pw = [jnp.where(off_diag, kk[i] * decay[i], 0.0) for i in ch]
        x = [jnp.concatenate([v[i] * beta[i], kb[i] * eg[i]], axis=1) for i in ch]
        n_lvl = int(math.log2(C))
        for lvl in range(n_lvl):
            px = [_bdot(pw[i], x[i]) for i in ch]
            if lvl + 1 < n_lvl:
                pw = [_bdot(pw[i], pw[i]) for i in ch]
            x = [x[i] - px[i] if lvl == 0 else x[i] + px[i] for i in ch]
        for i, (r0, hh) in enumerate(where):
            sl = slice(hh * GDN_DK, (hh + 1) * GDN_DK)
            u_ref[pl.ds(r0, C), sl] = x[i][:, :GDN_DV]
            w_ref[pl.ds(r0, C), sl] = x[i][:, GDN_DV:].astype(w_ref.dtype)
            qd_ref[pl.ds(r0, C), sl] = (q[i] * eg[i]).astype(qd_ref.dtype)
            kd_ref[pl.ds(r0, C), sl] = (k[i] * jnp.exp(gc[i][C - 1:C, :] - gc[i])).astype(kd_ref.dtype)
            qk_ref[pl.ds(r0, C), hh * C:(hh + 1) * C] = (qk[i] * decay[i]).astype(qk_ref.dtype)
        return carry

    lax.fori_loop(0, q_ref.shape[0] // (CG * C), body, 0)


def _gdn_scan_kernel(u_ref, w_ref, qd_ref, kd_ref, qk_ref, z_ref, g_ref, gain_ref, o_ref, state_ref, *, hb):
    hg = pl.program_id(1)
    C = GDN_CHUNK
    sub8 = lax.broadcasted_iota(jnp.int32, (8, LANES), 0)
    gain = gain_ref[...]

    @pl.when(pl.program_id(2) == 0)
    def _():
        state_ref[...] = jnp.zeros(state_ref.shape, state_ref.dtype)

    def body(ci, states):
        r0 = pl.multiple_of(ci * C, C)
        g_tail = g_ref[pl.ds(r0 + C - 8, 8), :]
        g_last_row = jnp.sum(jnp.where(sub8 == 7, g_tail, 0.0), axis=0, keepdims=True)
        loaded = []
        for hh in range(hb):
            sl = slice(hh * GDN_DK, (hh + 1) * GDN_DK)
            loaded.append((u_ref[pl.ds(r0, C), sl], w_ref[pl.ds(r0, C), sl], qd_ref[pl.ds(r0, C), sl],
                           kd_ref[pl.ds(r0, C), sl], qk_ref[pl.ds(r0, C), hh * C:(hh + 1) * C],
                           z_ref[pl.ds(r0, C), sl].astype(jnp.float32)))
        hs = range(hb)
        f32 = jnp.float32
        u, w, qd, kd, qk, z = zip(*loaded)
        sb = [states[i].astype(jnp.bfloat16) for i in hs]
        ws = [jnp.dot(w[i], sb[i], preferred_element_type=f32) for i in hs]
        qs = [jnp.dot(qd[i], sb[i], preferred_element_type=f32) for i in hs]
        vb = [(u[i] - ws[i]).astype(jnp.bfloat16) for i in hs]
        kv = [lax.dot_general(kd[i], vb[i], (((0,), (0,)), ((), ())), preferred_element_type=f32) for i in hs]
        qv = [jnp.dot(qk[i], vb[i], preferred_element_type=f32) for i in hs]
        a = [jnp.exp(_head_col(g_last_row, hg * hb + i)) for i in hs]
        new_states = tuple(states[i] * a[i] + kv[i] for i in hs)
        for i in hs:
            o = qs[i] + qv[i]
            on = o * lax.rsqrt(jnp.mean(o * o, axis=-1, keepdims=True) + EPS) * gain
            o_ref[pl.ds(r0, C), i * GDN_DV:(i + 1) * GDN_DV] = (on * (z[i] * _sigmoid(z[i]))).astype(o_ref.dtype)
        return new_states

    init = tuple(state_ref[hh] for hh in range(hb))
    final = lax.fori_loop(0, u_ref.shape[0] // C, body, init)
    for hh in range(hb):
        state_ref[hh] = final[hh]


def _gdn(qkv, proj, gates, decay, gdn_norm, batch, seq):
    t = batch * seq
    bf = jnp.bfloat16
    C = GDN_CHUNK
    hb = min(GDN_WY_HEADS, GDN_HEADS)
    ts = _pick(seq, GDN_WY_ROWS)
    assert GDN_HEADS % hb == 0 and ts % (GDN_WY_CG * C) == 0
    hw = hb * GDN_DK
    kq = GDN_QK // hw
    blk = lambda off: pl.BlockSpec((ts, hw), lambda i, g: (i, off + g))
    oblk = pl.BlockSpec((ts, hw), lambda i, g: (i, g))
    cblk = pl.BlockSpec((ts, hb * C), lambda i, g: (i, g))
    u, w, qd, kd, qk = pl.pallas_call(
        functools.partial(_gdn_wy_kernel, hb=hb), grid=(t // ts, GDN_HEADS // hb),
        in_specs=[blk(0), blk(kq), blk(2 * kq),
                  pl.BlockSpec((ts, gates.shape[1]), lambda i, g: (i, 0)), cblk],
        out_specs=[oblk, oblk, oblk, oblk, cblk],
        out_shape=[jax.ShapeDtypeStruct((t, GDN_V), jnp.float32), jax.ShapeDtypeStruct((t, GDN_QK), bf),
                   jax.ShapeDtypeStruct((t, GDN_QK), bf), jax.ShapeDtypeStruct((t, GDN_QK), bf),
                   jax.ShapeDtypeStruct((t, GDN_HEADS * C), bf)],
        compiler_params=_params(("parallel", "parallel")), name="gdn_wy")(qkv, qkv, qkv, gates, decay)
    hb = min(GDN_SCAN_HEADS, GDN_HEADS)
    ts = _pick(seq, GDN_SCAN_ROWS)
    assert GDN_HEADS % hb == 0
    nsb = seq // ts
    hw = hb * GDN_DK
    zoff = OFF_GZ // hw
    blk = pl.BlockSpec((ts, hw), lambda b, g, s: (b * nsb + s, g))
    return pl.pallas_call(
        functools.partial(_gdn_scan_kernel, hb=hb), grid=(batch, GDN_HEADS // hb, nsb),
        in_specs=[blk, blk, blk, blk,
                  pl.BlockSpec((ts, hb * C), lambda b, g, s: (b * nsb + s, g)),
                  pl.BlockSpec((ts, hw), lambda b, g, s: (b * nsb + s, zoff + g)),
                  pl.BlockSpec((ts, gates.shape[1]), lambda b, g, s: (b * nsb + s, 0)),
                  pl.BlockSpec((1, GDN_DV), lambda b, g, s: (0, 0))],
        out_specs=blk,
        out_shape=jax.ShapeDtypeStruct((t, GDN_V), bf),
        scratch_shapes=[pltpu.VMEM((hb, GDN_DK, GDN_DV), jnp.float32)],
        compiler_params=_params(("parallel", "parallel", "arbitrary")), name="gdn_scan")(
            u, w, qd, kd, qk, proj, gates, gdn_norm.reshape(1, GDN_DV).astype(jnp.float32))


def _dsa_prep_kernel(aq_ref, ak_ref, iq_ref, ik_ref, sm_ref, cos_ref, slo_ref, shi_ref, gq_ref, gk_ref,
                     q_out, k_out, iq_out, ik_out, w_out):
    cosf, slo, shi = cos_ref[...], slo_ref[...], shi_ref[...]
    half = ROT_DIM // 2

    xs = ([aq_ref[:, h * HEAD_DIM:(h + 1) * HEAD_DIM].astype(jnp.float32) for h in range(ATT_HEADS)]
          + [ak_ref[:, h * HEAD_DIM:(h + 1) * HEAD_DIM].astype(jnp.float32) for h in range(ATT_KV_HEADS)])
    gains = [gq_ref[...]] * ATT_HEADS + [gk_ref[...]] * ATT_KV_HEADS
    ms = [jnp.mean(x * x, axis=-1, keepdims=True) for x in xs]
    xn = [x * lax.rsqrt(m + EPS) * g for x, m, g in zip(xs, ms, gains)]
    lo = [pltpu.roll(x, LANES - half, 1) for x in xn]
    hi = [pltpu.roll(x, half, 1) for x in xn]
    ro = [x * cosf + a * slo + b * shi for x, a, b in zip(xn, lo, hi)]
    for h in range(ATT_HEADS):
        q_out[h * Q_BLOCK:(h + 1) * Q_BLOCK, :] = (ro[h] * (HEAD_DIM ** -0.5 * math.log2(math.e))).astype(q_out.dtype)
    for h in range(ATT_KV_HEADS):
        k_out[:, h * HEAD_DIM:(h + 1) * HEAD_DIM] = ro[ATT_HEADS + h].astype(k_out.dtype)
    r = lax.broadcasted_iota(jnp.int32, (IDX_DIM, IDX_DIM), 0)
    c = lax.broadcasted_iota(jnp.int32, (IDX_DIM, IDX_DIM), 1)
    swap = jnp.where(((c < half) & (r == c + half)) | ((c >= half) & (c < 2 * half) & (r == c - half)),
                     1.0, 0.0).astype(jnp.bfloat16)
    sin_signed = slo + shi
    xs = jnp.concatenate([iq_ref[:, h * IDX_DIM:(h + 1) * IDX_DIM] for h in range(IDX_HEADS)]
                         + [ik_ref[...]], axis=0)
    sw = jnp.dot(xs, swap, preferred_element_type=jnp.float32)
    nh = IDX_HEADS + 1
    rot = (xs.astype(jnp.float32).reshape(nh, Q_BLOCK, IDX_DIM) * cosf[None]
           + sw.reshape(nh, Q_BLOCK, IDX_DIM) * sin_signed[None]).reshape(nh * Q_BLOCK, IDX_DIM)
    iq_out[...] = rot[:IDX_HEADS * Q_BLOCK].astype(iq_out.dtype)
    ik_out[...] = rot[IDX_HEADS * Q_BLOCK:].astype(ik_out.dtype)
    w_out[...] = sm_ref[...] * (IDX_HEADS ** -0.5 * IDX_DIM ** -0.5)


def _rope_tables(seq):
    half = ROT_DIM // 2
    inv_freq = ROPE_THETA ** (-jnp.arange(half, dtype=jnp.float32) * 2.0 / ROT_DIM)
    ang = jnp.arange(seq, dtype=jnp.float32)[:, None] * inv_freq[None, :]
    cos, sin = jnp.cos(ang), jnp.sin(ang)
    z = jnp.zeros((seq, LANES - 2 * half), jnp.float32)
    zh = jnp.zeros((seq, half), jnp.float32)
    cosf = jnp.concatenate([cos, cos, jnp.ones_like(z)], axis=1)
    sin_lo = jnp.concatenate([-sin, zh, z], axis=1)
    sin_hi = jnp.concatenate([zh, sin, z], axis=1)
    return cosf, sin_lo, sin_hi


def _dsa_prep(proj, small, att_q_norm, att_k_norm, batch, seq):
    assert ROT_DIM == IDX_ROT
    t = batch * seq
    nq = seq // Q_BLOCK
    qb = Q_BLOCK
    cosf, slo, shi = _rope_tables(seq)
    tab = pl.BlockSpec((qb, LANES), lambda b, i: (i, 0))
    row = lambda w, off: pl.BlockSpec((qb, w), lambda b, i: (b * nq + i, off // w))
    vec = pl.BlockSpec((1, HEAD_DIM), lambda b, i: (0, 0))
    aq_w, ak_w = ATT_HEADS * HEAD_DIM, ATT_KV_HEADS * HEAD_DIM
    return pl.pallas_call(
        _dsa_prep_kernel, grid=(batch, nq),
        in_specs=[row(aq_w, OFF_AQ), row(ak_w, OFF_AK), row(IQ_W, OFF_IQ), row(IDX_DIM, OFF_IK),
                  pl.BlockSpec((qb, small.shape[1]), lambda b, i: (b * nq + i, 0)),
                  tab, tab, tab, vec, vec],
        out_specs=[pl.BlockSpec((ATT_HEADS * qb, HEAD_DIM), lambda b, i: (b * nq + i, 0)),
                   pl.BlockSpec((qb, ak_w), lambda b, i: (b * nq + i, 0)),
                   pl.BlockSpec((IDX_HEADS * qb, IDX_DIM), lambda b, i: (b * nq + i, 0)),
                   pl.BlockSpec((qb, IDX_DIM), lambda b, i: (b * nq + i, 0)),
                   pl.BlockSpec((qb, small.shape[1]), lambda b, i: (b * nq + i, 0))],
        out_shape=[jax.ShapeDtypeStruct((t * ATT_HEADS, HEAD_DIM), jnp.bfloat16),
                   jax.ShapeDtypeStruct((t, ak_w), jnp.bfloat16),
                   jax.ShapeDtypeStruct((t * IDX_HEADS, IDX_DIM), jnp.bfloat16),
                   jax.ShapeDtypeStruct((t, IDX_DIM), jnp.bfloat16),
                   jax.ShapeDtypeStruct((t, small.shape[1]), jnp.float32)],
        compiler_params=_params(("parallel", "parallel")), name="dsa_prep")(
            proj, proj, proj, proj, small, cosf, slo, shi,
            att_q_norm.reshape(1, HEAD_DIM).astype(jnp.float32),
            att_k_norm.reshape(1, HEAD_DIM).astype(jnp.float32))


DSA_KB = 512
DSA_SB = 512
DSA_HEAD_GROUPS = 8
INT_MIN = -2 ** 31


def _dsa_kernel(iq_ref, q_ref, w_ref, ik_ref, k_ref, v_ref, o_ref,
                key_ref, keyt_ref, wb_ref, m_ref, l_ref, acc_ref, *, topk):
    qi = pl.program_id(1)
    QB, KB, SB = Q_BLOCK, DSA_KB, DSA_SB
    n_kb = (qi * QB + QB + KB - 1) // KB
    qpos = qi * QB + lax.broadcasted_iota(jnp.int32, (QB, SB), 0)
    kiota = lax.broadcasted_iota(jnp.int32, (QB, SB), 1)

    w = w_ref[...]
    for h in range(IDX_HEADS):
        wb_ref[h] = jnp.broadcast_to(w[:, SM_IW + h:SM_IW + h + 1], (QB, LANES))

    def score_block(sb, carry):
        k0 = pl.multiple_of(sb * SB, SB)
        ikb = ik_ref[pl.ds(k0, SB), :]
        hg = IDX_HEADS // DSA_HEAD_GROUPS
        lgs = [lax.dot_general(iq_ref[g * hg * QB:(g + 1) * hg * QB, :], ikb, (((1,), (1,)), ((), ())),
                               preferred_element_type=jnp.float32) for g in range(DSA_HEAD_GROUPS)]
        sc = None
        for g, lg in enumerate(lgs):
            for hh in range(hg):
                wb = wb_ref[g * hg + hh]
                term = jnp.maximum(lg[hh * QB:(hh + 1) * QB], 0.0) * jnp.concatenate([wb] * (SB // LANES), axis=1)
                sc = term if sc is None else sc + term
        bits = pltpu.bitcast(sc + 0.0, jnp.int32)
        keys = bits ^ ((bits >> 31) & 0x7FFFFFFF)
        keys = jnp.where(k0 + kiota <= qpos, keys, INT_MIN)
        key_ref[:, pl.ds(k0, SB)] = keys
        keyt_ref[pl.ds(k0, SB), :] = keys.T
        return carry

    n_sb = (qi * QB + QB + SB - 1) // SB
    lax.fori_loop(0, n_sb, score_block, 0)

    @pl.when(n_sb < n_kb * (KB // SB))
    def _():
        k0 = pl.multiple_of(n_sb * SB, SB)
        key_ref[:, pl.ds(k0, SB)] = jnp.full((QB, SB), INT_MIN, jnp.int32)
        keyt_ref[pl.ds(k0, SB), :] = jnp.full((SB, QB), INT_MIN, jnp.int32)

    def count_ge(cand):
        def blk(kb, cnt):
            k0 = pl.multiple_of(kb * KB, KB)
            parts = [jnp.where(keyt_ref[pl.ds(k0 + 8 * j, 8), :] >= cand, 1.0, 0.0) for j in range(KB // 8)]
            while len(parts) > 1:
                parts = [parts[j] + parts[j + 1] for j in range(0, len(parts), 2)]
            return cnt + parts[0]
        cnt = lax.fori_loop(0, n_kb, blk, jnp.zeros((8, LANES), jnp.float32))
        return jnp.sum(cnt, axis=0, keepdims=True)

    def to_cols(x):
        return jnp.concatenate([x] * (QB // 8), axis=0).T

    tile = lambda x: jnp.broadcast_to(x, (8, LANES))
    cnt0 = tile(count_ge(jnp.zeros((8, LANES), jnp.int32)))
    thr0 = jnp.where(cnt0 >= topk, 0, jnp.full((8, LANES), INT_MIN, jnp.int32))

    def bit_step(i, carry):
        thr, cnt_thr = carry
        cand = thr | (jnp.int32(1) << (30 - i))
        cnt = tile(count_ge(cand))
        return jnp.where(cnt >= topk, cand, thr), jnp.where(cnt >= topk, cnt, cnt_thr)

    thr, cnt_thr = lax.fori_loop(0, 31, bit_step, (thr0, cnt0))

    surplus = jnp.where((thr > INT_MIN) & (cnt_thr > topk), 1.0, 0.0)

    @pl.when(jnp.max(surplus) > 0.0)
    def _():
        need = topk - tile(count_ge(thr + 1))
        sub8 = lax.broadcasted_iota(jnp.int32, (8, LANES), 0)

        def count_tied_before(idx):
            def blk(kb, cnt):
                k0 = pl.multiple_of(kb * KB, KB)
                for j in range(KB // 8):
                    kk = keyt_ref[pl.ds(k0 + 8 * j, 8), :]
                    hit = jnp.where(kk == thr, jnp.where(k0 + 8 * j + sub8 < idx, 1.0, 0.0), 0.0)
                    cnt = cnt + hit
                return cnt
            cnt = lax.fori_loop(0, n_kb, blk, jnp.zeros((8, LANES), jnp.float32))
            return tile(jnp.sum(cnt, axis=0, keepdims=True))

        n_bits = (key_ref.shape[1] - 1).bit_length()

        def idx_step(i, last):
            cand = last | (jnp.int32(1) << (n_bits - 1 - i))
            return jnp.where(count_tied_before(cand) < need, cand, last)

        last = lax.fori_loop(0, n_bits, idx_step, jnp.zeros((8, LANES), jnp.int32))
        thr_c = jnp.concatenate([to_cols(thr)] * (KB // LANES), axis=1)
        last_c = jnp.concatenate([to_cols(last)] * (KB // LANES), axis=1)
        flag_c = jnp.concatenate([to_cols(surplus)] * (KB // LANES), axis=1)
        pos = lax.broadcasted_iota(jnp.int32, (QB, KB), 1)

        def demote(kb, carry):
            k0 = pl.multiple_of(kb * KB, KB)
            kk = key_ref[:, pl.ds(k0, KB)]
            drop = jnp.where(kk == thr_c, jnp.where(k0 + pos > last_c, flag_c, 0.0), 0.0)
            key_ref[:, pl.ds(k0, KB)] = jnp.where(drop > 0.0, thr_c - 1, kk)
            return carry

        lax.fori_loop(0, n_kb, demote, 0)

    thr = to_cols(jnp.maximum(thr, INT_MIN + 1))

    G = ATT_GROUP
    NG = KB // LANES
    m_ref[...] = jnp.full(m_ref.shape, NEG_BIG, jnp.float32)
    l_ref[...] = jnp.zeros(l_ref.shape, jnp.float32)
    acc_ref[...] = jnp.zeros(acc_ref.shape, jnp.float32)

    def lane_fold(x, op):
        parts = [x[:, g * LANES:(g + 1) * LANES] for g in range(NG)]
        while len(parts) > 1:
            parts = [op(parts[i], parts[i + 1]) for i in range(0, len(parts), 2)]
        return parts[0]

    def attn_block(kb, carry):
        k0 = pl.multiple_of(kb * KB, KB)
        bias = jnp.where(key_ref[:, pl.ds(k0, KB)] >= jnp.concatenate([thr] * NG, axis=1), 0.0, NEG_BIG)
        bias_g = jnp.concatenate([bias] * G, axis=0)
        grp = range(ATT_KV_HEADS)
        rows = [slice(hk * G * QB, (hk + 1) * G * QB) for hk in grp]
        cols = [slice(hk * HEAD_DIM, (hk + 1) * HEAD_DIM) for hk in grp]
        s = [lax.dot_general(q_ref[rows[i], :], k_ref[pl.ds(k0, KB), cols[i]], (((1,), (1,)), ((), ())),
                             preferred_element_type=jnp.float32) for i in grp]
        p, alpha, m_new = [], [], []
        for i in grp:
            si = s[i] + bias_g
            m_old = m_ref[rows[i], :]
            mi = jnp.maximum(m_old, jnp.max(lane_fold(si, jnp.maximum), axis=1, keepdims=True))
            p.append(jnp.exp2(si - jnp.concatenate([mi] * NG, axis=1)).astype(jnp.bfloat16))
            alpha.append(jnp.exp2(m_old - mi))
            m_new.append(mi)
        ones = jnp.ones((KB, LANES), jnp.bfloat16)
        pv = [jnp.dot(p[i], jnp.concatenate([v_ref[pl.ds(k0, KB), cols[i]], ones], axis=1),
                      preferred_element_type=jnp.float32) for i in grp]
        for i in grp:
            l_ref[rows[i], :] = alpha[i] * l_ref[rows[i], :] + pv[i][:, HEAD_DIM:]
            acc_ref[rows[i], :] = alpha[i] * acc_ref[rows[i], :] + pv[i][:, :HEAD_DIM]
            m_ref[rows[i], :] = m_new[i]
        return carry

    lax.fori_loop(0, n_kb, attn_block, 0)
    for h in range(ATT_HEADS):
        rows = slice(h * QB, (h + 1) * QB)
        o_ref[:, h * HEAD_DIM:(h + 1) * HEAD_DIM] = (acc_ref[rows, :] / l_ref[rows, :]).astype(o_ref.dtype)


def _dsa(iq_hm, q_hm, w_s, ik_rot, k_att, proj, batch, seq, topk):
    nq = seq // Q_BLOCK
    qb = Q_BLOCK
    kvw = ATT_KV_HEADS * HEAD_DIM
    return pl.pallas_call(
        functools.partial(_dsa_kernel, topk=topk), grid=(batch, nq),
        in_specs=[pl.BlockSpec((IDX_HEADS * qb, IDX_DIM), lambda b, i: (b * nq + i, 0)),
                  pl.BlockSpec((ATT_HEADS * qb, HEAD_DIM), lambda b, i: (b * nq + i, 0)),
                  pl.BlockSpec((qb, w_s.shape[1]), lambda b, i: (b * nq + i, 0)),
                  pl.BlockSpec((seq, IDX_DIM), lambda b, i: (b, 0)),
                  pl.BlockSpec((seq, kvw), lambda b, i: (b, 0)),
                  pl.BlockSpec((seq, kvw), lambda b, i: (b, OFF_AV // kvw))],
        out_specs=pl.BlockSpec((qb, ATT_HEADS * HEAD_DIM), lambda b, i: (b * nq + i, 0)),
        out_shape=jax.ShapeDtypeStruct((batch * seq, ATT_HEADS * HEAD_DIM), jnp.bfloat16),
        scratch_shapes=[pltpu.VMEM((qb, seq), jnp.int32),
                        pltpu.VMEM((seq, qb), jnp.int32),
                        pltpu.VMEM((IDX_HEADS, qb, LANES), jnp.float32),
                        pltpu.VMEM((ATT_HEADS * qb, LANES), jnp.float32),
                        pltpu.VMEM((ATT_HEADS * qb, LANES), jnp.float32),
                        pltpu.VMEM((ATT_HEADS * qb, HEAD_DIM), jnp.float32)],
        compiler_params=_params(("parallel", "arbitrary")), name="dsa")(
            iq_hm, q_hm, w_s, ik_rot, k_att, proj)


def _mem_block_kernel(x_ref, gin_ref, wq_ref, kv_ref, gq_ref, gk_ref, wo_ref, gout_ref, y_ref, h_ref):
    def norm(t, gain):
        return t * lax.rsqrt(jnp.mean(t * t, axis=-1, keepdims=True) + EPS) * gain

    x = x_ref[...]
    q_all = jnp.dot(norm(x, gin_ref[...]).astype(jnp.bfloat16), wq_ref[...], preferred_element_type=jnp.float32)
    hs = range(MEM_HEADS)
    cols = [slice(h * MEM_HEAD_DIM, (h + 1) * MEM_HEAD_DIM) for h in hs]
    q = [norm(q_all[:, c], gq_ref[...]) * (MEM_HEAD_DIM ** -0.5) for c in cols]
    k = [norm(kv_ref[:, c], gk_ref[...]) for c in cols]
    s = [_bdot_nt(q[h], k[h]) for h in hs]
    p = [jnp.exp(s[h] - jnp.max(s[h], axis=1, keepdims=True)) for h in hs]
    pv = [_bdot(p[h], kv_ref[:, MEM_WIDTH + h * MEM_HEAD_DIM:MEM_WIDTH + (h + 1) * MEM_HEAD_DIM]) for h in hs]
    o = [pv[h] / jnp.sum(p[h], axis=1, keepdims=True) for h in hs]
    mo = jnp.concatenate(o, axis=1).astype(jnp.bfloat16)
    y = x + jnp.dot(mo, wo_ref[...], preferred_element_type=jnp.float32)
    y_ref[...] = y
    h_ref[...] = norm(y, gout_ref[...]).astype(h_ref.dtype)


def _mem_block(x2d, mkv, norm_in, w_q, gq, gk, w_o, norm_out, batch, seq, n_mem):
    t, d = x2d.shape
    tm = _pick(seq, 256)
    nq = seq // tm
    row = pl.BlockSpec((tm, d), lambda b, i: (b * nq + i, 0))
    full = lambda a: pl.BlockSpec(a.shape, lambda b, i: (0, 0))
    vec = lambda v, n: v.reshape(1, n).astype(jnp.float32)
    args = [x2d, vec(norm_in, d), w_q, mkv, vec(gq, MEM_HEAD_DIM), vec(gk, MEM_HEAD_DIM), w_o, vec(norm_out, d)]
    specs = [row, full(args[1]), full(w_q), pl.BlockSpec((n_mem, 2 * MEM_WIDTH), lambda b, i: (b, 0)),
             full(args[4]), full(args[5]), full(w_o), full(args[7])]
    return pl.pallas_call(
        _mem_block_kernel, grid=(batch, nq), in_specs=specs, out_specs=[row, row],
        out_shape=[jax.ShapeDtypeStruct((t, d), jnp.float32), jax.ShapeDtypeStruct((t, d), jnp.bfloat16)],
        compiler_params=_params(("parallel", "parallel")), name="mem_block")(*args)


def _split_w_in(w):
    sizes = (GDN_QK, GDN_QK, GDN_V, GDN_V, GDN_HEADS, GDN_HEADS, ATT_HEADS * HEAD_DIM,
             ATT_KV_HEADS * HEAD_DIM, ATT_KV_HEADS * HEAD_DIM, IQ_W, IDX_DIM, IDX_HEADS)
    offs = np.concatenate([[0], np.cumsum(sizes)])
    wt = w.T
    seg = lambda i: wt[int(offs[i]):int(offs[i + 1]), :]
    gq, gk, gv, gz, ga, gb, aq, ak, av, iq, ik, iw = (seg(i) for i in range(12))
    d = w.shape[0]
    bf = jnp.bfloat16
    main = jnp.concatenate([s.astype(bf) for s in (iq, gq, gk, gv, gz, aq, ak, av, ik)]
                           + [jnp.zeros((N_MAIN_PAD - N_MAIN, d), bf)], axis=0)
    nsm = 2 * GDN_HEADS + IDX_HEADS
    small = jnp.concatenate([ga, gb, iw, jnp.zeros((LANES - nsm, d), w.dtype)], axis=0)
    small_hi = small.astype(bf)
    small_lo = (small - small_hi.astype(jnp.float32)).astype(bf)
    return main, small_hi, small_lo


def kernel(x, mem, norm_mix, w_in, conv_w, a_log, dt_bias, gdn_norm, att_q_norm, att_k_norm, w_out,
           norm_mem_q, norm_mem_kv, w_mem_q, w_mem_kv, mem_q_norm, mem_k_norm, w_mem_o, norm_ffn,
           w_gate, w_up, w_down):
    B, S, D = x.shape
    M = mem.shape[1]
    T = B * S
    topk = min(TOPK_MAX, S // 4)
    bf = jnp.bfloat16
    x2 = x.reshape(T, D)
    for l in range(w_in.shape[0]):
        w_main, *w_small = _split_w_in(w_in[l])
        h, small = _rmsnorm(x2, norm_mix[l], w_small)
        proj = _matmul([(h, w_main)], out_dtype=bf, tm=2048, tn=IN_TN, w_t=True, name="in_proj")
        qkv = _gdn_prep(proj, conv_w[l], S)
        gates, decay = _gate_prep(small, a_log[l], dt_bias[l])
        o_a = _gdn(qkv, proj, gates, decay, gdn_norm[l], B, S)
        q_hm, k_att, iq_hm, ik_rot, w_s = _dsa_prep(proj, small, att_q_norm[l], att_k_norm[l], B, S)
        o_b = _dsa(iq_hm, q_hm, w_s, ik_rot, k_att, proj, B, S, topk)
        wo = w_out[l].astype(bf)
        x2 = _matmul([(o_a, wo[:GDN_V]), (o_b, wo[GDN_V:])], res=x2, tn=1024, name="out_proj")
        hm = _rmsnorm(mem.reshape(B * M, D), norm_mem_kv[l])
        mkv = _matmul([(hm, w_mem_kv[l].astype(bf))], name="mem_kv")
        x2, hf = _mem_block(x2, mkv, norm_mem_q[l], w_mem_q[l].astype(bf), mem_q_norm[l], mem_k_norm[l],
                            w_mem_o[l].astype(bf), norm_ffn[l], B, S, M)
        mid = _gateup(hf, w_gate[l], w_up[l])
        x2 = _matmul([(mid, w_down[l].astype(bf))], res=x2, tm=1024, tn=512,
                     single_buffer_rows=True, name="ffn_down")
    return x2.reshape(B, S, D)
```
